```python
import jax, jax.numpy as jnp
from jax import lax
import numpy as np

D_MODEL = 1024
BATCH = 8
SEQ = 2048
DEPTH = 4

CHUNK = 64
EPS = 1e-6
M_HEADS = 4
M_V = D_MODEL // 8
M_QK = M_V // 2
M_CONV = 4
G_HEADS = 4
G_V = D_MODEL // 16
G_QK = G_V // 2
G_RANK = 16
G_NORMALIZER = 16.0
R_HEADS = 4
R_V = D_MODEL // 16
R_QK = R_V // 2
ROPE_BASE = 10000.0
MAX_OFFSET = 4096
D_FF = 4 * D_MODEL

M_QK_W = M_HEADS * M_QK
M_V_W = M_HEADS * M_V
G_QK_W = G_HEADS * G_QK
G_V_W = G_HEADS * G_V
R_QK_W = R_HEADS * R_QK
R_V_W = R_HEADS * R_V
MIX = M_V_W + G_V_W + R_V_W
IN_SIZES = (M_QK_W, M_QK_W, M_V_W, M_HEADS, M_HEADS, M_V_W,
            G_QK_W, G_QK_W, G_V_W, G_RANK, G_V_W,
            R_QK_W, R_QK_W, R_V_W, R_V_W)
IN_COLS = sum(IN_SIZES)

kernel_name = 'hybrid_mlstm_gla_retention_sandwich'


def rmsnorm(x, g):
    xf = x.astype(jnp.float32)
    y = xf * lax.rsqrt(jnp.mean(xf * xf, axis=-1, keepdims=True) + EPS)
    return (y * g.astype(jnp.float32)).astype(x.dtype)


def head_rmsnorm(y, g, heads):
    B, S, C = y.shape
    yh = y.reshape(B, S, heads, C // heads)
    yh = yh * lax.rsqrt(jnp.mean(yh * yh, axis=-1, keepdims=True) + EPS)
    return yh.reshape(B, S, C) * g.astype(jnp.float32)


def causal_depthwise_conv(x, w):
    K, C = w.shape
    return lax.conv_general_dilated(x, w[:, None, :], window_strides=(1,), padding=[(K - 1, 0)],
                                    dimension_numbers=('NWC', 'WIO', 'NWC'), feature_group_count=C)


def to_chunks(x, heads):
    B, S, C = x.shape
    return x.reshape(B, S // CHUNK, CHUNK, heads, C // heads).transpose(0, 3, 1, 2, 4)


def from_chunks(y):
    B, H, N, L, d = y.shape
    return y.transpose(0, 2, 3, 1, 4).reshape(B, N * L, H * d)


def apply_rotary(x, cos, sin, heads):
    B, S, C = x.shape
    xh = x.reshape(B, S, heads, C // heads)
    x1, x2 = jnp.split(xh, 2, axis=-1)
    out = jnp.concatenate([x1 * cos - x2 * sin, x1 * sin + x2 * cos], axis=-1)
    return out.reshape(B, S, C)


def chunk_state_scan(decay, update):
    def step(state, inp):
        a, u = inp
        return a * state + u, state
    _, prev = lax.scan(step, jnp.zeros_like(update[:, :, 0]),
                       (jnp.moveaxis(decay, 2, 0), jnp.moveaxis(update, 2, 0)))
    return jnp.moveaxis(prev, 0, 2)


def chunked_gated_linear_attention(q, k, v, log_decay):
    L = q.shape[3]
    causal = jnp.tril(jnp.ones((L, L), dtype=bool))
    b = jnp.cumsum(log_decay, axis=3)
    b_last = b[:, :, :, -1:, :]
    rel = b[:, :, :, :, None, :] - b[:, :, :, None, :, :]
    decay = jnp.exp(jnp.where(causal[:, :, None], rel, -jnp.inf))
    if log_decay.shape[-1] == 1:
        scores = jnp.einsum('bhnid,bhnjd->bhnij', q, k) * decay[..., 0]
    else:
        scores = jnp.einsum('bhnid,bhnjd,bhnijd->bhnij', q, k, decay)
    intra = jnp.einsum('bhnij,bhnje->bhnie', scores, v)
    update = jnp.einsum('bhnjd,bhnje->bhnde', k * jnp.exp(b_last - b), v)
    prev = chunk_state_scan(jnp.exp(b_last[:, :, :, 0, :])[..., None], update)
    inter = jnp.einsum('bhnid,bhnde->bhnie', q * jnp.exp(b), prev)
    return intra + inter


def mlstm_chunked(q, k, v, i_pre, f_pre):
    L = q.shape[3]
    causal = jnp.tril(jnp.ones((L, L), dtype=bool))
    b = jnp.cumsum(jax.nn.log_sigmoid(f_pre), axis=3)
    b_last = b[..., -1]
    tail = b_last[..., None] - b + i_pre
    a = jnp.max(tail, axis=-1)
    wt = jnp.exp(tail - a[..., None])
    upd_c = jnp.einsum('bhnj,bhnjd,bhnje->bhnde', wt, k, v)
    upd_n = jnp.einsum('bhnj,bhnjd->bhnd', wt, k)

    def step(carry, inp):
        c, n, m = carry
        bl, an, uc, un = inp
        m_new = jnp.maximum(bl + m, an)
        s_old = jnp.exp(bl + m - m_new)
        s_new = jnp.exp(an - m_new)
        c_new = s_old[..., None, None] * c + s_new[..., None, None] * uc
        n_new = s_old[..., None] * n + s_new[..., None] * un
        return (c_new, n_new, m_new), (c, n, m)

    init = (jnp.zeros_like(upd_c[:, :, 0]), jnp.zeros_like(upd_n[:, :, 0]), jnp.zeros_like(a[:, :, 0]))
    _, (prev_c, prev_n, prev_m) = lax.scan(
        step, init, (jnp.moveaxis(b_last, 2, 0), jnp.moveaxis(a, 2, 0),
                     jnp.moveaxis(upd_c, 2, 0), jnp.moveaxis(upd_n, 2, 0)))
    prev_c = jnp.moveaxis(prev_c, 0, 2)
    prev_n = jnp.moveaxis(prev_n, 0, 2)
    prev_m = jnp.moveaxis(prev_m, 0, 2)

    d_log = b[..., :, None] - b[..., None, :] + i_pre[..., None, :]
    d_log = jnp.where(causal, d_log, -jnp.inf)
    inter_log = b + prev_m[..., None]
    m_i = jnp.maximum(jnp.max(d_log, axis=-1), inter_log)
    p = jnp.exp(d_log - m_i[..., None]) * jnp.einsum('bhnid,bhnjd->bhnij', q, k)
    inter_scale = jnp.exp(inter_log - m_i)
    num = (jnp.einsum('bhnij,bhnje->bhnie', p, v)
           + inter_scale[..., None] * jnp.einsum('bhnid,bhnde->bhnie', q, prev_c))
    den = jnp.sum(p, axis=-1) + inter_scale * jnp.einsum('bhnid,bhnd->bhni', q, prev_n)
    return num / jnp.maximum(jnp.abs(den), jnp.exp(-m_i))[..., None]


def hybrid_mixer(u, cos, sin, w_in, conv_w, i_bias, f_bias, m_norm, g_w_up, g_bias, g_norm, r_norm, w_out):
    f32 = jnp.float32
    proj = jnp.einsum('bsd,dc->bsc', u, w_in).astype(f32)
    (m_q, m_k, m_v, m_i, m_f, m_o, g_q, g_k, g_v, g_a, g_g,
     r_q, r_k, r_v, r_g) = jnp.split(proj, [int(c) for c in np.cumsum(IN_SIZES)[:-1]], axis=-1)

    qk = jax.nn.silu(causal_depthwise_conv(jnp.concatenate([m_q, m_k], axis=-1), conv_w.astype(f32)))
    mq, mk = jnp.split(qk, 2, axis=-1)
    h_m = mlstm_chunked(to_chunks(mq, M_HEADS), to_chunks(mk * (M_QK ** -0.5), M_HEADS),
                        to_chunks(m_v, M_HEADS),
                        to_chunks(m_i + i_bias.astype(f32), M_HEADS)[..., 0],
                        to_chunks(m_f + f_bias.astype(f32), M_HEADS)[..., 0])
    y_m = head_rmsnorm(from_chunks(h_m), m_norm, M_HEADS) * jax.nn.sigmoid(m_o)

    log_alpha = jax.nn.log_sigmoid(g_a @ g_w_up.astype(f32) + g_bias.astype(f32)) / G_NORMALIZER
    h_g = chunked_gated_linear_attention(to_chunks(g_q * (G_QK ** -0.5), G_HEADS), to_chunks(g_k, G_HEADS),
                                         to_chunks(g_v, G_HEADS), to_chunks(log_alpha, G_HEADS))
    y_g = head_rmsnorm(from_chunks(h_g), g_norm, G_HEADS) * jax.nn.silu(g_g)

    rq = to_chunks(apply_rotary(r_q, cos, sin, R_HEADS), R_HEADS)
    rk = to_chunks(apply_rotary(r_k, cos, sin, R_HEADS) * (R_QK ** -0.5), R_HEADS)
    log_gamma = jnp.log1p(-jnp.exp2(-5.0 - jnp.arange(R_HEADS, dtype=f32)))
    B, H, N, L, _ = rq.shape
    log_decay = jnp.broadcast_to(log_gamma[None, :, None, None, None], (B, H, N, L, 1))
    h_r = chunked_gated_linear_attention(rq, rk, to_chunks(r_v, R_HEADS), log_decay)
    y_r = head_rmsnorm(from_chunks(h_r), r_norm, R_HEADS) * jax.nn.silu(r_g)

    y = jnp.concatenate([y_m, y_g, y_r], axis=-1).astype(u.dtype)
    return jnp.einsum('bsc,cd->bsd', y, w_out)


def squared_relu_mlp(u, w1, w2):
    hdn = jnp.square(jax.nn.relu(jnp.einsum('bsd,df->bsf', u, w1)))
    return jnp.einsum('bsf,fd->bsd', hdn, w2)


def setup_inputs(seed: int = 0) -> dict:
    key = jax.random.key(seed)
    ks = jax.random.split(key, 20)
    f32 = jnp.float32

    def nrm(k, shape, scale):
        return jax.random.normal(k, shape, f32) * scale

    def gain(k, width):
        return 1.0 + nrm(k, (DEPTH, width), 0.02)

    x = nrm(ks[0], (BATCH, SEQ, D_MODEL), 1.0)
    offsets = jax.random.randint(ks[1], (BATCH, 1), 0, MAX_OFFSET, dtype=jnp.int32)
    positions = (offsets + jnp.arange(SEQ, dtype=jnp.int32)[None, :]).astype(jnp.int32)
    return {
        'x': x,
        'positions': positions,
        'norm_pre_mix': gain(ks[2], D_MODEL),
        'norm_post_mix': gain(ks[3], D_MODEL),
        'norm_pre_ffn': gain(ks[4], D_MODEL),
        'norm_post_ffn': gain(ks[5], D_MODEL),
        'w_in': nrm(ks[6], (DEPTH, D_MODEL, IN_COLS), D_MODEL ** -0.5),
        'mlstm_conv_w': nrm(ks[7], (DEPTH, M_CONV, 2 * M_QK_W), M_CONV ** -0.5),
        'mlstm_i_bias': nrm(ks[8], (DEPTH, M_HEADS), 0.1),
        'mlstm_f_bias': jnp.linspace(3.0, 6.0, M_HEADS, dtype=f32)[None, :] + nrm(ks[9], (DEPTH, M_HEADS), 0.1),
        'mlstm_norm': gain(ks[10], M_V_W),
        'gla_w_up': nrm(ks[11], (DEPTH, G_RANK, G_QK_W), G_RANK ** -0.5),
        'gla_gate_bias': nrm(ks[12], (DEPTH, G_QK_W), 0.1),
        'gla_norm': gain(ks[13], G_V_W),
        'ret_norm': gain(ks[14], R_V_W),
        'w_out': nrm(ks[15], (DEPTH, MIX, D_MODEL), MIX ** -0.5),
        'w_ff1': nrm(ks[16], (DEPTH, D_MODEL, D_FF), D_MODEL ** -0.5),
        'w_ff2': nrm(ks[17], (DEPTH, D_FF, D_MODEL), D_FF ** -0.5),
    }


def reference(x, positions, norm_pre_mix, norm_post_mix, norm_pre_ffn, norm_post_ffn, w_in,
              mlstm_conv_w, mlstm_i_bias, mlstm_f_bias, mlstm_norm, gla_w_up, gla_gate_bias,
              gla_norm, ret_norm, w_out, w_ff1, w_ff2):
    inv_freq = ROPE_BASE ** (-jnp.arange(0, R_QK, 2, dtype=jnp.float32) / R_QK)
    ang = positions.astype(jnp.float32)[..., None] * inv_freq
    cos = jnp.cos(ang)[:, :, None, :]
    sin = jnp.sin(ang)[:, :, None, :]

    h = x
    for l in range(DEPTH):
        mixed = hybrid_mixer(rmsnorm(h, norm_pre_mix[l]), cos, sin, w_in[l], mlstm_conv_w[l],
                             mlstm_i_bias[l], mlstm_f_bias[l], mlstm_norm[l], gla_w_up[l],
                             gla_gate_bias[l], gla_norm[l], ret_norm[l], w_out[l])
        h = h + rmsnorm(mixed, norm_post_mix[l])
        ff = squared_relu_mlp(rmsnorm(h, norm_pre_ffn[l]), w_ff1[l], w_ff2[l])
        h = h + rmsnorm(ff, norm_post_ffn[l])
    return h
```

```python
import functools

import numpy as np
import jax
import jax.numpy as jnp
from jax import lax
from jax.experimental import pallas as pl
from jax.experimental.pallas import tpu as pltpu

D_MODEL = 1024
DEPTH = 4
CHUNK = 64
EPS = 1e-6
M_HEADS, M_V, M_QK, M_CONV = 4, 128, 64, 4
G_HEADS, G_V, G_QK, G_RANK = 4, 64, 32, 16
G_NORMALIZER = 16.0
R_HEADS, R_V, R_QK = 4, 64, 32
ROPE_BASE = 10000.0
D_FF = 4 * D_MODEL

M_QK_W, M_V_W = M_HEADS * M_QK, M_HEADS * M_V
G_QK_W, G_V_W = G_HEADS * G_QK, G_HEADS * G_V
R_QK_W, R_V_W = R_HEADS * R_QK, R_HEADS * R_V
IN_SIZES = (M_QK_W, M_QK_W, M_V_W, M_HEADS, M_HEADS, M_V_W,
            G_QK_W, G_QK_W, G_V_W, G_RANK, G_V_W,
            R_QK_W, R_QK_W, R_V_W, R_V_W)
IN_OFFS = tuple(int(v) for v in np.cumsum((0,) + IN_SIZES))

LANES = 128
C_MQK = 0
C_MV = C_MQK + 2 * M_QK_W
C_MO = C_MV + M_V_W
C_GATE = C_MO + M_V_W
C_GQ = C_GATE + LANES
C_GK = C_GQ + G_QK_W
C_GV = C_GK + G_QK_W
C_GG = C_GV + G_V_W
C_RQ = C_GG + G_V_W
C_RK = C_RQ + R_QK_W
C_RV = C_RK + R_QK_W
C_RG = C_RV + R_V_W
P_COLS = C_RG + R_V_W
GATE_I, GATE_F, GATE_A = 0, M_HEADS, 2 * M_HEADS

TS = 256
TM = 512
FF_BLK = 1024
N_LEVELS = 6
VMEM_LIMIT = 52 * 1024 * 1024

_HI = lax.Precision.HIGHEST
_NT = (((1,), (1,)), ((), ()))
_TN = (((0,), (0,)), ((), ()))


def _bf(x):
    return x.astype(jnp.bfloat16)


def _rms(x, g):
    return x * lax.rsqrt(jnp.mean(x * x, axis=-1, keepdims=True) + EPS) * g


def _log_sigmoid(x):
    return jnp.minimum(x, 0.0) - jnp.log1p(jnp.exp(-jnp.abs(x)))


def _sigmoid(x):
    return 1.0 / (1.0 + jnp.exp(-x))


def _gla_constants():
    L = CHUNK
    T = np.zeros((N_LEVELS + 2, L, L), np.float32)
    M = np.zeros((N_LEVELS, L, L), np.float32)
    for li in range(N_LEVELS):
        s = L >> (li + 1)
        for i in range(L):
            seg = i // s
            if seg % 2 == 1:
                T[li, i, seg * s:i + 1] = 1.0
            else:
                T[li, i, i + 1:seg * s + s] = 1.0
            for j in range(L):
                if i // (2 * s) == j // (2 * s) and (i // s) % 2 == 1 and (j // s) % 2 == 0:
                    M[li, i, j] = 1.0
    for i in range(L):
        T[N_LEVELS, i, :i + 1] = 1.0
        T[N_LEVELS + 1, i, i + 1:] = 1.0
    return T.reshape((N_LEVELS + 2) * L, L), M


def _ret_constants():
    L = CHUNK
    log_gamma = np.log1p(-np.exp2(-5.0 - np.arange(R_HEADS, dtype=np.float64)))
    i = np.arange(L)
    rel = (i[:, None] - i[None, :]).astype(np.float64)
    D = np.where(rel[None] >= 0, np.exp(log_gamma[:, None, None] * rel[None]), 0.0)
    lane_head = np.repeat(np.arange(R_HEADS), R_QK)
    qdec = np.exp(log_gamma[lane_head][None, :] * (i[:, None] + 1.0))
    kdec = np.exp(log_gamma[lane_head][None, :] * (L - 1.0 - i[:, None]))
    g_chunk = np.exp(log_gamma * L)
    return (D.astype(np.float32), qdec.astype(np.float32), kdec.astype(np.float32),
            tuple(float(v) for v in g_chunk))


def _rope_kernel(pos_ref, invf_ref, sgn_ref, cos_ref, sin_ref):
    ang = pos_ref[0].astype(jnp.float32) * invf_ref[...]
    cos_ref[0] = jnp.cos(ang)
    sin_ref[0] = jnp.sin(ang) * sgn_ref[...]


def _rope_tables(positions):
    B, S = positions.shape
    r = np.arange(LANES) % R_QK
    inv_freq = ROPE_BASE ** (-jnp.arange(0, R_QK, 2, dtype=jnp.float32) / R_QK)
    invf = jnp.tile(inv_freq, LANES // (R_QK // 2))[None, :]
    sgn = jnp.asarray(np.where(r < R_QK // 2, -1.0, 1.0).astype(np.float32)[None, :])
    return pl.pallas_call(
        _rope_kernel,
        grid=(B,),
        in_specs=[pl.BlockSpec((1, S, 1), lambda b: (b, 0, 0)),
                  pl.BlockSpec((1, LANES), lambda b: (0, 0)),
                  pl.BlockSpec((1, LANES), lambda b: (0, 0))],
        out_specs=[pl.BlockSpec((1, S, LANES), lambda b: (b, 0, 0)),
                   pl.BlockSpec((1, S, LANES), lambda b: (b, 0, 0))],
        out_shape=[jax.ShapeDtypeStruct((B, S, LANES), jnp.float32)] * 2,
        name="rope_tables",
    )(positions[:, :, None], invf, sgn)


def _mixer_kernel(gchunk, h_ref, cos_ref, sin_ref, gpre_ref, gpost_ref, win_ref, wout_ref,
                  convw_ref, ifb_ref, wup_ref, gab_ref, mnorm_ref, gnorm_ref, rnorm_ref,
                  t_ref, lm_ref, tri_ref, d_ref, qdec_ref, kdec_ref,
                  out_ref,
                  proj_ref, qkpad_ref, la_ref, y_ref, mc_ref, mn_ref, mm_ref, gs_ref, rs_ref):
    f32 = jnp.float32
    t = pl.program_id(1)

    @pl.when(t == 0)
    def _():
        qkpad_ref[0:8, :] = jnp.zeros((8, 2 * M_QK_W), f32)
        mc_ref[...] = jnp.zeros_like(mc_ref)
        mn_ref[...] = jnp.zeros_like(mn_ref)
        mm_ref[...] = jnp.zeros_like(mm_ref)
        gs_ref[...] = jnp.zeros_like(gs_ref)
        rs_ref[...] = jnp.zeros_like(rs_ref)

    h_in = h_ref[0]
    u = _bf(_rms(h_in, gpre_ref[...]))
    for lo in range(0, P_COLS, 640):
        proj_ref[:, lo:lo + 640] = jnp.dot(u, win_ref[:, lo:lo + 640], preferred_element_type=f32)

    qkpad_ref[8:8 + TS, :] = proj_ref[:, C_MQK:C_MQK + 2 * M_QK_W]
    conv = jnp.zeros((TS, 2 * M_QK_W), f32)
    for j in range(M_CONV):
        conv = conv + convw_ref[j:j + 1, :] * qkpad_ref[5 + j:5 + j + TS, :]
    qkpad_ref[0:8, :] = qkpad_ref[TS:TS + 8, :]
    qk = conv * _sigmoid(conv)
    lane_qk = lax.broadcasted_iota(jnp.int32, (1, 2 * M_QK_W), 1)
    proj_ref[:, C_MQK:C_MQK + 2 * M_QK_W] = qk * jnp.where(lane_qk < M_QK_W, 1.0, M_QK ** -0.5)

    graw = proj_ref[:, C_GATE:C_GATE + LANES]
    gb = graw + ifb_ref[...]
    lane_g = lax.broadcasted_iota(jnp.int32, (TS, LANES), 1)
    proj_ref[:, C_GATE:C_GATE + LANES] = jnp.where(lane_g < GATE_F, gb, _log_sigmoid(gb))
    a_pre = jnp.dot(_bf(graw), wup_ref[...], preferred_element_type=f32) + gab_ref[...]
    la_ref[...] = _log_sigmoid(a_pre) * (1.0 / G_NORMALIZER)

    proj_ref[:, C_GQ:C_GQ + G_QK_W] = proj_ref[:, C_GQ:C_GQ + G_QK_W] * (G_QK ** -0.5)
    cosf = cos_ref[0]
    sins = sin_ref[0]
    lane_r = lax.broadcasted_iota(jnp.int32, (TS, LANES), 1)
    first_half = (lane_r % R_QK) < (R_QK // 2)
    for col, scale in ((C_RQ, 1.0), (C_RK, R_QK ** -0.5)):
        xr = proj_ref[:, col:col + R_QK_W]
        swapped = jnp.where(first_half, pltpu.roll(xr, LANES - R_QK // 2, axis=1),
                            pltpu.roll(xr, R_QK // 2, axis=1))
        proj_ref[:, col:col + R_QK_W] = (xr * cosf + swapped * sins) * scale

    row_i = lax.broadcasted_iota(jnp.int32, (CHUNK, CHUNK), 0)
    col_j = lax.broadcasted_iota(jnp.int32, (CHUNK, CHUNK), 1)
    causal = row_i >= col_j
    eye = (row_i == col_j).astype(f32)

    def chunk_body(c, carry):
        r0 = pl.multiple_of(c * CHUNK, CHUNK)
        rows = pl.ds(r0, CHUNK)

        gblk = proj_ref[rows, C_GATE:C_GATE + LANES]
        bcum = jnp.dot(tri_ref[...], gblk, precision=_HI, preferred_element_type=f32)
        lane_c = lax.broadcasted_iota(jnp.int32, (CHUNK, LANES), 1)
        xg = jnp.where(lane_c < GATE_F, gblk, bcum)
        xgt = xg.T
        for hd in range(M_HEADS):
            q = proj_ref[rows, C_MQK + hd * M_QK:C_MQK + (hd + 1) * M_QK]
            k = proj_ref[rows, C_MQK + M_QK_W + hd * M_QK:C_MQK + M_QK_W + (hd + 1) * M_QK]
            v = proj_ref[rows, C_MV + hd * M_V:C_MV + (hd + 1) * M_V]
            ip_col = xg[:, GATE_I + hd:GATE_I + hd + 1]
            b_col = xg[:, GATE_F + hd:GATE_F + hd + 1]
            ip_row = xgt[GATE_I + hd:GATE_I + hd + 1, :]
            b_row = xgt[GATE_F + hd:GATE_F + hd + 1, :]
            b_last = b_col[CHUNK - 1:CHUNK, :]
            tail = b_last - b_col + ip_col
            a_max = jnp.max(tail, axis=0, keepdims=True)
            kw = k * jnp.exp(tail - a_max)
            upd_c = lax.dot_general(_bf(kw), _bf(v), _TN, preferred_element_type=f32)
            upd_n = jnp.sum(kw, axis=0, keepdims=True)

            c_prev = mc_ref[hd]
            n_prev = mn_ref[hd]
            m_prev = mm_ref[hd]
            m_new = jnp.maximum(b_last + m_prev, a_max)
            s_old = jnp.exp(b_last + m_prev - m_new)
            s_new = jnp.exp(a_max - m_new)
            mc_ref[hd] = s_old * c_prev + s_new * upd_c
            mn_ref[hd] = s_old * n_prev + s_new * upd_n
            mm_ref[hd] = m_new

            d_log = jnp.where(causal, b_col - b_row + ip_row, -jnp.inf)
            inter_log = b_col + m_prev
            m_i = jnp.maximum(jnp.max(d_log, axis=1, keepdims=True), inter_log)
            qk_s = lax.dot_general(_bf(q), _bf(k), _NT, preferred_element_type=f32)
            p = jnp.exp(d_log - m_i) * qk_s
            isc = jnp.exp(inter_log - m_i)
            num = (jnp.dot(_bf(p), _bf(v), preferred_element_type=f32)
                   + isc * jnp.dot(_bf(q), _bf(c_prev), preferred_element_type=f32))
            den = (jnp.sum(p, axis=1, keepdims=True)
                   + isc * jnp.sum(q * n_prev, axis=1, keepdims=True))
            hm = num / jnp.maximum(jnp.abs(den), jnp.exp(-m_i))
            o_gate = _sigmoid(proj_ref[rows, C_MO + hd * M_V:C_MO + (hd + 1) * M_V])
            y_ref[rows, hd * M_V:(hd + 1) * M_V] = (
                _rms(hm, mnorm_ref[:, hd * M_V:(hd + 1) * M_V]) * o_gate)

        la = la_ref[rows, :]
        pw = jnp.exp(jnp.dot(t_ref[...], la, precision=_HI, preferred_element_type=f32))
        gq = proj_ref[rows, C_GQ:C_GQ + G_QK_W]
        gk = proj_ref[rows, C_GK:C_GK + G_QK_W]
        lvl = [(_bf(gq * pw[li * CHUNK:(li + 1) * CHUNK]), _bf(gk * pw[li * CHUNK:(li + 1) * CHUNK]))
               for li in range(N_LEVELS)]
        gq_b, gk_b = _bf(gq), _bf(gk)
        pb = pw[N_LEVELS * CHUNK:(N_LEVELS + 1) * CHUNK]
        q_in = _bf(gq * pb)
        k_out = _bf(gk * pw[(N_LEVELS + 1) * CHUNK:(N_LEVELS + 2) * CHUNK])
        for hd in range(G_HEADS):
            sl = slice(hd * G_QK, (hd + 1) * G_QK)
            v = _bf(proj_ref[rows, C_GV + hd * G_V:C_GV + (hd + 1) * G_V])
            sc = eye * lax.dot_general(gq_b[:, sl], gk_b[:, sl], _NT, preferred_element_type=f32)
            for li in range(N_LEVELS):
                sc = sc + lm_ref[li] * lax.dot_general(lvl[li][0][:, sl], lvl[li][1][:, sl], _NT,
                                                       preferred_element_type=f32)
            st_prev = gs_ref[hd]
            hg = (jnp.dot(_bf(sc), v, preferred_element_type=f32)
                  + lax.dot_general(q_in[:, sl], _bf(st_prev), _NT, preferred_element_type=f32))
            gs_ref[hd] = (st_prev * pb[CHUNK - 1:CHUNK, sl]
                          + lax.dot_general(v, k_out[:, sl], _TN, preferred_element_type=f32))
            gg = proj_ref[rows, C_GG + hd * G_V:C_GG + (hd + 1) * G_V]
            y_ref[rows, M_V_W + hd * G_V:M_V_W + (hd + 1) * G_V] = (
                _rms(hg, gnorm_ref[:, hd * G_V:(hd + 1) * G_V]) * (gg * _sigmoid(gg)))

        rq = proj_ref[rows, C_RQ:C_RQ + R_QK_W]
        rk = proj_ref[rows, C_RK:C_RK + R_QK_W]
        rq_b, rk_b = _bf(rq), _bf(rk)
        rq_in = _bf(rq * qdec_ref[...])
        rk_out = _bf(rk * kdec_ref[...])
        for hd in range(R_HEADS):
            sl = slice(hd * R_QK, (hd + 1) * R_QK)
            v = _bf(proj_ref[rows, C_RV + hd * R_V:C_RV + (hd + 1) * R_V])
            sc = d_ref[hd] * lax.dot_general(rq_b[:, sl], rk_b[:, sl], _NT, preferred_element_type=f32)
            st_prev = rs_ref[hd]
            hr = (jnp.dot(_bf(sc), v, preferred_element_type=f32)
                  + lax.dot_general(rq_in[:, sl], _bf(st_prev), _NT, preferred_element_type=f32))
            rs_ref[hd] = (st_prev * gchunk[hd]
                          + lax.dot_general(v, rk_out[:, sl], _TN, preferred_element_type=f32))
            rg = proj_ref[rows, C_RG + hd * R_V:C_RG + (hd + 1) * R_V]
            y_ref[rows, M_V_W + G_V_W + hd * R_V:M_V_W + G_V_W + (hd + 1) * R_V] = (
                _rms(hr, rnorm_ref[:, hd * R_V:(hd + 1) * R_V]) * (rg * _sigmoid(rg)))
        return carry

    lax.fori_loop(0, TS // CHUNK, chunk_body, 0)

    mixed = jnp.dot(_bf(y_ref[...]), wout_ref[...], preferred_element_type=f32)
    out_ref[0] = h_in + _rms(mixed, gpost_ref[...])


def _const_spec(shape):
    nd = len(shape)
    return pl.BlockSpec(shape, lambda b, t, _nd=nd: (0,) * _nd)


def _mixer_layer(h, cosf, sins, gpre, gpost, win, wout, convw, ifb, wup, gab, mnorm, gnorm, rnorm,
                 consts):
    B, S, D = h.shape
    t_mat, lvl_mask, tri, dmat, qdec, kdec, gchunk = consts
    tile = pl.BlockSpec((1, TS, D), lambda b, t: (b, t, 0))
    rope = pl.BlockSpec((1, TS, LANES), lambda b, t: (b, t, 0))
    operands = (h, cosf, sins, gpre, gpost, win, wout, convw, ifb, wup, gab, mnorm, gnorm, rnorm,
                t_mat, lvl_mask, tri, dmat, qdec, kdec)
    in_specs = [tile, rope, rope] + [_const_spec(o.shape) for o in operands[3:]]
    return pl.pallas_call(
        functools.partial(_mixer_kernel, gchunk),
        grid=(B, S // TS),
        in_specs=in_specs,
        out_specs=tile,
        out_shape=jax.ShapeDtypeStruct(h.shape, h.dtype),
        scratch_shapes=[
            pltpu.VMEM((TS, P_COLS), jnp.float32),
            pltpu.VMEM((TS + 8, 2 * M_QK_W), jnp.float32),
            pltpu.VMEM((TS, LANES), jnp.float32),
            pltpu.VMEM((TS, D_MODEL), jnp.float32),
            pltpu.VMEM((M_HEADS, M_QK, M_V), jnp.float32),
            pltpu.VMEM((M_HEADS, 1, M_QK), jnp.float32),
            pltpu.VMEM((M_HEADS, 1, 1), jnp.float32),
            pltpu.VMEM((G_HEADS, G_V, G_QK), jnp.float32),
            pltpu.VMEM((R_HEADS, R_V, R_QK), jnp.float32),
        ],
        compiler_params=pltpu.CompilerParams(
            dimension_semantics=("arbitrary", "arbitrary"), vmem_limit_bytes=VMEM_LIMIT),
        name="mixer_layer",
    )(*operands)


def _ffn_kernel(h_ref, gpre_ref, gpost_ref, w1_ref, w2_ref, out_ref):
    f32 = jnp.float32
    h_in = h_ref[...]
    u = _bf(_rms(h_in, gpre_ref[...]))
    acc = jnp.zeros((TM, D_MODEL), f32)
    for lo in range(0, D_FF, FF_BLK):
        hid = jnp.dot(u, w1_ref[:, lo:lo + FF_BLK], preferred_element_type=f32)
        hid = jnp.square(jnp.maximum(hid, 0.0))
        acc = acc + jnp.dot(_bf(hid), w2_ref[lo:lo + FF_BLK, :], preferred_element_type=f32)
    out_ref[...] = h_in + _rms(acc, gpost_ref[...])


def _ffn_layer(h2, gpre, gpost, w1, w2):
    n_tok, D = h2.shape
    tile = pl.BlockSpec((TM, D), lambda i: (i, 0))
    return pl.pallas_call(
        _ffn_kernel,
        grid=(n_tok // TM,),
        in_specs=[tile,
                  pl.BlockSpec((1, D), lambda i: (0, 0)),
                  pl.BlockSpec((1, D), lambda i: (0, 0)),
                  pl.BlockSpec((D, D_FF), lambda i: (0, 0)),
                  pl.BlockSpec((D_FF, D), lambda i: (0, 0))],
        out_specs=tile,
        out_shape=jax.ShapeDtypeStruct(h2.shape, h2.dtype),
        compiler_params=pltpu.CompilerParams(
            dimension_semantics=("arbitrary",), vmem_limit_bytes=VMEM_LIMIT),
        name="ffn_layer",
    )(h2, gpre, gpost, w1, w2)


def _pack_w_in(w_in):
    def seg(i):
        return w_in[:, :, IN_OFFS[i]:IN_OFFS[i + 1]]
    (m_q, m_k, m_v, m_i, m_f, m_o, g_q, g_k, g_v, g_a, g_g, r_q, r_k, r_v, r_g) = [seg(i) for i in range(15)]
    pad = jnp.zeros(w_in.shape[:2] + (LANES - 2 * M_HEADS - G_RANK,), w_in.dtype)
    packed = jnp.concatenate([m_q, m_k, m_v, m_o, m_i, m_f, g_a, pad,
                              g_q, g_k, g_v, g_g, r_q, r_k, r_v, r_g], axis=-1)
    return packed.astype(jnp.bfloat16)


def kernel(x, positions, norm_pre_mix, norm_post_mix, norm_pre_ffn, norm_post_ffn, w_in, mlstm_conv_w, mlstm_i_bias, mlstm_f_bias, mlstm_norm, gla_w_up, gla_gate_bias, gla_norm, ret_norm, w_out, w_ff1, w_ff2):
    B, S, D = x.shape
    f32 = jnp.float32
    t_np, m_np = _gla_constants()
    d_np, qdec_np, kdec_np, gchunk = _ret_constants()
    tri_np = t_np[N_LEVELS * CHUNK:(N_LEVELS + 1) * CHUNK]
    consts = (jnp.asarray(t_np), jnp.asarray(m_np), jnp.asarray(tri_np), jnp.asarray(d_np),
              jnp.asarray(qdec_np), jnp.asarray(kdec_np), gchunk)

    cosf, sins = _rope_tables(positions)

    win_p = _pack_w_in(w_in)
    wout_b = w_out.astype(jnp.bfloat16)
    w1_b = w_ff1.astype(jnp.bfloat16)
    w2_b = w_ff2.astype(jnp.bfloat16)
    ifb = jnp.concatenate([mlstm_i_bias, mlstm_f_bias,
                           jnp.zeros((DEPTH, LANES - 2 * M_HEADS), f32)], axis=-1)[:, None, :]
    wup = jnp.zeros((DEPTH, LANES, G_QK_W), f32).at[:, GATE_A:GATE_A + G_RANK, :].set(gla_w_up)
    wup = wup.astype(jnp.bfloat16)

    h = x
    for l in range(DEPTH):
        h = _mixer_layer(h, cosf, sins, norm_pre_mix[l][None], norm_post_mix[l][None],
                         win_p[l], wout_b[l], mlstm_conv_w[l], ifb[l], wup[l],
                         gla_gate_bias[l][None], mlstm_norm[l][None], gla_norm[l][None],
                         ret_norm[l][None], consts)
        h = _ffn_layer(h.reshape(B * S, D), norm_pre_ffn[l][None], norm_post_ffn[l][None],
                       w1_b[l], w2_b[l]).reshape(B, S, D)
    return h
```

```python
import functools

import numpy as np
import jax
import jax.numpy as jnp
from jax import lax
from jax.experimental import pallas as pl
from jax.experimental.pallas import tpu as pltpu

D_MODEL = 1024
DEPTH = 4
EPS = 1e-6
M_HEADS, M_V, M_QK, M_CONV = 4, 128, 64, 4
G_HEADS, G_V, G_QK, G_RANK = 4, 64, 32, 16
G_NORMALIZER = 16.0
R_HEADS, R_V, R_QK = 4, 64, 32
ROPE_BASE = 10000.0
D_FF = 4 * D_MODEL

M_QK_W, M_V_W = M_HEADS * M_QK, M_HEADS * M_V
G_QK_W, G_V_W = G_HEADS * G_QK, G_HEADS * G_V
R_QK_W, R_V_W = R_HEADS * R_QK, R_HEADS * R_V
IN_SIZES = (M_QK_W, M_QK_W, M_V_W, M_HEADS, M_HEADS, M_V_W,
            G_QK_W, G_QK_W, G_V_W, G_RANK, G_V_W,
            R_QK_W, R_QK_W, R_V_W, R_V_W)
IN_OFFS = tuple(int(v) for v in np.cumsum((0,) + IN_SIZES))

LANES = 128
C_MQK = 0
C_MV = C_MQK + 2 * M_QK_W
C_MO = C_MV + M_V_W
C_GATE = C_MO + M_V_W
C_GQ = C_GATE + LANES
C_GK = C_GQ + G_QK_W
C_GV = C_GK + G_QK_W
C_GG = C_GV + G_V_W
C_RQ = C_GG + G_V_W
C_RK = C_RQ + R_QK_W
C_RV = C_RK + R_QK_W
C_RG = C_RV + R_V_W
P_COLS = C_RG + R_V_W
GATE_I, GATE_F, GATE_A = 0, M_HEADS, 2 * M_HEADS

LC = 128
N_LEVELS = 7
TS = 512
TM = 512
FF_BLK = 1024
PROJ_BLK = 640
VMEM_LIMIT = 56 * 1024 * 1024

_NT = (((1,), (1,)), ((), ()))
f32 = jnp.float32
bf16 = jnp.bfloat16


def _bf(x):
    return x.astype(bf16)


def _dot(a, b):
    return jnp.dot(a, b, preferred_element_type=f32)


def _dot_nt(a, b):
    return lax.dot_general(a, b, _NT, preferred_element_type=f32)


def _rms(x, g):
    return x * lax.rsqrt(jnp.mean(x * x, axis=-1, keepdims=True) + EPS) * g


def _log_sigmoid(x):
    return jnp.minimum(x, 0.0) - jnp.log1p(jnp.exp(-jnp.abs(x)))


def _sigmoid(x):
    return 1.0 / (1.0 + jnp.exp(-x))


def _split3(x):
    hi = _bf(x)
    r1 = x - hi.astype(f32)
    mid = _bf(r1)
    lo = _bf(r1 - mid.astype(f32))
    return jnp.concatenate([hi, mid, lo], axis=0)


def _chunk_constants():
    L = LC
    T = np.zeros((N_LEVELS + 2, L, L), np.float32)
    M = np.zeros((N_LEVELS + 1, L, L), np.float32)
    idx = np.arange(L)
    for li in range(N_LEVELS):
        s = L >> (li + 1)
        for i in range(L):
            seg = i // s
            if seg % 2 == 1:
                T[li, i, seg * s:i + 1] = 1.0
            else:
                T[li, i, i + 1:seg * s + s] = 1.0
        same_blk = (idx[:, None] // (2 * s)) == (idx[None, :] // (2 * s))
        M[li] = same_blk & ((idx[:, None] // s) % 2 == 1) & ((idx[None, :] // s) % 2 == 0)
    M[N_LEVELS] = np.eye(L)
    for i in range(L):
        T[N_LEVELS, i, :i + 1] = 1.0
        T[N_LEVELS + 1, i, i + 1:] = 1.0
    T = T.reshape((N_LEVELS + 2) * L, L)
    return np.concatenate([T, T, T], axis=1), M


def _ret_constants():
    L = LC
    log_gamma = np.log1p(-np.exp2(-5.0 - np.arange(R_HEADS, dtype=np.float64)))
    i = np.arange(L)
    rel = (i[:, None] - i[None, :]).astype(np.float64)
    D = np.where(rel[None] >= 0, np.exp(log_gamma[:, None, None] * rel[None]), 0.0)
    lg_lane = log_gamma[np.repeat(np.arange(R_HEADS), R_QK)][None, :]
    qdec = np.exp(lg_lane * (i[:, None] + 1.0))
    kdec = np.exp(lg_lane * (L - 1.0 - i[:, None]))
    sdec = np.exp(lg_lane * L)
    return tuple(a.astype(np.float32) for a in (D, qdec, kdec, sdec))


def _head_constants():
    blk = (np.arange(G_V_W)[:, None] // G_V) == (np.arange(G_QK_W)[None, :] // G_QK)
    seg = ((np.arange(G_V_W)[:, None] // G_V) == (np.arange(G_V_W)[None, :] // G_V)) / float(G_V)
    ones_mean = np.full((M_V, M_V), 1.0 / M_V)
    return blk.astype(np.float32), seg.astype(np.float32), ones_mean.astype(np.float32)


def _rope_kernel(pos_ref, invf_ref, sgn_ref, cos_ref, sin_ref):
    ang = pos_ref[0].astype(f32) * invf_ref[...]
    cos_ref[0] = jnp.cos(ang)
    sin_ref[0] = jnp.sin(ang) * sgn_ref[...]


def _rope_tables(positions):
    B, S = positions.shape
    r = np.arange(LANES) % R_QK
    inv_freq = ROPE_BASE ** (-jnp.arange(0, R_QK, 2, dtype=f32) / R_QK)
    invf = jnp.tile(inv_freq, LANES // (R_QK // 2))[None, :]
    sgn = jnp.asarray(np.where(r < R_QK // 2, -1.0, 1.0).astype(np.float32)[None, :])
    return pl.pallas_call(
        _rope_kernel,
        grid=(B,),
        in_specs=[pl.BlockSpec((1, S, 1), lambda b: (b, 0, 0)),
                  pl.BlockSpec((1, LANES), lambda b: (0, 0)),
                  pl.BlockSpec((1, LANES), lambda b: (0, 0))],
        out_specs=[pl.BlockSpec((1, S, LANES), lambda b: (b, 0, 0)),
                   pl.BlockSpec((1, S, LANES), lambda b: (b, 0, 0))],
        out_shape=[jax.ShapeDtypeStruct((B, S, LANES), f32)] * 2,
        name="rope_tables",
    )(positions[:, :, None], invf, sgn)


def _mixer_kernel(h_ref, cos_ref, sin_ref, gpre_ref, gpost_ref, win_ref, wout_ref,
                  convw_ref, ifb_ref, wup_ref, gab_ref, mnorm_ref, gnorm_ref, rnorm_ref,
                  t_ref, lm_ref, d_ref, qdec_ref, kdec_ref, sdec_ref, blk_ref, seg_ref, onesm_ref,
                  out_ref,
                  proj_ref, qkpad_ref, la_ref, vext_ref, y_ref, mc_ref, mm_ref, gs_ref, rs_ref):
    t = pl.program_id(1)

    @pl.when(t == 0)
    def _():
        qkpad_ref[0:8, :] = jnp.zeros((8, 2 * M_QK_W), f32)
        mc_ref[...] = jnp.zeros_like(mc_ref)
        mm_ref[...] = jnp.zeros_like(mm_ref)
        gs_ref[...] = jnp.zeros_like(gs_ref)
        rs_ref[...] = jnp.zeros_like(rs_ref)
        for hd in range(M_HEADS):
            vext_ref[:, (2 * hd + 1) * M_V:(2 * hd + 2) * M_V] = jnp.ones((TS, M_V), bf16)

    h_in = h_ref[0]
    u = _bf(_rms(h_in, gpre_ref[0]))
    for lo in range(0, P_COLS, PROJ_BLK):
        proj_ref[:, lo:lo + PROJ_BLK] = _dot(u, win_ref[:, lo:lo + PROJ_BLK])

    qkpad_ref[8:8 + TS, :] = proj_ref[:, C_MQK:C_MQK + 2 * M_QK_W]
    conv = jnp.zeros((TS, 2 * M_QK_W), f32)
    for j in range(M_CONV):
        conv = conv + convw_ref[0, j:j + 1, :] * qkpad_ref[5 + j:5 + j + TS, :]
    qkpad_ref[0:8, :] = qkpad_ref[TS:TS + 8, :]
    qk = conv * _sigmoid(conv)
    lane_qk = lax.broadcasted_iota(jnp.int32, (1, 2 * M_QK_W), 1)
    proj_ref[:, C_MQK:C_MQK + 2 * M_QK_W] = qk * jnp.where(lane_qk < M_QK_W, 1.0, M_QK ** -0.5)
    for hd in range(M_HEADS):
        vext_ref[:, 2 * hd * M_V:(2 * hd + 1) * M_V] = _bf(proj_ref[:, C_MV + hd * M_V:C_MV + (hd + 1) * M_V])

    graw = proj_ref[:, C_GATE:C_GATE + LANES]
    gb = graw + ifb_ref[0]
    lane_g = lax.broadcasted_iota(jnp.int32, (TS, LANES), 1)
    proj_ref[:, C_GATE:C_GATE + LANES] = jnp.where(lane_g < GATE_F, gb, _log_sigmoid(gb))
    a_pre = _dot(_bf(graw), wup_ref[0]) + gab_ref[0]
    la_ref[...] = _log_sigmoid(a_pre) * (1.0 / G_NORMALIZER)

    proj_ref[:, C_GQ:C_GQ + G_QK_W] = proj_ref[:, C_GQ:C_GQ + G_QK_W] * (G_QK ** -0.5)
    cosf = cos_ref[0]
    sins = sin_ref[0]
    first_half = (lane_g % R_QK) < (R_QK // 2)
    for col, scale in ((C_RQ, 1.0), (C_RK, R_QK ** -0.5)):
        xr = proj_ref[:, col:col + R_QK_W]
        swapped = jnp.where(first_half, pltpu.roll(xr, LANES - R_QK // 2, axis=1),
                            pltpu.roll(xr, R_QK // 2, axis=1))
        proj_ref[:, col:col + R_QK_W] = (xr * cosf + swapped * sins) * scale

    row_i = lax.broadcasted_iota(jnp.int32, (LC, LC), 0)
    col_j = lax.broadcasted_iota(jnp.int32, (LC, LC), 1)
    causal = row_i >= col_j
    top_rows = row_i < M_QK
    left_lanes = col_j < G_V
    lane1 = lax.broadcasted_iota(jnp.int32, (1, LANES), 1)
    pair_mask = [(lane1 < M_QK).astype(f32), (lane1 >= M_QK).astype(f32)]
    head_mask = [((lane1 // G_QK) == hd).astype(f32) for hd in range(G_HEADS)]

    def head_scores(qx, kxs):
        rhs = jnp.concatenate([_bf(kx) for kx in kxs], axis=0)
        return _dot_nt(_bf(qx), rhs)

    def linear_attn(p_heads, v_all, q_in, k_out, st_ref, dec_row, gain, gate):
        vb = _bf(v_all)
        tiles = []
        for tp in range(2):
            vt = vb[:, tp * LANES:(tp + 1) * LANES]
            tiles.append(jnp.where(left_lanes, _dot(_bf(p_heads[2 * tp]), vt),
                                   _dot(_bf(p_heads[2 * tp + 1]), vt)))
        st = st_ref[...]
        hcat = jnp.concatenate(tiles, axis=1) + _dot_nt(_bf(q_in), _bf(st))
        st_ref[...] = st * dec_row + blk_ref[...] * _dot(_bf(v_all.T), _bf(k_out))
        ms = _dot(_bf(hcat * hcat), seg_ref[...])
        return hcat * lax.rsqrt(ms + EPS) * gain * (gate * _sigmoid(gate))

    def chunk_body(c, carry):
        r0 = pl.multiple_of(c * LC, LC)
        rows = pl.ds(r0, LC)

        gblk = proj_ref[rows, C_GATE:C_GATE + LANES]
        bcum = _dot(t_ref[N_LEVELS * LC:(N_LEVELS + 1) * LC, :], _split3(gblk))
        xg = jnp.where(col_j < GATE_F, gblk, bcum)
        xgt = xg.T
        for tp in range(M_HEADS // 2):
            q_t = proj_ref[rows, C_MQK + tp * LANES:C_MQK + (tp + 1) * LANES]
            k_t = proj_ref[rows, C_MQK + M_QK_W + tp * LANES:C_MQK + M_QK_W + (tp + 1) * LANES]
            k_b = _bf(k_t)
            k_tt = k_t.T
            c_prev = mc_ref[tp]
            c_prev_b = _bf(c_prev)
            w_rows, s_olds = [], []
            for e in range(2):
                hd = 2 * tp + e
                g_row = xgt[GATE_I + hd:GATE_I + hd + 1, :] - xgt[GATE_F + hd:GATE_F + hd + 1, :]
                b_col = xg[:, GATE_F + hd:GATE_F + hd + 1]
                m_prev = mm_ref[hd]
                dmat = jnp.where(causal, g_row, -jnp.inf)
                rmax = jnp.max(dmat, axis=1, keepdims=True)
                mu = jnp.maximum(rmax, m_prev)
                qm = _bf(q_t * pair_mask[e])
                p = jnp.exp(dmat - mu) * _dot_nt(qm, k_b)
                isc = jnp.exp(m_prev - mu)
                vext = vext_ref[rows, 2 * hd * M_V:(2 * hd + 2) * M_V]
                ne = _dot(_bf(p), vext) + isc * _dot(qm, c_prev_b[:, 2 * e * M_V:(2 * e + 2) * M_V])
                num = ne[:, :M_V]
                den = ne[:, M_V:]
                hm = num / jnp.maximum(jnp.abs(den), jnp.exp(-(b_col + mu)))
                ms = _dot(_bf(hm * hm), onesm_ref[...])
                o_gate = _sigmoid(proj_ref[rows, C_MO + hd * M_V:C_MO + (hd + 1) * M_V])
                y_ref[rows, hd * M_V:(hd + 1) * M_V] = _bf(
                    hm * lax.rsqrt(ms + EPS) * mnorm_ref[0, :, hd * M_V:(hd + 1) * M_V] * o_gate)
                g_max = rmax[LC - 1:LC, :]
                b_last = b_col[LC - 1:LC, :]
                a_max = b_last + g_max
                m_new = jnp.maximum(b_last + m_prev, a_max)
                s_olds.append(jnp.exp(b_last + m_prev - m_new))
                w_rows.append(jnp.exp(g_row - g_max) * jnp.exp(a_max - m_new))
                mm_ref[hd] = m_new
            kw_t = _bf(k_tt * jnp.where(top_rows, w_rows[0], w_rows[1]))
            for e in range(2):
                hd = 2 * tp + e
                rs = slice(e * M_QK, (e + 1) * M_QK)
                cs = slice(2 * e * M_V, (2 * e + 2) * M_V)
                vext = vext_ref[rows, 2 * hd * M_V:(2 * hd + 2) * M_V]
                mc_ref[tp, rs, cs] = s_olds[e] * c_prev[rs, cs] + _dot(kw_t[rs], vext)

        e_all = _dot(t_ref[...], _split3(la_ref[rows, :]))
        gq = proj_ref[rows, C_GQ:C_GQ + G_QK_W]
        gk = proj_ref[rows, C_GK:C_GK + G_QK_W]
        gk_m = [gk * head_mask[hd] for hd in range(G_HEADS)]
        full = head_scores(gq, gk_m)
        acc = [lm_ref[N_LEVELS] * full[:, hd * LC:(hd + 1) * LC] for hd in range(G_HEADS)]
        for li in range(N_LEVELS):
            pw = jnp.exp(e_all[li * LC:(li + 1) * LC])
            full = head_scores(gq * pw, [km * pw for km in gk_m])
            acc = [acc[hd] + lm_ref[li] * full[:, hd * LC:(hd + 1) * LC] for hd in range(G_HEADS)]
        pb = jnp.exp(e_all[N_LEVELS * LC:(N_LEVELS + 1) * LC])
        pk = jnp.exp(e_all[(N_LEVELS + 1) * LC:(N_LEVELS + 2) * LC])
        y_ref[rows, M_V_W:M_V_W + G_V_W] = _bf(linear_attn(
            acc, proj_ref[rows, C_GV:C_GV + G_V_W], gq * pb, gk * pk, gs_ref,
            pb[LC - 1:LC, :], gnorm_ref[0], proj_ref[rows, C_GG:C_GG + G_V_W]))

        rq = proj_ref[rows, C_RQ:C_RQ + R_QK_W]
        rk = proj_ref[rows, C_RK:C_RK + R_QK_W]
        full = head_scores(rq, [rk * head_mask[hd] for hd in range(R_HEADS)])
        p_ret = [d_ref[hd] * full[:, hd * LC:(hd + 1) * LC] for hd in range(R_HEADS)]
        y_ref[rows, M_V_W + G_V_W:M_V_W + G_V_W + R_V_W] = _bf(linear_attn(
            p_ret, proj_ref[rows, C_RV:C_RV + R_V_W], rq * qdec_ref[...], rk * kdec_ref[...], rs_ref,
            sdec_ref[...], rnorm_ref[0], proj_ref[rows, C_RG:C_RG + R_V_W]))
        return carry

    lax.fori_loop(0, TS // LC, chunk_body, 0)

    mixed = _dot(y_ref[...], wout_ref[...])
    out_ref[0] = h_in + _rms(mixed, gpost_ref[0])


def _mixer_layer(layer, h, cosf, sins, params, consts):
    B, S, D = h.shape
    tile = pl.BlockSpec((1, TS, D), lambda b, t: (b, t, 0))
    rope = pl.BlockSpec((1, TS, LANES), lambda b, t: (b, t, 0))

    def weight_spec(a):
        return pl.BlockSpec((None,) + a.shape[1:], lambda b, t: (layer, 0, 0))

    def small_spec(a):
        return pl.BlockSpec((1,) + a.shape[1:], lambda b, t: (layer, 0, 0))

    def const_spec(a):
        nd = a.ndim
        return pl.BlockSpec(a.shape, lambda b, t: (0,) * nd)

    gpre, gpost, win, wout, convw, ifb, wup, gab, mnorm, gnorm, rnorm = params
    in_specs = ([tile, rope, rope, small_spec(gpre), small_spec(gpost), weight_spec(win), weight_spec(wout)]
                + [small_spec(a) for a in (convw, ifb, wup, gab, mnorm, gnorm, rnorm)]
                + [const_spec(a) for a in consts])
    return pl.pallas_call(
        _mixer_kernel,
        grid=(B, S // TS),
        in_specs=in_specs,
        out_specs=tile,
        out_shape=jax.ShapeDtypeStruct(h.shape, h.dtype),
        scratch_shapes=[
            pltpu.VMEM((TS, P_COLS), f32),
            pltpu.VMEM((TS + 8, 2 * M_QK_W), f32),
            pltpu.VMEM((TS, LANES), f32),
            pltpu.VMEM((TS, 2 * M_V_W), bf16),
            pltpu.VMEM((TS, D_MODEL), bf16),
            pltpu.VMEM((M_HEADS // 2, 2 * M_QK, 4 * M_V), f32),
            pltpu.VMEM((M_HEADS, 1, 1), f32),
            pltpu.VMEM((G_V_W, G_QK_W), f32),
            pltpu.VMEM((R_V_W, R_QK_W), f32),
        ],
        compiler_params=pltpu.CompilerParams(
            dimension_semantics=("arbitrary", "arbitrary"), vmem_limit_bytes=VMEM_LIMIT),
        name="mixer_layer",
    )(h, cosf, sins, gpre, gpost, win, wout, convw, ifb, wup, gab, mnorm, gnorm, rnorm, *consts)


def _ffn_kernel(h_ref, gpre_ref, gpost_ref, w1_ref, w2_ref, out_ref):
    h_in = h_ref[...]
    u = _bf(_rms(h_in, gpre_ref[0]))
    acc = jnp.zeros((TM, D_MODEL), f32)
    for lo in range(0, D_FF, FF_BLK):
        hid = _dot(u, w1_ref[:, lo:lo + FF_BLK])
        hid = jnp.square(jnp.maximum(hid, 0.0))
        acc = acc + _dot(_bf(hid), w2_ref[lo:lo + FF_BLK, :])
    out_ref[...] = h_in + _rms(acc, gpost_ref[0])


def _ffn_layer(layer, h2, gpre, gpost, w1, w2):
    n_tok, D = h2.shape
    tile = pl.BlockSpec((TM, D), lambda i: (i, 0))
    return pl.pallas_call(
        _ffn_kernel,
        grid=(n_tok // TM,),
        in_specs=[tile,
                  pl.BlockSpec((1, 1, D), lambda i: (layer, 0, 0)),
                  pl.BlockSpec((1, 1, D), lambda i: (layer, 0, 0)),
                  pl.BlockSpec((None, D, D_FF), lambda i: (layer, 0, 0)),
                  pl.BlockSpec((None, D_FF, D), lambda i: (layer, 0, 0))],
        out_specs=tile,
        out_shape=jax.ShapeDtypeStruct(h2.shape, h2.dtype),
        compiler_params=pltpu.CompilerParams(
            dimension_semantics=("arbitrary",), vmem_limit_bytes=VMEM_LIMIT),
        name="ffn_layer",
    )(h2, gpre, gpost, w1, w2)


def _pack_w_in(w_in):
    def seg(i):
        return w_in[:, :, IN_OFFS[i]:IN_OFFS[i + 1]]
    (m_q, m_k, m_v, m_i, m_f, m_o, g_q, g_k, g_v, g_a, g_g, r_q, r_k, r_v, r_g) = [seg(i) for i in range(15)]
    pad = jnp.zeros(w_in.shape[:2] + (LANES - 2 * M_HEADS - G_RANK,), w_in.dtype)
    packed = jnp.concatenate([m_q, m_k, m_v, m_o, m_i, m_f, g_a, pad,
                              g_q, g_k, g_v, g_g, r_q, r_k, r_v, r_g], axis=-1)
    return packed.astype(bf16)


def kernel(x, positions, norm_pre_mix, norm_post_mix, norm_pre_ffn, norm_post_ffn, w_in, mlstm_conv_w, mlstm_i_bias, mlstm_f_bias, mlstm_norm, gla_w_up, gla_gate_bias, gla_norm, ret_norm, w_out, w_ff1, w_ff2):
    B, S, D = x.shape
    t_np, m_np = _chunk_constants()
    consts = ((jnp.asarray(t_np, bf16), jnp.asarray(m_np))
              + tuple(jnp.asarray(a) for a in _ret_constants())
              + tuple(jnp.asarray(a) for a in _head_constants()))
    blk, seg, ones_mean = consts[-3:]
    consts = consts[:-3] + (blk, seg.astype(bf16), ones_mean.astype(bf16))

    cosf, sins = _rope_tables(positions)

    ifb = jnp.concatenate([mlstm_i_bias, mlstm_f_bias,
                           jnp.zeros((DEPTH, LANES - 2 * M_HEADS), f32)], axis=-1)
    wup = jnp.zeros((DEPTH, LANES, G_QK_W), f32).at[:, GATE_A:GATE_A + G_RANK, :].set(gla_w_up)
    row = lambda a: a[:, None, :]
    params = (row(norm_pre_mix), row(norm_post_mix), _pack_w_in(w_in), w_out.astype(bf16),
              mlstm_conv_w, row(ifb), wup.astype(bf16), row(gla_gate_bias),
              row(mlstm_norm), row(gla_norm), row(ret_norm))
    gpre_f, gpost_f = row(norm_pre_ffn), row(norm_post_ffn)
    w1_b = w_ff1.astype(bf16)
    w2_b = w_ff2.astype(bf16)

    h = x
    for l in range(DEPTH):
        h = _mixer_layer(l, h, cosf, sins, params, consts)
        h = _ffn_layer(l, h.reshape(B * S, D), gpre_f, gpost_f, w1_b, w2_b).reshape(B, S, D)
    return h
```

```python
import functools

import numpy as np
import jax
import jax.numpy as jnp
from jax import lax
from jax.experimental import pallas as pl
from jax.experimental.pallas import tpu as pltpu

D_MODEL = 1024
DEPTH = 4
EPS = 1e-6
M_HEADS, M_V, M_QK, M_CONV = 4, 128, 64, 4
G_HEADS, G_V, G_QK, G_RANK = 4, 64, 32, 16
G_NORMALIZER = 16.0
R_HEADS, R_V, R_QK = 4, 64, 32
ROPE_BASE = 10000.0
D_FF = 4 * D_MODEL

M_QK_W, M_V_W = M_HEADS * M_QK, M_HEADS * M_V
G_QK_W, G_V_W = G_HEADS * G_QK, G_HEADS * G_V
R_QK_W, R_V_W = R_HEADS * R_QK, R_HEADS * R_V
IN_SIZES = (M_QK_W, M_QK_W, M_V_W, M_HEADS, M_HEADS, M_V_W,
            G_QK_W, G_QK_W, G_V_W, G_RANK, G_V_W,
            R_QK_W, R_QK_W, R_V_W, R_V_W)
IN_OFFS = tuple(int(v) for v in np.cumsum((0,) + IN_SIZES))

LANES = 128
C_MQK = 0
C_MV = C_MQK + 2 * M_QK_W
C_MO = C_MV + M_V_W
C_GATE = C_MO + M_V_W
C_GQ = C_GATE + LANES
C_GK = C_GQ + G_QK_W
C_GV = C_GK + G_QK_W
C_GG = C_GV + G_V_W
C_RQ = C_GG + G_V_W
C_RK = C_RQ + R_QK_W
C_RV = C_RK + R_QK_W
C_RG = C_RV + R_V_W
P_COLS = C_RG + R_V_W
GATE_I, GATE_F, GATE_A = 0, M_HEADS, 2 * M_HEADS

LC = 128
N_LEVELS = 7
TS = 512
TM = 512
FF_BLK = 1024
PROJ_BLK = 1280
VMEM_LIMIT = 56 * 1024 * 1024

_NT = (((1,), (1,)), ((), ()))
f32 = jnp.float32
bf16 = jnp.bfloat16


def _bf(x):
    return x.astype(bf16)


def _dot(a, b):
    return jnp.dot(a, b, preferred_element_type=f32)


def _dot_nt(a, b):
    return lax.dot_general(a, b, _NT, preferred_element_type=f32)


def _rms(x, g):
    return x * lax.rsqrt(jnp.mean(x * x, axis=-1, keepdims=True) + EPS) * g


def _log_sigmoid(x):
    return jnp.minimum(x, 0.0) - jnp.log1p(jnp.exp(-jnp.abs(x)))


def _sigmoid(x):
    return 1.0 / (1.0 + jnp.exp(-x))


def _split3(x):
    hi = _bf(x)
    r1 = x - hi.astype(f32)
    mid = _bf(r1)
    lo = _bf(r1 - mid.astype(f32))
    return jnp.concatenate([hi, mid, lo], axis=0)


def _split2(x):
    hi = _bf(x)
    return jnp.concatenate([hi, _bf(x - hi.astype(f32))], axis=0)


def _chunk_constants():
    L = LC
    T = np.zeros((N_LEVELS + 2, L, L), np.float32)
    M = np.zeros((N_LEVELS + 1, L, L), np.float32)
    idx = np.arange(L)
    for li in range(N_LEVELS):
        s = L >> (li + 1)
        for i in range(L):
            seg = i // s
            if seg % 2 == 1:
                T[li, i, seg * s:i + 1] = 1.0
            else:
                T[li, i, i + 1:seg * s + s] = 1.0
        same_blk = (idx[:, None] // (2 * s)) == (idx[None, :] // (2 * s))
        M[li] = same_blk & ((idx[:, None] // s) % 2 == 1) & ((idx[None, :] // s) % 2 == 0)
    M[N_LEVELS] = np.eye(L)
    for i in range(L):
        T[N_LEVELS, i, :i + 1] = 1.0
        T[N_LEVELS + 1, i, i + 1:] = 1.0
    T = T.reshape((N_LEVELS + 2) * L, L)
    return np.concatenate([T, T, T], axis=1), M


def _ret_constants():
    L = LC
    log_gamma = np.log1p(-np.exp2(-5.0 - np.arange(R_HEADS, dtype=np.float64)))
    i = np.arange(L)
    rel = (i[:, None] - i[None, :]).astype(np.float64)
    D = np.where(rel[None] >= 0, np.exp(log_gamma[:, None, None] * rel[None]), 0.0)
    lg_lane = log_gamma[np.repeat(np.arange(R_HEADS), R_QK)][None, :]
    qdec = np.exp(lg_lane * (i[:, None] + 1.0))
    kdec = np.exp(lg_lane * (L - 1.0 - i[:, None]))
    sdec = np.exp(lg_lane * L)
    return tuple(a.astype(np.float32) for a in (D, qdec, kdec, sdec))


def _head_constants():
    blk = (np.arange(G_V_W)[:, None] // G_V) == (np.arange(G_QK_W)[None, :] // G_QK)
    seg = ((np.arange(G_V_W)[:, None] // G_V) == (np.arange(G_V_W)[None, :] // G_V)) / float(G_V)
    ones_mean = np.full((M_V, M_V), 1.0 / M_V)
    return blk.astype(np.float32), seg.astype(np.float32), ones_mean.astype(np.float32)


def _rope_kernel(pos_ref, invf_ref, sgn_ref, cos_ref, sin_ref):
    ang = pos_ref[0].astype(f32) * invf_ref[...]
    cos_ref[0] = jnp.cos(ang)
    sin_ref[0] = jnp.sin(ang) * sgn_ref[...]


def _rope_tables(positions):
    B, S = positions.shape
    r = np.arange(LANES) % R_QK
    inv_freq = ROPE_BASE ** (-jnp.arange(0, R_QK, 2, dtype=f32) / R_QK)
    invf = jnp.tile(inv_freq, LANES // (R_QK // 2))[None, :]
    sgn = jnp.asarray(np.where(r < R_QK // 2, -1.0, 1.0).astype(np.float32)[None, :])
    return pl.pallas_call(
        _rope_kernel,
        grid=(B,),
        in_specs=[pl.BlockSpec((1, S, 1), lambda b: (b, 0, 0)),
                  pl.BlockSpec((1, LANES), lambda b: (0, 0)),
                  pl.BlockSpec((1, LANES), lambda b: (0, 0))],
        out_specs=[pl.BlockSpec((1, S, LANES), lambda b: (b, 0, 0)),
                   pl.BlockSpec((1, S, LANES), lambda b: (b, 0, 0))],
        out_shape=[jax.ShapeDtypeStruct((B, S, LANES), f32)] * 2,
        name="rope_tables",
    )(positions[:, :, None], invf, sgn)


def _mixer_kernel(h_ref, cos_ref, sin_ref, gpre_ref, gpost_ref, win_ref, wout_ref,
                  convw_ref, ifb_ref, wup_ref, gab_ref, mnorm_ref, gnorm_ref, rnorm_ref,
                  t_ref, lmb_ref, d_ref, qdec_ref, kdec_ref, sdec_ref, blk_ref, seg_ref, onesm_ref,
                  out_ref,
                  proj_ref, qkpad_ref, la_ref, vext_ref, y_ref, mc_ref, mm_ref, gs_ref, rs_ref):
    t = pl.program_id(1)

    @pl.when(t == 0)
    def _():
        qkpad_ref[0:8, :] = jnp.zeros((8, 2 * M_QK_W), f32)
        mc_ref[...] = jnp.zeros_like(mc_ref)
        mm_ref[...] = jnp.zeros_like(mm_ref)
        gs_ref[...] = jnp.zeros_like(gs_ref)
        rs_ref[...] = jnp.zeros_like(rs_ref)
        for hd in range(M_HEADS):
            vext_ref[:, (2 * hd + 1) * M_V:(2 * hd + 2) * M_V] = jnp.ones((TS, M_V), bf16)

    h_in = h_ref[0]
    u = _bf(_rms(h_in, gpre_ref[0]))
    for lo in range(0, P_COLS, PROJ_BLK):
        hi = min(lo + PROJ_BLK, P_COLS)
        proj_ref[:, lo:hi] = _dot(u, win_ref[:, lo:hi])

    qkpad_ref[8:8 + TS, :] = proj_ref[:, C_MQK:C_MQK + 2 * M_QK_W]
    conv = jnp.zeros((TS, 2 * M_QK_W), f32)
    for j in range(M_CONV):
        conv = conv + convw_ref[0, j:j + 1, :] * qkpad_ref[5 + j:5 + j + TS, :]
    qkpad_ref[0:8, :] = qkpad_ref[TS:TS + 8, :]
    qk = conv * _sigmoid(conv)
    lane_qk = lax.broadcasted_iota(jnp.int32, (1, 2 * M_QK_W), 1)
    proj_ref[:, C_MQK:C_MQK + 2 * M_QK_W] = qk * jnp.where(lane_qk < M_QK_W, 1.0, M_QK ** -0.5)
    for hd in range(M_HEADS):
        vext_ref[:, 2 * hd * M_V:(2 * hd + 1) * M_V] = _bf(proj_ref[:, C_MV + hd * M_V:C_MV + (hd + 1) * M_V])

    graw = proj_ref[:, C_GATE:C_GATE + LANES]
    gb = graw + ifb_ref[0]
    lane_g = lax.broadcasted_iota(jnp.int32, (TS, LANES), 1)
    proj_ref[:, C_GATE:C_GATE + LANES] = jnp.where(lane_g < GATE_F, gb, _log_sigmoid(gb))
    a_pre = _dot(_bf(graw), wup_ref[0]) + gab_ref[0]
    la_ref[...] = _log_sigmoid(a_pre) * (1.0 / G_NORMALIZER)

    proj_ref[:, C_GQ:C_GQ + G_QK_W] = proj_ref[:, C_GQ:C_GQ + G_QK_W] * (G_QK ** -0.5)
    cosf = cos_ref[0]
    sins = sin_ref[0]
    first_half = (lane_g % R_QK) < (R_QK // 2)
    for col, scale in ((C_RQ, 1.0), (C_RK, R_QK ** -0.5)):
        xr = proj_ref[:, col:col + R_QK_W]
        swapped = jnp.where(first_half, pltpu.roll(xr, LANES - R_QK // 2, axis=1),
                            pltpu.roll(xr, R_QK // 2, axis=1))
        proj_ref[:, col:col + R_QK_W] = (xr * cosf + swapped * sins) * scale

    row_i = lax.broadcasted_iota(jnp.int32, (LC, LC), 0)
    col_j = lax.broadcasted_iota(jnp.int32, (LC, LC), 1)
    causal = row_i >= col_j
    top_rows = row_i < M_QK
    left_lanes = col_j < G_V
    lane1 = lax.broadcasted_iota(jnp.int32, (1, LANES), 1)
    pair_mask = [(lane1 < M_QK).astype(f32), (lane1 >= M_QK).astype(f32)]
    head_mask = [((lane1 // G_QK) == hd).astype(f32) for hd in range(G_HEADS)]

    head_mask_b = [m.astype(bf16) for m in head_mask]

    def masked_rhs(kb):
        return jnp.concatenate([kb * m for m in head_mask_b], axis=0)

    def linear_attn(p_heads, v_all, q_in, k_out, st_ref, dec_row, gain, gate):
        vb = _bf(v_all)
        tiles = []
        for tp in range(2):
            vt = vb[:, tp * LANES:(tp + 1) * LANES]
            tiles.append(jnp.where(left_lanes, _dot(p_heads[2 * tp], vt), _dot(p_heads[2 * tp + 1], vt)))
        st = st_ref[...]
        hcat = jnp.concatenate(tiles, axis=1) + _dot_nt(_bf(q_in), _bf(st))
        st_ref[...] = st * dec_row + blk_ref[...] * _dot(_bf(v_all.T), _bf(k_out))
        ms = _dot(_bf(hcat * hcat), seg_ref[...])
        return hcat * lax.rsqrt(ms + EPS) * gain * (gate * _sigmoid(gate))

    def chunk_body(c, carry):
        r0 = pl.multiple_of(c * LC, LC)
        rows = pl.ds(r0, LC)

        heads = range(M_HEADS)

        gblk = proj_ref[rows, C_GATE:C_GATE + LANES]
        bcum = _dot(t_ref[N_LEVELS * LC:(N_LEVELS + 1) * LC, :], _split3(gblk))
        e_all = _dot(t_ref[:, :2 * LC], _split2(la_ref[rows, :]))
        m_q = [proj_ref[rows, C_MQK + tp * LANES:C_MQK + (tp + 1) * LANES] for tp in range(2)]
        m_k = [proj_ref[rows, C_MQK + M_QK_W + tp * LANES:C_MQK + M_QK_W + (tp + 1) * LANES]
               for tp in range(2)]
        m_kb = [_bf(k) for k in m_k]
        qm = [_bf(m_q[hd // 2] * pair_mask[hd % 2]) for hd in heads]
        s_m = [_dot_nt(qm[hd], m_kb[hd // 2]) for hd in heads]
        rq = proj_ref[rows, C_RQ:C_RQ + R_QK_W]
        rk = proj_ref[rows, C_RK:C_RK + R_QK_W]
        s_r = _dot_nt(_bf(rq), masked_rhs(_bf(rk)))
        gq = proj_ref[rows, C_GQ:C_GQ + G_QK_W]
        gk = proj_ref[rows, C_GK:C_GK + G_QK_W]

        xg = jnp.where(col_j < GATE_F, gblk, bcum)
        xgt = xg.T
        g_row = [xgt[GATE_I + hd:GATE_I + hd + 1, :] - xgt[GATE_F + hd:GATE_F + hd + 1, :] for hd in heads]
        b_col = [xg[:, GATE_F + hd:GATE_F + hd + 1] for hd in heads]
        m_prev = [mm_ref[hd] for hd in heads]
        dmat = [jnp.where(causal, g_row[hd], -jnp.inf) for hd in heads]
        rmax = [jnp.max(dmat[hd], axis=1, keepdims=True) for hd in heads]
        mu = [jnp.maximum(rmax[hd], m_prev[hd]) for hd in heads]
        p_m = [_bf(jnp.exp(dmat[hd] - mu[hd]) * s_m[hd]) for hd in heads]
        s_old, w_row = [], []
        for hd in heads:
            g_max = rmax[hd][LC - 1:LC, :]
            b_last = b_col[hd][LC - 1:LC, :]
            a_max = b_last + g_max
            m_new = jnp.maximum(b_last + m_prev[hd], a_max)
            s_old.append(jnp.exp(b_last + m_prev[hd] - m_new))
            w_row.append(jnp.exp(g_row[hd] - g_max) * jnp.exp(a_max - m_new))
            mm_ref[hd] = m_new

        x = _dot_nt(_bf(gq), masked_rhs(_bf(gk)))
        acc = [lmb_ref[N_LEVELS] * _bf(x[:, hd * LC:(hd + 1) * LC]) for hd in range(G_HEADS)]
        for li in range(N_LEVELS):
            pw = jnp.exp(e_all[li * LC:(li + 1) * LC])
            x = _dot_nt(_bf(gq * pw), masked_rhs(_bf(gk * pw)))
            acc = [acc[hd] + lmb_ref[li] * _bf(x[:, hd * LC:(hd + 1) * LC]) for hd in range(G_HEADS)]

        c_prev = [mc_ref[tp] for tp in range(2)]
        c_prev_b = [_bf(cp) for cp in c_prev]
        for hd in heads:
            tp, e = hd // 2, hd % 2
            isc = jnp.exp(m_prev[hd] - mu[hd])
            vext = vext_ref[rows, 2 * hd * M_V:(2 * hd + 2) * M_V]
            ne = _dot(p_m[hd], vext) + isc * _dot(qm[hd], c_prev_b[tp][:, 2 * e * M_V:(2 * e + 2) * M_V])
            num = ne[:, :M_V]
            den = ne[:, M_V:]
            hm = num / jnp.maximum(jnp.abs(den), jnp.exp(-(b_col[hd] + mu[hd])))
            ms = _dot(_bf(hm * hm), onesm_ref[...])
            o_gate = _sigmoid(proj_ref[rows, C_MO + hd * M_V:C_MO + (hd + 1) * M_V])
            y_ref[rows, hd * M_V:(hd + 1) * M_V] = _bf(
                hm * lax.rsqrt(ms + EPS) * mnorm_ref[0, :, hd * M_V:(hd + 1) * M_V] * o_gate)

        p_ret = [_bf(d_ref[hd] * s_r[:, hd * LC:(hd + 1) * LC]) for hd in range(R_HEADS)]
        y_ref[rows, M_V_W + G_V_W:M_V_W + G_V_W + R_V_W] = _bf(linear_attn(
            p_ret, proj_ref[rows, C_RV:C_RV + R_V_W], rq * qdec_ref[...], rk * kdec_ref[...], rs_ref,
            sdec_ref[...], rnorm_ref[0], proj_ref[rows, C_RG:C_RG + R_V_W]))
        pb = jnp.exp(e_all[N_LEVELS * LC:(N_LEVELS + 1) * LC])
        pk = jnp.exp(e_all[(N_LEVELS + 1) * LC:(N_LEVELS + 2) * LC])
        y_ref[rows, M_V_W:M_V_W + G_V_W] = _bf(linear_attn(
            acc, proj_ref[rows, C_GV:C_GV + G_V_W], gq * pb, gk * pk, gs_ref,
            pb[LC - 1:LC, :], gnorm_ref[0], proj_ref[rows, C_GG:C_GG + G_V_W]))

        for tp in range(2):
            kw_t = _bf(m_k[tp].T * jnp.where(top_rows, w_row[2 * tp], w_row[2 * tp + 1]))
            for e in range(2):
                hd = 2 * tp + e
                rs = slice(e * M_QK, (e + 1) * M_QK)
                cs = slice(2 * e * M_V, (2 * e + 2) * M_V)
                vext = vext_ref[rows, 2 * hd * M_V:(2 * hd + 2) * M_V]
                mc_ref[tp, rs, cs] = s_old[hd] * c_prev[tp][rs, cs] + _dot(kw_t[rs], vext)
        return carry

    lax.fori_loop(0, TS // LC, chunk_body, 0, unroll=2)

    mixed = _dot(y_ref[...], wout_ref[...])
    out_ref[0] = h_in + _rms(mixed, gpost_ref[0])


def _mixer_layer(layer, h, cosf, sins, params, consts):
    B, S, D = h.shape
    tile = pl.BlockSpec((1, TS, D), lambda b, t: (b, t, 0))
    rope = pl.BlockSpec((1, TS, LANES), lambda b, t: (b, t, 0))

    def weight_spec(a):
        return pl.BlockSpec((None,) + a.shape[1:], lambda b, t: (layer, 0, 0))

    def small_spec(a):
        return pl.BlockSpec((1,) + a.shape[1:], lambda b, t: (layer, 0, 0))

    def const_spec(a):
        nd = a.ndim
        return pl.BlockSpec(a.shape, lambda b, t: (0,) * nd)

    gpre, gpost, win, wout, convw, ifb, wup, gab, mnorm, gnorm, rnorm = params
    in_specs = ([tile, rope, rope, small_spec(gpre), small_spec(gpost), weight_spec(win), weight_spec(wout)]
                + [small_spec(a) for a in (convw, ifb, wup, gab, mnorm, gnorm, rnorm)]
                + [const_spec(a) for a in consts])
    return pl.pallas_call(
        _mixer_kernel,
        grid=(B, S // TS),
        in_specs=in_specs,
        out_specs=tile,
        out_shape=jax.ShapeDtypeStruct(h.shape, h.dtype),
        scratch_shapes=[
            pltpu.VMEM((TS, P_COLS), f32),
            pltpu.VMEM((TS + 8, 2 * M_QK_W), f32),
            pltpu.VMEM((TS, LANES), f32),
            pltpu.VMEM((TS, 2 * M_V_W), bf16),
            pltpu.VMEM((TS, D_MODEL), bf16),
            pltpu.VMEM((M_HEADS // 2, 2 * M_QK, 4 * M_V), f32),
            pltpu.VMEM((M_HEADS, 1, 1), f32),
            pltpu.VMEM((G_V_W, G_QK_W), f32),
            pltpu.VMEM((R_V_W, R_QK_W), f32),
        ],
        compiler_params=pltpu.CompilerParams(
            dimension_semantics=("arbitrary", "arbitrary"), vmem_limit_bytes=VMEM_LIMIT),
        name="mixer_layer",
    )(h, cosf, sins, gpre, gpost, win, wout, convw, ifb, wup, gab, mnorm, gnorm, rnorm, *consts)


def _ffn_kernel(h_ref, gpre_ref, gpost_ref, w1_ref, w2_ref, out_ref):
    h_in = h_ref[...]
    u = _bf(_rms(h_in, gpre_ref[0]))
    acc = jnp.zeros((TM, D_MODEL), f32)
    for lo in range(0, D_FF, FF_BLK):
        hid = _dot(u, w1_ref[:, lo:lo + FF_BLK])
        hid = jnp.square(jnp.maximum(hid, 0.0))
        acc = acc + _dot(_bf(hid), w2_ref[lo:lo + FF_BLK, :])
    out_ref[...] = h_in + _rms(acc, gpost_ref[0])


def _ffn_layer(layer, h2, gpre, gpost, w1, w2):
    n_tok, D = h2.shape
    tile = pl.BlockSpec((TM, D), lambda i: (i, 0))
    return pl.pallas_call(
        _ffn_kernel,
        grid=(n_tok // TM,),
        in_specs=[tile,
                  pl.BlockSpec((1, 1, D), lambda i: (layer, 0, 0)),
                  pl.BlockSpec((1, 1, D), lambda i: (layer, 0, 0)),
                  pl.BlockSpec((None, D, D_FF), lambda i: (layer, 0, 0)),
                  pl.BlockSpec((None, D_FF, D), lambda i: (layer, 0, 0))],
        out_specs=tile,
        out_shape=jax.ShapeDtypeStruct(h2.shape, h2.dtype),
        compiler_params=pltpu.CompilerParams(
            dimension_semantics=("arbitrary",), vmem_limit_bytes=VMEM_LIMIT),
        name="ffn_layer",
    )(h2, gpre, gpost, w1, w2)


def _pack_w_in(w_in):
    def seg(i):
        return w_in[:, :, IN_OFFS[i]:IN_OFFS[i + 1]]
    (m_q, m_k, m_v, m_i, m_f, m_o, g_q, g_k, g_v, g_a, g_g, r_q, r_k, r_v, r_g) = [seg(i) for i in range(15)]
    pad = jnp.zeros(w_in.shape[:2] + (LANES - 2 * M_HEADS - G_RANK,), w_in.dtype)
    packed = jnp.concatenate([m_q, m_k, m_v, m_o, m_i, m_f, g_a, pad,
                              g_q, g_k, g_v, g_g, r_q, r_k, r_v, r_g], axis=-1)
    return packed.astype(bf16)


def kernel(x, positions, norm_pre_mix, norm_post_mix, norm_pre_ffn, norm_post_ffn, w_in, mlstm_conv_w, mlstm_i_bias, mlstm_f_bias, mlstm_norm, gla_w_up, gla_gate_bias, gla_norm, ret_norm, w_out, w_ff1, w_ff2):
    B, S, D = x.shape
    t_np, m_np = _chunk_constants()
    consts = ((jnp.asarray(t_np, bf16), jnp.asarray(m_np, bf16))
              + tuple(jnp.asarray(a) for a in _ret_constants())
              + tuple(jnp.asarray(a) for a in _head_constants()))
    blk, seg, ones_mean = consts[-3:]
    consts = consts[:-3] + (blk, seg.astype(bf16), ones_mean.astype(bf16))

    cosf, sins = _rope_tables(positions)

    ifb = jnp.concatenate([mlstm_i_bias, mlstm_f_bias,
                           jnp.zeros((DEPTH, LANES - 2 * M_HEADS), f32)], axis=-1)
    wup = jnp.zeros((DEPTH, LANES, G_QK_W), f32).at[:, GATE_A:GATE_A + G_RANK, :].set(gla_w_up)
    row = lambda a: a[:, None, :]
    params = (row(norm_pre_mix), row(norm_post_mix), _pack_w_in(w_in), w_out.astype(bf16),
              mlstm_conv_w, row(ifb), wup.astype(bf16), row(gla_gate_bias),
              row(mlstm_norm), row(gla_norm), row(ret_norm))
    gpre_f, gpost_f = row(norm_pre_ffn), row(norm_post_ffn)
    w1_b = w_ff1.astype(bf16)
    w2_b = w_ff2.astype(bf16)

    h = x
    for l in range(DEPTH):
        h = _mixer_layer(l, h, cosf, sins, params, consts)
        h = _ffn_layer(l, h.reshape(B * S, D), gpre_f, gpost_f, w1_b, w2_b).reshape(B, S, D)
    return h
```

```python
import numpy as np
import jax
import jax.numpy as jnp
from jax import lax
from jax.experimental import pallas as pl
from jax.experimental.pallas import tpu as pltpu

D_MODEL = 1024
DEPTH = 4
EPS = 1e-6
M_HEADS, M_V, M_QK, M_CONV = 4, 128, 64, 4
G_HEADS, G_V, G_QK, G_RANK = 4, 64, 32, 16
G_NORMALIZER = 16.0
R_HEADS, R_V, R_QK = 4, 64, 32
ROPE_BASE = 10000.0
D_FF = 4 * D_MODEL

M_QK_W, M_V_W = M_HEADS * M_QK, M_HEADS * M_V
G_QK_W, G_V_W = G_HEADS * G_QK, G_HEADS * G_V
R_QK_W, R_V_W = R_HEADS * R_QK, R_HEADS * R_V
IN_SIZES = (M_QK_W, M_QK_W, M_V_W, M_HEADS, M_HEADS, M_V_W,
            G_QK_W, G_QK_W, G_V_W, G_RANK, G_V_W,
            R_QK_W, R_QK_W, R_V_W, R_V_W)
IN_OFFS = tuple(int(v) for v in np.cumsum((0,) + IN_SIZES))

LANES = 128
C_MQK = 0
C_MV = C_MQK + 2 * M_QK_W
C_MO = C_MV + M_V_W
C_GATE = C_MO + M_V_W
C_GQ = C_GATE + LANES
C_GK = C_GQ + G_QK_W
C_GV = C_GK + G_QK_W
C_GG = C_GV + G_V_W
C_RQ = C_GG + G_V_W
C_RK = C_RQ + R_QK_W
C_RV = C_RK + R_QK_W
C_RG = C_RV + R_V_W
P_COLS = C_RG + R_V_W
GATE_I, GATE_F, GATE_A = 0, M_HEADS, 2 * M_HEADS

LC = 128
N_LEVELS = 7
TS = 512
TM = 512
FF_BLK = 1024
PROJ_BLK = 1280
VMEM_LIMIT = 56 * 1024 * 1024

_NT = (((1,), (1,)), ((), ()))
f32 = jnp.float32
bf16 = jnp.bfloat16


def _bf(x):
    return x.astype(bf16)


def _dot(a, b):
    return jnp.dot(a, b, preferred_element_type=f32)


def _dot_nt(a, b):
    return lax.dot_general(a, b, _NT, preferred_element_type=f32)


def _rms(x, g):
    return x * lax.rsqrt(jnp.mean(x * x, axis=-1, keepdims=True) + EPS) * g


def _log_sigmoid(x):
    return jnp.minimum(x, 0.0) - jnp.log1p(jnp.exp(-jnp.abs(x)))


def _sigmoid(x):
    return 1.0 / (1.0 + jnp.exp(-x))


def _split3(x):
    hi = _bf(x)
    r1 = x - hi.astype(f32)
    mid = _bf(r1)
    lo = _bf(r1 - mid.astype(f32))
    return jnp.concatenate([hi, mid, lo], axis=0)


def _split2(x):
    hi = _bf(x)
    return jnp.concatenate([hi, _bf(x - hi.astype(f32))], axis=0)


def _chunk_constants():
    L = LC
    T = np.zeros((N_LEVELS + 2, L, L), np.float32)
    M = np.zeros((N_LEVELS + 1, L, L), np.float32)
    idx = np.arange(L)
    for li in range(N_LEVELS):
        s = L >> (li + 1)
        for i in range(L):
            seg = i // s
            if seg % 2 == 1:
                T[li, i, seg * s:i + 1] = 1.0
            else:
                T[li, i, i + 1:seg * s + s] = 1.0
        same_blk = (idx[:, None] // (2 * s)) == (idx[None, :] // (2 * s))
        M[li] = same_blk & ((idx[:, None] // s) % 2 == 1) & ((idx[None, :] // s) % 2 == 0)
    M[N_LEVELS] = np.eye(L)
    for i in range(L):
        T[N_LEVELS, i, :i + 1] = 1.0
        T[N_LEVELS + 1, i, i + 1:] = 1.0
    T = T.reshape((N_LEVELS + 2) * L, L)
    return np.concatenate([T, T, T], axis=1), M


def _ret_constants():
    L = LC
    log_gamma = np.log1p(-np.exp2(-5.0 - np.arange(R_HEADS, dtype=np.float64)))
    i = np.arange(L)
    rel = (i[:, None] - i[None, :]).astype(np.float64)
    D = np.where(rel[None] >= 0, np.exp(log_gamma[:, None, None] * rel[None]), 0.0)
    lg_lane = log_gamma[np.repeat(np.arange(R_HEADS), R_QK)][None, :]
    qdec = np.exp(lg_lane * (i[:, None] + 1.0))
    kdec = np.exp(lg_lane * (L - 1.0 - i[:, None]))
    sdec = np.exp(lg_lane * L)
    return tuple(a.astype(np.float32) for a in (D, qdec, kdec, sdec))


def _head_constants():
    blk = (np.arange(G_V_W)[:, None] // G_V) == (np.arange(G_QK_W)[None, :] // G_QK)
    seg = ((np.arange(G_V_W)[:, None] // G_V) == (np.arange(G_V_W)[None, :] // G_V)) / float(G_V)
    ones_mean = np.kron(np.eye(2), np.full((M_V, M_V), 1.0 / M_V))
    return blk.astype(np.float32), seg.astype(np.float32), ones_mean.astype(np.float32)


def _rope_kernel(pos_ref, invf_ref, sgn_ref, cos_ref, sin_ref):
    ang = pos_ref[0].astype(f32) * invf_ref[...]
    cos_ref[0] = jnp.cos(ang)
    sin_ref[0] = jnp.sin(ang) * sgn_ref[...]


def _rope_tables(positions):
    B, S = positions.shape
    r = np.arange(LANES) % R_QK
    inv_freq = ROPE_BASE ** (-jnp.arange(0, R_QK, 2, dtype=f32) / R_QK)
    invf = jnp.tile(inv_freq, LANES // (R_QK // 2))[None, :]
    sgn = jnp.asarray(np.where(r < R_QK // 2, -1.0, 1.0).astype(np.float32)[None, :])
    return pl.pallas_call(
        _rope_kernel,
        grid=(B,),
        in_specs=[pl.BlockSpec((1, S, 1), lambda b: (b, 0, 0)),
                  pl.BlockSpec((1, LANES), lambda b: (0, 0)),
                  pl.BlockSpec((1, LANES), lambda b: (0, 0))],
        out_specs=[pl.BlockSpec((1, S, LANES), lambda b: (b, 0, 0)),
                   pl.BlockSpec((1, S, LANES), lambda b: (b, 0, 0))],
        out_shape=[jax.ShapeDtypeStruct((B, S, LANES), f32)] * 2,
        name="rope_tables",
    )(positions[:, :, None], invf, sgn)


def _mixer_kernel(h_ref, cos_ref, sin_ref, gpre_ref, gpost_ref, win_ref, wout_ref,
                  convw_ref, ifb_ref, wup_ref, gab_ref, mnorm_ref, gnorm_ref, rnorm_ref,
                  t_ref, lmb_ref, d_ref, qdec_ref, kdec_ref, sdec_ref, blk_ref, seg_ref, onesm_ref,
                  out_ref,
                  proj_ref, qkpad_ref, la_ref, e_ref, vext_ref, y_ref, mc_ref, mm_ref, gs_ref, rs_ref):
    t = pl.program_id(1)

    @pl.when(t == 0)
    def _():
        qkpad_ref[0:8, :] = jnp.zeros((8, 2 * M_QK_W), f32)
        mc_ref[...] = jnp.zeros_like(mc_ref)
        mm_ref[...] = jnp.zeros_like(mm_ref)
        gs_ref[...] = jnp.zeros_like(gs_ref)
        rs_ref[...] = jnp.zeros_like(rs_ref)
        for hd in range(M_HEADS):
            vext_ref[:, (2 * hd + 1) * M_V:(2 * hd + 2) * M_V] = jnp.ones((TS, M_V), bf16)

    h_in = h_ref[0]
    u = _bf(_rms(h_in, gpre_ref[0]))
    for lo in range(0, P_COLS, PROJ_BLK):
        hi = min(lo + PROJ_BLK, P_COLS)
        proj_ref[:, lo:hi] = _dot(u, win_ref[:, lo:hi])

    qkpad_ref[8:8 + TS, :] = proj_ref[:, C_MQK:C_MQK + 2 * M_QK_W]
    conv = jnp.zeros((TS, 2 * M_QK_W), f32)
    for j in range(M_CONV):
        conv = conv + convw_ref[0, j:j + 1, :] * qkpad_ref[5 + j:5 + j + TS, :]
    qkpad_ref[0:8, :] = qkpad_ref[TS:TS + 8, :]
    qk = conv * _sigmoid(conv)
    lane_qk = lax.broadcasted_iota(jnp.int32, (1, 2 * M_QK_W), 1)
    proj_ref[:, C_MQK:C_MQK + 2 * M_QK_W] = qk * jnp.where(lane_qk < M_QK_W, 1.0, M_QK ** -0.5)
    for hd in range(M_HEADS):
        vext_ref[:, 2 * hd * M_V:(2 * hd + 1) * M_V] = _bf(proj_ref[:, C_MV + hd * M_V:C_MV + (hd + 1) * M_V])

    graw = proj_ref[:, C_GATE:C_GATE + LANES]
    gb = graw + ifb_ref[0]
    lane_g = lax.broadcasted_iota(jnp.int32, (TS, LANES), 1)
    proj_ref[:, C_GATE:C_GATE + LANES] = jnp.where(lane_g < GATE_F, gb, _log_sigmoid(gb))
    a_pre = _dot(_bf(graw), wup_ref[0]) + gab_ref[0]
    la_ref[...] = _log_sigmoid(a_pre) * (1.0 / G_NORMALIZER)

    proj_ref[:, C_GQ:C_GQ + G_QK_W] = proj_ref[:, C_GQ:C_GQ + G_QK_W] * (G_QK ** -0.5)
    cosf = cos_ref[0]
    sins = sin_ref[0]
    first_half = (lane_g % R_QK) < (R_QK // 2)
    for col, scale in ((C_RQ, 1.0), (C_RK, R_QK ** -0.5)):
        xr = proj_ref[:, col:col + R_QK_W]
        swapped = jnp.where(first_half, pltpu.roll(xr, LANES - R_QK // 2, axis=1),
                            pltpu.roll(xr, R_QK // 2, axis=1))
        proj_ref[:, col:col + R_QK_W] = (xr * cosf + swapped * sins) * scale

    row_i = lax.broadcasted_iota(jnp.int32, (LC, LC), 0)
    col_j = lax.broadcasted_iota(jnp.int32, (LC, LC), 1)
    causal = row_i >= col_j
    top_rows = row_i < M_QK
    left_lanes = col_j < G_V
    lane1 = lax.broadcasted_iota(jnp.int32, (1, LANES), 1)
    pair_mask = [(lane1 < M_QK).astype(f32), (lane1 >= M_QK).astype(f32)]
    pair_mask_b = [m.astype(bf16) for m in pair_mask]
    head_mask_b = [((lane1 // G_QK) == hd).astype(bf16) for hd in range(G_HEADS)]

    def masked_rhs(kb):
        return jnp.concatenate([kb * m for m in head_mask_b], axis=0)

    def pair_tiles(p_heads, vb):
        tiles = []
        for tp in range(2):
            vt = vb[:, tp * LANES:(tp + 1) * LANES]
            tiles.append(_dot(jnp.concatenate([p_heads[2 * tp], p_heads[2 * tp + 1]], axis=1),
                              jnp.concatenate([vt * pair_mask_b[0], vt * pair_mask_b[1]], axis=0)))
        return jnp.concatenate(tiles, axis=1)

    def head_norm_gate(hcat, gain, gate):
        ms = _dot(_bf(hcat * hcat), seg_ref[...])
        return _bf(hcat * lax.rsqrt(ms + EPS) * gain * (gate * _sigmoid(gate)))

    n_ch = TS // LC
    chunks = range(n_ch)
    heads = range(M_HEADS)
    rows = [slice(c * LC, (c + 1) * LC) for c in chunks]
    t_cum = t_ref[N_LEVELS * LC:(N_LEVELS + 1) * LC, :]

    gblk, bcum, qm, s_m, s_r, m_k, gq, gk, rq, rk = ([] for _ in range(10))
    for c in range(0, n_ch, 2):
        gblk += [proj_ref[rows[c + i], C_GATE:C_GATE + LANES] for i in range(2)]
        cum2 = _dot(t_cum, jnp.concatenate([_split3(gblk[c]), _split3(gblk[c + 1])], axis=1))
        bcum += [cum2[:, :LANES], cum2[:, LANES:]]
        e2 = _dot(t_ref[:, :2 * LC], jnp.concatenate(
            [_split2(la_ref[rows[c], :]), _split2(la_ref[rows[c + 1], :])], axis=1))
        e_ref[c] = e2[:, :LANES]
        e_ref[c + 1] = e2[:, LANES:]
    for c in chunks:
        m_q = [proj_ref[rows[c], C_MQK + tp * LANES:C_MQK + (tp + 1) * LANES] for tp in range(2)]
        m_k.append([proj_ref[rows[c], C_MQK + M_QK_W + tp * LANES:C_MQK + M_QK_W + (tp + 1) * LANES]
                    for tp in range(2)])
        qm.append([_bf(m_q[hd // 2] * pair_mask[hd % 2]) for hd in heads])
        s_m.append([])
        for tp in range(2):
            kb = _bf(m_k[c][tp])
            s2 = _dot_nt(_bf(m_q[tp]), jnp.concatenate([kb * pair_mask_b[0], kb * pair_mask_b[1]], axis=0))
            s_m[c] += [s2[:, :LC], s2[:, LC:]]
        rq.append(proj_ref[rows[c], C_RQ:C_RQ + R_QK_W])
        rk.append(proj_ref[rows[c], C_RK:C_RK + R_QK_W])
        s_r.append(_dot_nt(_bf(rq[c]), masked_rhs(_bf(rk[c]))))
        gq.append(proj_ref[rows[c], C_GQ:C_GQ + G_QK_W])
        gk.append(proj_ref[rows[c], C_GK:C_GK + G_QK_W])

    g_row, b_col, rmax = [], [], []
    for c in chunks:
        xg = jnp.where(col_j < GATE_F, gblk[c], bcum[c])
        xgt = xg.T
        g_row.append([xgt[GATE_I + hd:GATE_I + hd + 1, :] - xgt[GATE_F + hd:GATE_F + hd + 1, :]
                      for hd in heads])
        b_col.append([xg[:, GATE_F + hd:GATE_F + hd + 1] for hd in heads])
        rmax.append([jnp.max(jnp.where(causal, g_row[c][hd], -jnp.inf), axis=1, keepdims=True)
                     for hd in heads])

    m_run = [mm_ref[hd] for hd in heads]
    m_prev, s_old, w_row = [], [], []
    for c in chunks:
        m_prev.append(list(m_run))
        s_old.append([])
        w_row.append([])
        for hd in heads:
            g_max = rmax[c][hd][LC - 1:LC, :]
            b_last = b_col[c][hd][LC - 1:LC, :]
            a_max = b_last + g_max
            m_new = jnp.maximum(b_last + m_run[hd], a_max)
            s_old[c].append(jnp.exp(b_last + m_run[hd] - m_new))
            w_row[c].append(jnp.exp(g_row[c][hd] - g_max) * jnp.exp(a_max - m_new))
            m_run[hd] = m_new
    for hd in heads:
        mm_ref[hd] = m_run[hd]

    mu = [[jnp.maximum(rmax[c][hd], m_prev[c][hd]) for hd in heads] for c in chunks]
    p_m = [[_bf(jnp.exp(jnp.where(causal, g_row[c][hd], -jnp.inf) - mu[c][hd]) * s_m[c][hd])
            for hd in heads] for c in chunks]

    RB = 16
    n_rb = LC // RB

    def add_level(acc_c, x, li, blocks):
        for hd in range(G_HEADS):
            for n, b in enumerate(blocks):
                term = (lmb_ref[li, b * RB:(b + 1) * RB, :]
                        * _bf(x[n * RB:(n + 1) * RB, hd * LC:(hd + 1) * LC]))
                acc_c[hd][b] = term if acc_c[hd][b] is None else acc_c[hd][b] + term

    acc = []
    for c in chunks:
        acc.append([[None] * n_rb for _ in range(G_HEADS)])
        add_level(acc[c], _dot_nt(_bf(gq[c]), masked_rhs(_bf(gk[c]))), N_LEVELS, list(range(n_rb)))
    for li in range(N_LEVELS):
        seg = LC >> (li + 1)
        if seg >= RB:
            blocks = [b for b in range(n_rb) if ((b * RB) // seg) % 2 == 1]
        else:
            blocks = list(range(n_rb))
        for c in chunks:
            pw = jnp.exp(e_ref[c, li * LC:(li + 1) * LC, :])
            qs = _bf(gq[c] * pw)
            if len(blocks) < n_rb:
                qs = jnp.concatenate([qs[b * RB:(b + 1) * RB] for b in blocks], axis=0)
            add_level(acc[c], _dot_nt(qs, masked_rhs(_bf(gk[c] * pw))), li, blocks)
    acc = [[jnp.concatenate(acc[c][hd], axis=0) for hd in range(G_HEADS)] for c in chunks]

    intra_m, u_m, intra_g, u_g, qin_g, dec_g, intra_r, u_r, qin_r = ([] for _ in range(9))
    for c in chunks:
        vext = [vext_ref[rows[c], 2 * hd * M_V:(2 * hd + 2) * M_V] for hd in heads]
        intra_m.append([_dot(p_m[c][hd], vext[hd]) for hd in heads])
        u_m.append([])
        for tp in range(2):
            kw_t = _bf(m_k[c][tp].T * jnp.where(top_rows, w_row[c][2 * tp], w_row[c][2 * tp + 1]))
            for e in range(2):
                u_m[c].append(_dot(kw_t[e * M_QK:(e + 1) * M_QK], vext[2 * tp + e]))
        p_ret = [_bf(d_ref[hd] * s_r[c][:, hd * LC:(hd + 1) * LC]) for hd in range(R_HEADS)]
        rv = proj_ref[rows[c], C_RV:C_RV + R_V_W]
        intra_r.append(pair_tiles(p_ret, _bf(rv)))
        u_r.append(blk_ref[...] * _dot(_bf(rv.T), _bf(rk[c] * kdec_ref[...])))
        qin_r.append(_bf(rq[c] * qdec_ref[...]))
        pb = jnp.exp(e_ref[c, N_LEVELS * LC:(N_LEVELS + 1) * LC, :])
        pk = jnp.exp(e_ref[c, (N_LEVELS + 1) * LC:(N_LEVELS + 2) * LC, :])
        gv = proj_ref[rows[c], C_GV:C_GV + G_V_W]
        intra_g.append(pair_tiles(acc[c], _bf(gv)))
        u_g.append(blk_ref[...] * _dot(_bf(gv.T), _bf(gk[c] * pk)))
        qin_g.append(_bf(gq[c] * pb))
        dec_g.append(pb[LC - 1:LC, :])

    c_st = [mc_ref[hd] for hd in heads]
    g_st = gs_ref[...]
    r_st = rs_ref[...]
    zero_half = jnp.zeros((M_QK, 2 * M_V), bf16)
    inter_m, inter_g, inter_r = [], [], []
    for c in chunks:
        inter_m.append([])
        for hd in heads:
            cb = _bf(c_st[hd])
            padded = jnp.concatenate([cb, zero_half] if hd % 2 == 0 else [zero_half, cb], axis=0)
            inter_m[c].append(_dot(qm[c][hd], padded))
            c_st[hd] = s_old[c][hd] * c_st[hd] + u_m[c][hd]
        inter_g.append(_dot_nt(qin_g[c], _bf(g_st)))
        g_st = g_st * dec_g[c] + u_g[c]
        inter_r.append(_dot_nt(qin_r[c], _bf(r_st)))
        r_st = r_st * sdec_ref[...] + u_r[c]
    for hd in heads:
        mc_ref[hd] = c_st[hd]
    gs_ref[...] = g_st
    rs_ref[...] = r_st

    for c in chunks:
        for tp in range(2):
            hm = []
            for hd in (2 * tp, 2 * tp + 1):
                ne = intra_m[c][hd] + jnp.exp(m_prev[c][hd] - mu[c][hd]) * inter_m[c][hd]
                num = ne[:, :M_V]
                den = ne[:, M_V:]
                hm.append(num / jnp.maximum(jnp.abs(den), jnp.exp(-(b_col[c][hd] + mu[c][hd]))))
            hm = jnp.concatenate(hm, axis=1)
            ms = _dot(_bf(hm * hm), onesm_ref[...])
            cols = slice(2 * tp * M_V, (2 * tp + 2) * M_V)
            o_gate = _sigmoid(proj_ref[rows[c], C_MO + 2 * tp * M_V:C_MO + (2 * tp + 2) * M_V])
            y_ref[rows[c], cols] = _bf(hm * lax.rsqrt(ms + EPS) * mnorm_ref[0, :, cols] * o_gate)
        y_ref[rows[c], M_V_W:M_V_W + G_V_W] = head_norm_gate(
            intra_g[c] + inter_g[c], gnorm_ref[0], proj_ref[rows[c], C_GG:C_GG + G_V_W])
        y_ref[rows[c], M_V_W + G_V_W:M_V_W + G_V_W + R_V_W] = head_norm_gate(
            intra_r[c] + inter_r[c], rnorm_ref[0], proj_ref[rows[c], C_RG:C_RG + R_V_W])

    mixed = _dot(y_ref[...], wout_ref[...])
    out_ref[0] = h_in + _rms(mixed, gpost_ref[0])


def _mixer_layer(layer, h, cosf, sins, params, consts):
    B, S, D = h.shape
    tile = pl.BlockSpec((1, TS, D), lambda b, t: (b, t, 0))
    rope = pl.BlockSpec((1, TS, LANES), lambda b, t: (b, t, 0))

    def weight_spec(a):
        return pl.BlockSpec((None,) + a.shape[1:], lambda b, t: (layer, 0, 0))

    def small_spec(a):
        return pl.BlockSpec((1,) + a.shape[1:], lambda b, t: (layer, 0, 0))

    def const_spec(a):
        nd = a.ndim
        return pl.BlockSpec(a.shape, lambda b, t: (0,) * nd)

    gpre, gpost, win, wout, convw, ifb, wup, gab, mnorm, gnorm, rnorm = params
    in_specs = ([tile, rope, rope, small_spec(gpre), small_spec(gpost), weight_spec(win), weight_spec(wout)]
                + [small_spec(a) for a in (convw, ifb, wup, gab, mnorm, gnorm, rnorm)]
                + [const_spec(a) for a in consts])
    return pl.pallas_call(
        _mixer_kernel,
        grid=(B, S // TS),
        in_specs=in_specs,
        out_specs=tile,
        out_shape=jax.ShapeDtypeStruct(h.shape, h.dtype),
        scratch_shapes=[
            pltpu.VMEM((TS, P_COLS), f32),
            pltpu.VMEM((TS + 8, 2 * M_QK_W), f32),
            pltpu.VMEM((TS, LANES), f32),
            pltpu.VMEM((TS // LC, (N_LEVELS + 2) * LC, LANES), f32),
            pltpu.VMEM((TS, 2 * M_V_W), bf16),
            pltpu.VMEM((TS, D_MODEL), bf16),
            pltpu.VMEM((M_HEADS, M_QK, 2 * M_V), f32),
            pltpu.VMEM((M_HEADS, 1, 1), f32),
            pltpu.VMEM((G_V_W, G_QK_W), f32),
            pltpu.VMEM((R_V_W, R_QK_W), f32),
        ],
        compiler_params=pltpu.CompilerParams(
            dimension_semantics=("arbitrary", "arbitrary"), vmem_limit_bytes=VMEM_LIMIT),
        name="mixer_layer",
    )(h, cosf, sins, gpre, gpost, win, wout, convw, ifb, wup, gab, mnorm, gnorm, rnorm, *consts)


def _ffn_kernel(h_ref, gpre_ref, gpost_ref, w1_ref, w2_ref, out_ref):
    h_in = h_ref[...]
    u = _bf(_rms(h_in, gpre_ref[0]))
    acc = jnp.zeros((TM, D_MODEL), f32)
    for lo in range(0, D_FF, FF_BLK):
        hid = _dot(u, w1_ref[:, lo:lo + FF_BLK])
        hid = jnp.square(jnp.maximum(hid, 0.0))
        acc = acc + _dot(_bf(hid), w2_ref[lo:lo + FF_BLK, :])
    out_ref[...] = h_in + _rms(acc, gpost_ref[0])


def _ffn_layer(layer, h2, gpre, gpost, w1, w2):
    n_tok, D = h2.shape
    tile = pl.BlockSpec((TM, D), lambda i: (i, 0))
    return pl.pallas_call(
        _ffn_kernel,
        grid=(n_tok // TM,),
        in_specs=[tile,
                  pl.BlockSpec((1, 1, D), lambda i: (layer, 0, 0)),
                  pl.BlockSpec((1, 1, D), lambda i: (layer, 0, 0)),
                  pl.BlockSpec((None, D, D_FF), lambda i: (layer, 0, 0)),
                  pl.BlockSpec((None, D_FF, D), lambda i: (layer, 0, 0))],
        out_specs=tile,
        out_shape=jax.ShapeDtypeStruct(h2.shape, h2.dtype),
        compiler_params=pltpu.CompilerParams(
            dimension_semantics=("arbitrary",), vmem_limit_bytes=VMEM_LIMIT),
        name="ffn_layer",
    )(h2, gpre, gpost, w1, w2)


def _pack_w_in(w_in):
    def seg(i):
        return w_in[:, :, IN_OFFS[i]:IN_OFFS[i + 1]]
    (m_q, m_k, m_v, m_i, m_f, m_o, g_q, g_k, g_v, g_a, g_g, r_q, r_k, r_v, r_g) = [seg(i) for i in range(15)]
    pad = jnp.zeros(w_in.shape[:2] + (LANES - 2 * M_HEADS - G_RANK,), w_in.dtype)
    packed = jnp.concatenate([m_q, m_k, m_v, m_o, m_i, m_f, g_a, pad,
                              g_q, g_k, g_v, g_g, r_q, r_k, r_v, r_g], axis=-1)
    return packed.astype(bf16)


def kernel(x, positions, norm_pre_mix, norm_post_mix, norm_pre_ffn, norm_post_ffn, w_in, mlstm_conv_w, mlstm_i_bias, mlstm_f_bias, mlstm_norm, gla_w_up, gla_gate_bias, gla_norm, ret_norm, w_out, w_ff1, w_ff2):
    B, S, D = x.shape
    t_np, m_np = _chunk_constants()
    blk_np, seg_np, ones_np = _head_constants()
    consts = ((jnp.asarray(t_np, bf16), jnp.asarray(m_np, bf16))
              + tuple(jnp.asarray(a) for a in _ret_constants())
              + (jnp.asarray(blk_np), jnp.asarray(seg_np, bf16), jnp.asarray(ones_np, bf16)))

    cosf, sins = _rope_tables(positions)

    ifb = jnp.concatenate([mlstm_i_bias, mlstm_f_bias,
                           jnp.zeros((DEPTH, LANES - 2 * M_HEADS), f32)], axis=-1)
    wup = jnp.zeros((DEPTH, LANES, G_QK_W), f32).at[:, GATE_A:GATE_A + G_RANK, :].set(gla_w_up)
    row = lambda a: a[:, None, :]
    params = (row(norm_pre_mix), row(norm_post_mix), _pack_w_in(w_in), w_out.astype(bf16),
              mlstm_conv_w, row(ifb), wup.astype(bf16), row(gla_gate_bias),
              row(mlstm_norm), row(gla_norm), row(ret_norm))
    gpre_f, gpost_f = row(norm_pre_ffn), row(norm_post_ffn)
    w1_b = w_ff1.astype(bf16)
    w2_b = w_ff2.astype(bf16)

    h = x
    for l in range(DEPTH):
        h = _mixer_layer(l, h, cosf, sins, params, consts)
        h = _ffn_layer(l, h.reshape(B * S, D), gpre_f, gpost_f, w1_b, w2_b).reshape(B, S, D)
    return h
```

```python
import functools

import numpy as np
import jax
import jax.numpy as jnp
from jax import lax
from jax.experimental import pallas as pl
from jax.experimental.pallas import tpu as pltpu

D_MODEL = 1024
DEPTH = 4
EPS = 1e-6
M_HEADS, M_V, M_QK, M_CONV = 4, 128, 64, 4
G_HEADS, G_V, G_QK, G_RANK = 4, 64, 32, 16
G_NORMALIZER = 16.0
R_HEADS, R_V, R_QK = 4, 64, 32
ROPE_BASE = 10000.0
D_FF = 4 * D_MODEL

M_QK_W, M_V_W = M_HEADS * M_QK, M_HEADS * M_V
G_QK_W, G_V_W = G_HEADS * G_QK, G_HEADS * G_V
R_QK_W, R_V_W = R_HEADS * R_QK, R_HEADS * R_V
IN_SIZES = (M_QK_W, M_QK_W, M_V_W, M_HEADS, M_HEADS, M_V_W,
            G_QK_W, G_QK_W, G_V_W, G_RANK, G_V_W,
            R_QK_W, R_QK_W, R_V_W, R_V_W)
IN_OFFS = tuple(int(v) for v in np.cumsum((0,) + IN_SIZES))

LANES = 128
C_MQK = 0
C_MV = C_MQK + 2 * M_QK_W
C_GATE = C_MV + M_V_W
C_MO = C_GATE + LANES
C_GQ = 0
C_GK = C_GQ + G_QK_W
C_GV = C_GK + G_QK_W
C_GG = C_GV + G_V_W
C_RQ = 0
C_RK = C_RQ + R_QK_W
C_RV = C_RK + R_QK_W
C_RG = C_RV + R_V_W
W_M0 = 0
W_G0 = C_MO + M_V_W
W_R0 = W_G0 + C_GG + G_V_W
P_COLS = W_R0 + C_RG + R_V_W
GATE_I, GATE_F, GATE_A = 0, M_HEADS, 2 * M_HEADS

LC = 128
N_LEVELS = 7
TS = 512
TM = 512
FF_BLK = 1024
VMEM_LIMIT = 56 * 1024 * 1024

_NT = (((1,), (1,)), ((), ()))
f32 = jnp.float32
bf16 = jnp.bfloat16


def _bf(x):
    return x.astype(bf16)


def _dot(a, b):
    return jnp.dot(a, b, preferred_element_type=f32)


def _dot_nt(a, b):
    return lax.dot_general(a, b, _NT, preferred_element_type=f32)


def _rms(x, g):
    return x * lax.rsqrt(jnp.mean(x * x, axis=-1, keepdims=True) + EPS) * g


def _log_sigmoid(x):
    return jnp.minimum(x, 0.0) - jnp.log(1.0 + jnp.exp(-jnp.abs(x)))


def _sigmoid(x):
    return 0.5 * jnp.tanh(0.5 * x) + 0.5


def _silu(x):
    half = 0.5 * x
    return half + half * jnp.tanh(half)


def _split3(x):
    hi = _bf(x)
    r1 = x - hi.astype(f32)
    mid = _bf(r1)
    lo = _bf(r1 - mid.astype(f32))
    return jnp.concatenate([hi, mid, lo], axis=0)


def _split2(x):
    hi = _bf(x)
    return jnp.concatenate([hi, _bf(x - hi.astype(f32))], axis=0)


def _chunk_constants():
    L = LC
    T = np.zeros((N_LEVELS + 2, L, L), np.float32)
    M = np.zeros((N_LEVELS + 1, L, L), np.float32)
    idx = np.arange(L)
    for li in range(N_LEVELS):
        s = L >> (li + 1)
        for i in range(L):
            seg = i // s
            if seg % 2 == 1:
                T[li, i, seg * s:i + 1] = 1.0
            else:
                T[li, i, i + 1:seg * s + s] = 1.0
        same_blk = (idx[:, None] // (2 * s)) == (idx[None, :] // (2 * s))
        M[li] = same_blk & ((idx[:, None] // s) % 2 == 1) & ((idx[None, :] // s) % 2 == 0)
    M[N_LEVELS] = np.eye(L)
    for i in range(L):
        T[N_LEVELS, i, :i + 1] = 1.0
        T[N_LEVELS + 1, i, i + 1:] = 1.0
    T = T.reshape((N_LEVELS + 2) * L, L)
    return np.concatenate([T, T, T], axis=1), M


def _ret_constants():
    L = LC
    log_gamma = np.log1p(-np.exp2(-5.0 - np.arange(R_HEADS, dtype=np.float64)))
    i = np.arange(L)
    rel = (i[:, None] - i[None, :]).astype(np.float64)
    D = np.where(rel[None] >= 0, np.exp(log_gamma[:, None, None] * rel[None]), 0.0)
    lg_lane = log_gamma[np.repeat(np.arange(R_HEADS), R_QK)][None, :]
    qdec = np.exp(lg_lane * (i[:, None] + 1.0))
    kdec = np.exp(lg_lane * (L - 1.0 - i[:, None]))
    sdec = np.exp(lg_lane * L)
    return tuple(a.astype(np.float32) for a in (D, qdec, kdec, sdec))


def _head_constants():
    blk = (np.arange(G_V_W)[:, None] // G_V) == (np.arange(G_QK_W)[None, :] // G_QK)
    seg = ((np.arange(G_V_W)[:, None] // G_V) == (np.arange(G_V_W)[None, :] // G_V)) / float(G_V)
    ones_mean = np.kron(np.eye(2), np.full((M_V, M_V), 1.0 / M_V))
    return blk.astype(np.float32), seg.astype(np.float32), ones_mean.astype(np.float32)


def _rope_kernel(pos_ref, invf_ref, sgn_ref, cos_ref, sin_ref):
    ang = pos_ref[0].astype(f32) * invf_ref[...]
    cos_ref[0] = jnp.cos(ang)
    sin_ref[0] = jnp.sin(ang) * sgn_ref[...]


def _rope_tables(positions):
    B, S = positions.shape
    r = np.arange(LANES) % R_QK
    inv_freq = ROPE_BASE ** (-jnp.arange(0, R_QK, 2, dtype=f32) / R_QK)
    invf = jnp.tile(inv_freq, LANES // (R_QK // 2))[None, :]
    sgn = jnp.asarray(np.where(r < R_QK // 2, -1.0, 1.0).astype(np.float32)[None, :])
    return pl.pallas_call(
        _rope_kernel,
        grid=(B,),
        in_specs=[pl.BlockSpec((1, S, 1), lambda b: (b, 0, 0)),
                  pl.BlockSpec((1, LANES), lambda b: (0, 0)),
                  pl.BlockSpec((1, LANES), lambda b: (0, 0))],
        out_specs=[pl.BlockSpec((1, S, LANES), lambda b: (b, 0, 0)),
                   pl.BlockSpec((1, S, LANES), lambda b: (b, 0, 0))],
        out_shape=[jax.ShapeDtypeStruct((B, S, LANES), f32)] * 2,
        name="rope_tables",
    )(positions[:, :, None], invf, sgn)


def _mixer_kernel(n_t, hc_ref, cos_ref, sin_ref, gpre_ref, gpost_ref, win_ref, wout_ref,
                  convw_ref, ifb_ref, wup_ref, gab_ref, mnorm_ref, gnorm_ref, rnorm_ref,
                  t_ref, lmb_ref, d_ref, qdec_ref, kdec_ref, sdec_ref, blk_ref, seg_ref, onesm_ref,
                  out_ref,
                  pm_ref, pg_ref, pr_ref, qkpad_ref, la_ref, e_ref, vext_ref, y_ref,
                  mc_ref, mm_ref, gs_ref, rs_ref):
    step = pl.program_id(0)
    rm_ref, rg_ref, rr_ref = pm_ref, pg_ref, pr_ref

    def project(h_tile):
        u = _bf(_rms(h_tile, gpre_ref[0]))
        rm_ref[...] = _dot(u, win_ref[:, W_M0:W_G0])
        rg_ref[...] = _dot(u, win_ref[:, W_G0:W_R0])
        rr_ref[...] = _dot(u, win_ref[:, W_R0:P_COLS])

    def preprocess(cosf, sins):
        qkpad_ref[8:8 + TS, :] = rm_ref[:, C_MQK:C_MQK + 2 * M_QK_W]
        conv = jnp.zeros((TS, 2 * M_QK_W), f32)
        for j in range(M_CONV):
            conv = conv + convw_ref[0, j:j + 1, :] * qkpad_ref[5 + j:5 + j + TS, :]
        qkpad_ref[0:8, :] = qkpad_ref[TS:TS + 8, :]
        qk = _silu(conv)
        lane_qk = lax.broadcasted_iota(jnp.int32, (1, 2 * M_QK_W), 1)
        pm_ref[:, C_MQK:C_MQK + 2 * M_QK_W] = qk * jnp.where(lane_qk < M_QK_W, 1.0, M_QK ** -0.5)
        for hd in range(M_HEADS):
            vext_ref[:, 2 * hd * M_V:(2 * hd + 1) * M_V] = _bf(
                rm_ref[:, C_MV + hd * M_V:C_MV + (hd + 1) * M_V])
        graw = rm_ref[:, C_GATE:C_GATE + LANES]
        gb = graw + ifb_ref[0]
        lane_g = lax.broadcasted_iota(jnp.int32, (TS, LANES), 1)
        pm_ref[:, C_GATE:C_GATE + LANES] = jnp.where(lane_g < GATE_F, gb, _log_sigmoid(gb))
        a_pre = _dot(_bf(graw), wup_ref[0]) + gab_ref[0]
        la_ref[...] = _log_sigmoid(a_pre) * (1.0 / G_NORMALIZER)
        pg_ref[:, C_GQ:C_GQ + G_QK_W] = rg_ref[:, C_GQ:C_GQ + G_QK_W] * (G_QK ** -0.5)
        first_half = (lane_g % R_QK) < (R_QK // 2)
        for col, scale in ((C_RQ, 1.0), (C_RK, R_QK ** -0.5)):
            xr = rr_ref[:, col:col + R_QK_W]
            swapped = jnp.where(first_half, pltpu.roll(xr, LANES - R_QK // 2, axis=1),
                                pltpu.roll(xr, R_QK // 2, axis=1))
            pr_ref[:, col:col + R_QK_W] = (xr * cosf + swapped * sins) * scale

    @pl.when(step == 0)
    def _():
        for hd in range(M_HEADS):
            vext_ref[:, (2 * hd + 1) * M_V:(2 * hd + 2) * M_V] = jnp.ones((TS, M_V), bf16)

    @pl.when(step % n_t == 0)
    def _():
        qkpad_ref[0:8, :] = jnp.zeros((8, 2 * M_QK_W), f32)
        mc_ref[...] = jnp.zeros_like(mc_ref)
        mm_ref[...] = jnp.zeros_like(mm_ref)
        gs_ref[...] = jnp.zeros_like(gs_ref)
        rs_ref[...] = jnp.zeros_like(rs_ref)

    h_in = hc_ref[0]
    project(h_in)
    preprocess(cos_ref[0], sin_ref[0])

    row_i = lax.broadcasted_iota(jnp.int32, (LC, LC), 0)
    col_j = lax.broadcasted_iota(jnp.int32, (LC, LC), 1)
    causal = row_i >= col_j
    top_rows = row_i < M_QK
    left_lanes = col_j < G_V
    lane1 = lax.broadcasted_iota(jnp.int32, (1, LANES), 1)
    pair_mask = [(lane1 < M_QK).astype(f32), (lane1 >= M_QK).astype(f32)]
    pair_mask_b = [m.astype(bf16) for m in pair_mask]
    head_mask_b = [((lane1 // G_QK) == hd).astype(bf16) for hd in range(G_HEADS)]

    def masked_rhs(kb):
        return jnp.concatenate([kb * m for m in head_mask_b], axis=0)

    def pair_tiles(p_heads, vb):
        tiles = []
        for tp in range(2):
            vt = vb[:, tp * LANES:(tp + 1) * LANES]
            tiles.append(_dot(jnp.concatenate([p_heads[2 * tp], p_heads[2 * tp + 1]], axis=1),
                              jnp.concatenate([vt * pair_mask_b[0], vt * pair_mask_b[1]], axis=0)))
        return jnp.concatenate(tiles, axis=1)

    def head_norm_gate(hcat, gain, gate):
        ms = _dot(_bf(hcat * hcat), seg_ref[...])
        return _bf(hcat * lax.rsqrt(ms + EPS) * gain * _silu(gate))

    n_ch = TS // LC
    chunks = range(n_ch)
    heads = range(M_HEADS)
    rows = [slice(c * LC, (c + 1) * LC) for c in chunks]
    t_cum = t_ref[N_LEVELS * LC:(N_LEVELS + 1) * LC, :]

    gblk, bcum, qm, s_m, s_r, m_k, gq, gk, rq, rk = ([] for _ in range(10))
    for c in range(0, n_ch, 2):
        gblk += [pm_ref[rows[c + i], C_GATE:C_GATE + LANES] for i in range(2)]
        cum2 = _dot(t_cum, jnp.concatenate([_split3(gblk[c]), _split3(gblk[c + 1])], axis=1))
        bcum += [cum2[:, :LANES], cum2[:, LANES:]]
        e2 = _dot(t_ref[:, :2 * LC], jnp.concatenate(
            [_split2(la_ref[rows[c], :]), _split2(la_ref[rows[c + 1], :])], axis=1))
        e_ref[c] = e2[:, :LANES]
        e_ref[c + 1] = e2[:, LANES:]
    for c in chunks:
        m_q = [pm_ref[rows[c], C_MQK + tp * LANES:C_MQK + (tp + 1) * LANES] for tp in range(2)]
        m_k.append([pm_ref[rows[c], C_MQK + M_QK_W + tp * LANES:C_MQK + M_QK_W + (tp + 1) * LANES]
                    for tp in range(2)])
        qm.append([m_q[hd // 2] * pair_mask[hd % 2] for hd in heads])
        s_m.append([])
        for tp in range(2):
            kb = _bf(m_k[c][tp])
            s2 = _dot_nt(_bf(m_q[tp]), jnp.concatenate([kb * pair_mask_b[0], kb * pair_mask_b[1]], axis=0))
            s_m[c] += [s2[:, :LC], s2[:, LC:]]

    g_row, b_col, rmax = [], [], []
    for c in chunks:
        xg = jnp.where(col_j < GATE_F, gblk[c], bcum[c])
        xgt = xg.T
        g_row.append([xgt[GATE_I + hd:GATE_I + hd + 1, :] - xgt[GATE_F + hd:GATE_F + hd + 1, :]
                      for hd in heads])
        b_col.append([xg[:, GATE_F + hd:GATE_F + hd + 1] for hd in heads])
        rmax.append([jnp.max(jnp.where(causal, g_row[c][hd], -jnp.inf), axis=1, keepdims=True)
                     for hd in heads])

    m_run = [mm_ref[hd] for hd in heads]
    m_prev, s_old, w_row = [], [], []
    for c in chunks:
        m_prev.append(list(m_run))
        s_old.append([])
        w_row.append([])
        for hd in heads:
            g_max = rmax[c][hd][LC - 1:LC, :]
            b_last = b_col[c][hd][LC - 1:LC, :]
            a_max = b_last + g_max
            m_new = jnp.maximum(b_last + m_run[hd], a_max)
            s_old[c].append(jnp.exp(b_last + m_run[hd] - m_new))
            w_row[c].append(jnp.exp(g_row[c][hd] - g_max) * jnp.exp(a_max - m_new))
            m_run[hd] = m_new
    for hd in heads:
        mm_ref[hd] = m_run[hd]

    mu = [[jnp.maximum(rmax[c][hd], m_prev[c][hd]) for hd in heads] for c in chunks]
    p_m = []
    for c in chunks:
        p_m.append([_bf(jnp.exp(jnp.where(causal, g_row[c][hd], -jnp.inf) - mu[c][hd]) * s_m[c][hd])
                    for hd in heads])

    for c in chunks:
        rq.append(pr_ref[rows[c], C_RQ:C_RQ + R_QK_W])
        rk.append(pr_ref[rows[c], C_RK:C_RK + R_QK_W])
        s_r.append(_dot_nt(_bf(rq[c]), masked_rhs(_bf(rk[c]))))
        gq.append(pg_ref[rows[c], C_GQ:C_GQ + G_QK_W])
        gk.append(pg_ref[rows[c], C_GK:C_GK + G_QK_W])

    RB = 16
    n_rb = LC // RB

    def add_level(acc_c, x, li, blocks):
        for hd in range(G_HEADS):
            for n, b in enumerate(blocks):
                term = (lmb_ref[li, b * RB:(b + 1) * RB, :]
                        * _bf(x[n * RB:(n + 1) * RB, hd * LC:(hd + 1) * LC]))
                acc_c[hd][b] = term if acc_c[hd][b] is None else acc_c[hd][b] + term

    acc = []
    for c in chunks:
        acc.append([[None] * n_rb for _ in range(G_HEADS)])
        add_level(acc[c], _dot_nt(_bf(gq[c]), masked_rhs(_bf(gk[c]))), N_LEVELS, list(range(n_rb)))
    for li in range(N_LEVELS):
        seg = LC >> (li + 1)
        if seg >= RB:
            blocks = [b for b in range(n_rb) if ((b * RB) // seg) % 2 == 1]
        else:
            blocks = list(range(n_rb))
        for c in chunks:
            pw = jnp.exp(e_ref[c, li * LC:(li + 1) * LC, :])
            qs = _bf(gq[c] * pw)
            if len(blocks) < n_rb:
                qs = jnp.concatenate([qs[b * RB:(b + 1) * RB] for b in blocks], axis=0)
            add_level(acc[c], _dot_nt(qs, masked_rhs(_bf(gk[c] * pw))), li, blocks)
    acc = [[jnp.concatenate(acc[c][hd], axis=0) for hd in range(G_HEADS)] for c in chunks]

    vexts, u_m, intra_g, u_g, qin_g, dec_g, intra_r, u_r, qin_r = ([] for _ in range(9))
    for c in chunks:
        vext = [vext_ref[rows[c], 2 * hd * M_V:(2 * hd + 2) * M_V] for hd in heads]
        vexts.append(vext)
        u_m.append([])
        for tp in range(2):
            kw_t = _bf(m_k[c][tp].T * jnp.where(top_rows, w_row[c][2 * tp], w_row[c][2 * tp + 1]))
            for e in range(2):
                u_m[c].append(_dot(kw_t[e * M_QK:(e + 1) * M_QK], vext[2 * tp + e]))
        p_ret = [_bf(d_ref[hd] * s_r[c][:, hd * LC:(hd + 1) * LC]) for hd in range(R_HEADS)]
        rv = pr_ref[rows[c], C_RV:C_RV + R_V_W]
        intra_r.append(pair_tiles(p_ret, _bf(rv)))
        u_r.append(blk_ref[...] * _dot(_bf(rv.T), _bf(rk[c] * kdec_ref[...])))
        qin_r.append(_bf(rq[c] * qdec_ref[...]))
        pb = jnp.exp(e_ref[c, N_LEVELS * LC:(N_LEVELS + 1) * LC, :])
        pk = jnp.exp(e_ref[c, (N_LEVELS + 1) * LC:(N_LEVELS + 2) * LC, :])
        gv = pg_ref[rows[c], C_GV:C_GV + G_V_W]
        intra_g.append(pair_tiles(acc[c], _bf(gv)))
        u_g.append(blk_ref[...] * _dot(_bf(gv.T), _bf(gk[c] * pk)))
        qin_g.append(_bf(gq[c] * pb))
        dec_g.append(pb[LC - 1:LC, :])

    c_st = [mc_ref[hd] for hd in heads]
    g_st = gs_ref[...]
    r_st = rs_ref[...]
    zero_half = jnp.zeros((M_QK, 2 * M_V), bf16)
    ne_m, inter_g, inter_r = [], [], []
    for c in chunks:
        ne_m.append([])
        for hd in heads:
            cb = _bf(c_st[hd])
            padded = jnp.concatenate([cb, zero_half] if hd % 2 == 0 else [zero_half, cb], axis=0)
            q_in = _bf(qm[c][hd] * jnp.exp(m_prev[c][hd] - mu[c][hd]))
            ne_m[c].append(_dot(jnp.concatenate([p_m[c][hd], q_in], axis=1),
                                jnp.concatenate([vexts[c][hd], padded], axis=0)))
            c_st[hd] = s_old[c][hd] * c_st[hd] + u_m[c][hd]
        inter_g.append(_dot_nt(qin_g[c], _bf(g_st)))
        g_st = g_st * dec_g[c] + u_g[c]
        inter_r.append(_dot_nt(qin_r[c], _bf(r_st)))
        r_st = r_st * sdec_ref[...] + u_r[c]
    for hd in heads:
        mc_ref[hd] = c_st[hd]
    gs_ref[...] = g_st
    rs_ref[...] = r_st

    for c in chunks:
        for tp in range(2):
            hm = []
            for hd in (2 * tp, 2 * tp + 1):
                ne = ne_m[c][hd]
                num = ne[:, :M_V]
                den = ne[:, M_V:]
                hm.append(num / jnp.maximum(jnp.abs(den), jnp.exp(-(b_col[c][hd] + mu[c][hd]))))
            hm = jnp.concatenate(hm, axis=1)
            ms = _dot(_bf(hm * hm), onesm_ref[...])
            cols = slice(2 * tp * M_V, (2 * tp + 2) * M_V)
            o_gate = _sigmoid(pm_ref[rows[c], C_MO + 2 * tp * M_V:C_MO + (2 * tp + 2) * M_V])
            y_ref[rows[c], cols] = _bf(hm * lax.rsqrt(ms + EPS) * mnorm_ref[0, :, cols] * o_gate)
        y_ref[rows[c], M_V_W:M_V_W + G_V_W] = head_norm_gate(
            intra_g[c] + inter_g[c], gnorm_ref[0], pg_ref[rows[c], C_GG:C_GG + G_V_W])
        y_ref[rows[c], M_V_W + G_V_W:M_V_W + G_V_W + R_V_W] = head_norm_gate(
            intra_r[c] + inter_r[c], rnorm_ref[0], pr_ref[rows[c], C_RG:C_RG + R_V_W])

    mixed = _dot(y_ref[...], wout_ref[...])
    out_ref[0] = h_in + _rms(mixed, gpost_ref[0])


def _mixer_layer(layer, h, cosf, sins, params, consts):
    B, S, D = h.shape
    n_t = S // TS
    n_steps = B * n_t

    def tile_of(step):
        return step // n_t, step % n_t, 0

    tile = pl.BlockSpec((1, TS, D), tile_of)
    rope = pl.BlockSpec((1, TS, LANES), tile_of)

    def weight_spec(a):
        return pl.BlockSpec((None,) + a.shape[1:], lambda s: (layer, 0, 0))

    def small_spec(a):
        return pl.BlockSpec((1,) + a.shape[1:], lambda s: (layer, 0, 0))

    def const_spec(a):
        nd = a.ndim
        return pl.BlockSpec(a.shape, lambda s: (0,) * nd)

    gpre, gpost, win, wout, convw, ifb, wup, gab, mnorm, gnorm, rnorm = params
    in_specs = ([tile, rope, rope, small_spec(gpre), small_spec(gpost),
                 weight_spec(win), weight_spec(wout)]
                + [small_spec(a) for a in (convw, ifb, wup, gab, mnorm, gnorm, rnorm)]
                + [const_spec(a) for a in consts])
    return pl.pallas_call(
        functools.partial(_mixer_kernel, n_t),
        grid=(n_steps,),
        in_specs=in_specs,
        out_specs=tile,
        out_shape=jax.ShapeDtypeStruct(h.shape, h.dtype),
        scratch_shapes=[
            pltpu.VMEM((TS, W_G0 - W_M0), f32),
            pltpu.VMEM((TS, W_R0 - W_G0), f32),
            pltpu.VMEM((TS, P_COLS - W_R0), f32),
            pltpu.VMEM((TS + 8, 2 * M_QK_W), f32),
            pltpu.VMEM((TS, LANES), f32),
            pltpu.VMEM((TS // LC, (N_LEVELS + 2) * LC, LANES), f32),
            pltpu.VMEM((TS, 2 * M_V_W), bf16),
            pltpu.VMEM((TS, D_MODEL), bf16),
            pltpu.VMEM((M_HEADS, M_QK, 2 * M_V), f32),
            pltpu.VMEM((M_HEADS, 1, 1), f32),
            pltpu.VMEM((G_V_W, G_QK_W), f32),
            pltpu.VMEM((R_V_W, R_QK_W), f32),
        ],
        compiler_params=pltpu.CompilerParams(
            dimension_semantics=("arbitrary",), vmem_limit_bytes=VMEM_LIMIT),
        name="mixer_layer",
    )(h, cosf, sins, gpre, gpost, win, wout, convw, ifb, wup, gab, mnorm, gnorm, rnorm, *consts)


def _ffn_kernel(h_ref, gpre_ref, gpost_ref, w1_ref, w2_ref, out_ref):
    h_in = h_ref[...]
    u = _bf(_rms(h_in, gpre_ref[0]))
    acc = jnp.zeros((TM, D_MODEL), f32)
    for lo in range(0, D_FF, FF_BLK):
        hid = _dot(u, w1_ref[:, lo:lo + FF_BLK])
        hid = jnp.square(jnp.maximum(hid, 0.0))
        acc = acc + _dot(_bf(hid), w2_ref[lo:lo + FF_BLK, :])
    out_ref[...] = h_in + _rms(acc, gpost_ref[0])


def _ffn_layer(layer, h2, gpre, gpost, w1, w2):
    n_tok, D = h2.shape
    tile = pl.BlockSpec((TM, D), lambda i: (i, 0))
    return pl.pallas_call(
        _ffn_kernel,
        grid=(n_tok // TM,),
        in_specs=[tile,
                  pl.BlockSpec((1, 1, D), lambda i: (layer, 0, 0)),
                  pl.BlockSpec((1, 1, D), lambda i: (layer, 0, 0)),
                  pl.BlockSpec((None, D, D_FF), lambda i: (layer, 0, 0)),
                  pl.BlockSpec((None, D_FF, D), lambda i: (layer, 0, 0))],
        out_specs=tile,
        out_shape=jax.ShapeDtypeStruct(h2.shape, h2.dtype),
        compiler_params=pltpu.CompilerParams(
            dimension_semantics=("arbitrary",), vmem_limit_bytes=VMEM_LIMIT),
        name="ffn_layer",
    )(h2, gpre, gpost, w1, w2)


def _pack_w_in(w_in):
    def cols(first, last):
        return w_in[:, :, IN_OFFS[first]:IN_OFFS[last + 1]]
    pad = jnp.zeros(w_in.shape[:2] + (LANES - 2 * M_HEADS - G_RANK,), w_in.dtype)
    packed = jnp.concatenate([cols(0, 2), cols(3, 4), cols(9, 9), pad, cols(5, 8), cols(10, 14)], axis=-1)
    return packed.astype(bf16)


def kernel(x, positions, norm_pre_mix, norm_post_mix, norm_pre_ffn, norm_post_ffn, w_in, mlstm_conv_w, mlstm_i_bias, mlstm_f_bias, mlstm_norm, gla_w_up, gla_gate_bias, gla_norm, ret_norm, w_out, w_ff1, w_ff2):
    B, S, D = x.shape
    t_np, m_np = _chunk_constants()
    blk_np, seg_np, ones_np = _head_constants()
    consts = ((jnp.asarray(t_np, bf16), jnp.asarray(m_np, bf16))
              + tuple(jnp.asarray(a) for a in _ret_constants())
              + (jnp.asarray(blk_np), jnp.asarray(seg_np, bf16), jnp.asarray(ones_np, bf16)))

    cosf, sins = _rope_tables(positions)

    ifb = jnp.concatenate([mlstm_i_bias, mlstm_f_bias,
                           jnp.zeros((DEPTH, LANES - 2 * M_HEADS), f32)], axis=-1)
    wup = jnp.zeros((DEPTH, LANES, G_QK_W), f32).at[:, GATE_A:GATE_A + G_RANK, :].set(gla_w_up)
    row = lambda a: a[:, None, :]
    params = (row(norm_pre_mix), row(norm_post_mix), _pack_w_in(w_in), w_out.astype(bf16),
              mlstm_conv_w, row(ifb), wup.astype(bf16), row(gla_gate_bias),
              row(mlstm_norm), row(gla_norm), row(ret_norm))
    gpre_f, gpost_f = row(norm_pre_ffn), row(norm_post_ffn)
    w1_b = w_ff1.astype(bf16)
    w2_b = w_ff2.astype(bf16)

    h = x
    for l in range(DEPTH):
        h = _mixer_layer(l, h, cosf, sins, params, consts)
        h = _ffn_layer(l, h.reshape(B * S, D), gpre_f, gpost_f, w1_b, w2_b).reshape(B, S, D)
    return h
```

```python
import functools

import numpy as np
import jax
import jax.numpy as jnp
from jax import lax
from jax.experimental import pallas as pl
from jax.experimental.pallas import tpu as pltpu

D_MODEL = 1024
DEPTH = 4
EPS = 1e-6
M_HEADS, M_V, M_QK, M_CONV = 4, 128, 64, 4
G_HEADS, G_V, G_QK, G_RANK = 4, 64, 32, 16
G_NORMALIZER = 16.0
R_HEADS, R_V, R_QK = 4, 64, 32
ROPE_BASE = 10000.0
D_FF = 4 * D_MODEL

M_QK_W, M_V_W = M_HEADS * M_QK, M_HEADS * M_V
G_QK_W, G_V_W = G_HEADS * G_QK, G_HEADS * G_V
R_QK_W, R_V_W = R_HEADS * R_QK, R_HEADS * R_V
IN_SIZES = (M_QK_W, M_QK_W, M_V_W, M_HEADS, M_HEADS, M_V_W,
            G_QK_W, G_QK_W, G_V_W, G_RANK, G_V_W,
            R_QK_W, R_QK_W, R_V_W, R_V_W)
IN_OFFS = tuple(int(v) for v in np.cumsum((0,) + IN_SIZES))

LANES = 128
C_MQK = 0
C_MV = C_MQK + 2 * M_QK_W
C_GATE = C_MV + M_V_W
C_MO = C_GATE + LANES
C_GQ = 0
C_GK = C_GQ + G_QK_W
C_GV = C_GK + G_QK_W
C_GG = C_GV + G_V_W
C_RQ = 0
C_RK = C_RQ + R_QK_W
C_RV = C_RK + R_QK_W
C_RG = C_RV + R_V_W
W_M0 = 0
W_G0 = C_MO + M_V_W
W_R0 = W_G0 + C_GG + G_V_W
P_COLS = W_R0 + C_RG + R_V_W
GATE_I, GATE_F, GATE_A = 0, M_HEADS, 2 * M_HEADS

LC = 128
N_LEVELS = 7
TS = 512
TM = 512
FF_BLK = 1024
VMEM_LIMIT = 56 * 1024 * 1024

_NT = (((1,), (1,)), ((), ()))
f32 = jnp.float32
bf16 = jnp.bfloat16


def _bf(x):
    return x.astype(bf16)


def _dot(a, b):
    return jnp.dot(a, b, preferred_element_type=f32)


def _dot_nt(a, b):
    return lax.dot_general(a, b, _NT, preferred_element_type=f32)


def _rms(x, g):
    return x * lax.rsqrt(jnp.mean(x * x, axis=-1, keepdims=True) + EPS) * g


def _log_sigmoid(x):
    return jnp.minimum(x, 0.0) - jnp.log(1.0 + jnp.exp(-jnp.abs(x)))


def _sigmoid(x):
    return 0.5 * jnp.tanh(0.5 * x) + 0.5


def _silu(x):
    half = 0.5 * x
    return half + half * jnp.tanh(half)


def _split3(x):
    hi = _bf(x)
    r1 = x - hi.astype(f32)
    mid = _bf(r1)
    lo = _bf(r1 - mid.astype(f32))
    return jnp.concatenate([hi, mid, lo], axis=0)


def _split2(x):
    hi = _bf(x)
    return jnp.concatenate([hi, _bf(x - hi.astype(f32))], axis=0)


def _chunk_constants():
    L = LC
    T = np.zeros((N_LEVELS + 2, L, L), np.float32)
    M = np.zeros((N_LEVELS + 1, L, L), np.float32)
    idx = np.arange(L)
    for li in range(N_LEVELS):
        s = L >> (li + 1)
        for i in range(L):
            seg = i // s
            if seg % 2 == 1:
                T[li, i, seg * s:i + 1] = 1.0
            else:
                T[li, i, i + 1:seg * s + s] = 1.0
        same_blk = (idx[:, None] // (2 * s)) == (idx[None, :] // (2 * s))
        M[li] = same_blk & ((idx[:, None] // s) % 2 == 1) & ((idx[None, :] // s) % 2 == 0)
    M[N_LEVELS] = np.eye(L)
    for i in range(L):
        T[N_LEVELS, i, :i + 1] = 1.0
        T[N_LEVELS + 1, i, i + 1:] = 1.0
    T = T.reshape((N_LEVELS + 2) * L, L)
    return np.concatenate([T, T, T], axis=1), M


def _ret_constants():
    L = LC
    log_gamma = np.log1p(-np.exp2(-5.0 - np.arange(R_HEADS, dtype=np.float64)))
    i = np.arange(L)
    rel = (i[:, None] - i[None, :]).astype(np.float64)
    D = np.where(rel[None] >= 0, np.exp(log_gamma[:, None, None] * rel[None]), 0.0)
    lg_lane = log_gamma[np.repeat(np.arange(R_HEADS), R_QK)][None, :]
    qdec = np.exp(lg_lane * (i[:, None] + 1.0))
    kdec = np.exp(lg_lane * (L - 1.0 - i[:, None]))
    sdec = np.exp(lg_lane * L)
    return tuple(a.astype(np.float32) for a in (D, qdec, kdec, sdec))


def _head_constants():
    blk = (np.arange(G_V_W)[:, None] // G_V) == (np.arange(G_QK_W)[None, :] // G_QK)
    seg = ((np.arange(G_V_W)[:, None] // G_V) == (np.arange(G_V_W)[None, :] // G_V)) / float(G_V)
    ones_mean = np.kron(np.eye(2), np.full((M_V, M_V), 1.0 / M_V))
    return blk.astype(np.float32), seg.astype(np.float32), ones_mean.astype(np.float32)


def _rope_kernel(pos_ref, invf_ref, sgn_ref, cos_ref, sin_ref):
    ang = pos_ref[0].astype(f32) * invf_ref[...]
    cos_ref[0] = jnp.cos(ang)
    sin_ref[0] = jnp.sin(ang) * sgn_ref[...]


def _rope_tables(positions):
    B, S = positions.shape
    r = np.arange(LANES) % R_QK
    inv_freq = ROPE_BASE ** (-jnp.arange(0, R_QK, 2, dtype=f32) / R_QK)
    invf = jnp.tile(inv_freq, LANES // (R_QK // 2))[None, :]
    sgn = jnp.asarray(np.where(r < R_QK // 2, -1.0, 1.0).astype(np.float32)[None, :])
    return pl.pallas_call(
        _rope_kernel,
        grid=(B,),
        in_specs=[pl.BlockSpec((1, S, 1), lambda b: (b, 0, 0)),
                  pl.BlockSpec((1, LANES), lambda b: (0, 0)),
                  pl.BlockSpec((1, LANES), lambda b: (0, 0))],
        out_specs=[pl.BlockSpec((1, S, LANES), lambda b: (b, 0, 0)),
                   pl.BlockSpec((1, S, LANES), lambda b: (b, 0, 0))],
        out_shape=[jax.ShapeDtypeStruct((B, S, LANES), f32)] * 2,
        name="rope_tables",
    )(positions[:, :, None], invf, sgn)


def _mixer_kernel(n_t, hc_ref, cos_ref, sin_ref, gpre_ref, gpost_ref, win_ref, wout_ref,
                  convw_ref, ifb_ref, wup_ref, gab_ref, mnorm_ref, gnorm_ref, rnorm_ref,
                  t_ref, lmb_ref, d_ref, qdec_ref, kdec_ref, sdec_ref, blk_ref, seg_ref, onesm_ref,
                  out_ref,
                  pm_ref, pg_ref, pr_ref, qkpad_ref, la_ref, e_ref, vext_ref, y_ref,
                  mc_ref, mm_ref, gs_ref, rs_ref):
    step = pl.program_id(0)
    rm_ref, rg_ref, rr_ref = pm_ref, pg_ref, pr_ref

    def project(h_tile):
        u = _bf(_rms(h_tile, gpre_ref[0]))
        rm_ref[...] = _dot(u, win_ref[:, W_M0:W_G0])
        rg_ref[...] = _dot(u, win_ref[:, W_G0:W_R0])
        rr_ref[...] = _dot(u, win_ref[:, W_R0:P_COLS])

    def preprocess(cosf, sins):
        qkpad_ref[8:8 + TS, :] = rm_ref[:, C_MQK:C_MQK + 2 * M_QK_W]
        conv = jnp.zeros((TS, 2 * M_QK_W), f32)
        for j in range(M_CONV):
            conv = conv + convw_ref[0, j:j + 1, :] * qkpad_ref[5 + j:5 + j + TS, :]
        qkpad_ref[0:8, :] = qkpad_ref[TS:TS + 8, :]
        qk = _silu(conv)
        lane_qk = lax.broadcasted_iota(jnp.int32, (1, 2 * M_QK_W), 1)
        pm_ref[:, C_MQK:C_MQK + 2 * M_QK_W] = qk * jnp.where(lane_qk < M_QK_W, 1.0, M_QK ** -0.5)
        for hd in range(M_HEADS):
            vext_ref[:, 2 * hd * M_V:(2 * hd + 1) * M_V] = _bf(
                rm_ref[:, C_MV + hd * M_V:C_MV + (hd + 1) * M_V])
        graw = rm_ref[:, C_GATE:C_GATE + LANES]
        gb = graw + ifb_ref[0]
        lane_g = lax.broadcasted_iota(jnp.int32, (TS, LANES), 1)
        pm_ref[:, C_GATE:C_GATE + LANES] = jnp.where(lane_g < GATE_F, gb, _log_sigmoid(gb))
        a_pre = _dot(_bf(graw), wup_ref[0]) + gab_ref[0]
        la_ref[...] = _log_sigmoid(a_pre) * (1.0 / G_NORMALIZER)
        pg_ref[:, C_GQ:C_GQ + G_QK_W] = rg_ref[:, C_GQ:C_GQ + G_QK_W] * (G_QK ** -0.5)
        first_half = (lane_g % R_QK) < (R_QK // 2)
        for col, scale in ((C_RQ, 1.0), (C_RK, R_QK ** -0.5)):
            xr = rr_ref[:, col:col + R_QK_W]
            swapped = jnp.where(first_half, pltpu.roll(xr, LANES - R_QK // 2, axis=1),
                                pltpu.roll(xr, R_QK // 2, axis=1))
            pr_ref[:, col:col + R_QK_W] = (xr * cosf + swapped * sins) * scale

    @pl.when(step == 0)
    def _():
        for hd in range(M_HEADS):
            vext_ref[:, (2 * hd + 1) * M_V:(2 * hd + 2) * M_V] = jnp.ones((TS, M_V), bf16)

    @pl.when(step % n_t == 0)
    def _():
        qkpad_ref[0:8, :] = jnp.zeros((8, 2 * M_QK_W), f32)
        mc_ref[...] = jnp.zeros_like(mc_ref)
        mm_ref[...] = jnp.zeros_like(mm_ref)
        gs_ref[...] = jnp.zeros_like(gs_ref)
        rs_ref[...] = jnp.zeros_like(rs_ref)

    h_in = hc_ref[0]
    project(h_in)
    preprocess(cos_ref[0], sin_ref[0])

    row_i = lax.broadcasted_iota(jnp.int32, (LC, LC), 0)
    col_j = lax.broadcasted_iota(jnp.int32, (LC, LC), 1)
    causal = row_i >= col_j
    top_rows = row_i < M_QK
    left_lanes = col_j < G_V
    lane1 = lax.broadcasted_iota(jnp.int32, (1, LANES), 1)
    pair_mask = [(lane1 < M_QK).astype(f32), (lane1 >= M_QK).astype(f32)]
    pair_mask_b = [m.astype(bf16) for m in pair_mask]
    head_mask_b = [((lane1 // G_QK) == hd).astype(bf16) for hd in range(G_HEADS)]

    def masked_rhs(kb):
        return jnp.concatenate([kb * m for m in head_mask_b], axis=0)

    def pair_tiles(p_heads, vb):
        tiles = []
        for tp in range(2):
            vt = vb[:, tp * LANES:(tp + 1) * LANES]
            tiles.append(_dot(jnp.concatenate([p_heads[2 * tp], p_heads[2 * tp + 1]], axis=1),
                              jnp.concatenate([vt * pair_mask_b[0], vt * pair_mask_b[1]], axis=0)))
        return jnp.concatenate(tiles, axis=1)

    def head_norm_gate(hcat, gain, gate):
        ms = _dot(_bf(hcat * hcat), seg_ref[...])
        return _bf(hcat * lax.rsqrt(ms + EPS) * gain * _silu(gate))

    n_ch = TS // LC
    chunks = range(n_ch)
    heads = range(M_HEADS)
    rows = [slice(c * LC, (c + 1) * LC) for c in chunks]
    t_cum = t_ref[N_LEVELS * LC:(N_LEVELS + 1) * LC, :]

    gblk, bcum, qm, s_m, s_r, m_k, gq, gk, rq, rk = ([] for _ in range(10))
    for c in range(0, n_ch, 2):
        gblk += [pm_ref[rows[c + i], C_GATE:C_GATE + LANES] for i in range(2)]
        cum2 = _dot(t_cum, jnp.concatenate([_split3(gblk[c]), _split3(gblk[c + 1])], axis=1))
        bcum += [cum2[:, :LANES], cum2[:, LANES:]]
        e2 = _dot(t_ref[:, :2 * LC], jnp.concatenate(
            [_split2(la_ref[rows[c], :]), _split2(la_ref[rows[c + 1], :])], axis=1))
        e_ref[c] = e2[:, :LANES]
        e_ref[c + 1] = e2[:, LANES:]
    for c in chunks:
        m_q = [pm_ref[rows[c], C_MQK + tp * LANES:C_MQK + (tp + 1) * LANES] for tp in range(2)]
        m_k.append([pm_ref[rows[c], C_MQK + M_QK_W + tp * LANES:C_MQK + M_QK_W + (tp + 1) * LANES]
                    for tp in range(2)])
        qm.append([m_q[hd // 2] * pair_mask[hd % 2] for hd in heads])
        s_m.append([])
        for tp in range(2):
            kb = _bf(m_k[c][tp])
            s2 = _dot_nt(_bf(m_q[tp]), jnp.concatenate([kb * pair_mask_b[0], kb * pair_mask_b[1]], axis=0))
            s_m[c] += [s2[:, :LC], s2[:, LC:]]

    g_row, b_col, rmax = [], [], []
    for c in chunks:
        xg = jnp.where(col_j < GATE_F, gblk[c], bcum[c])
        xgt = xg.T
        g_row.append([xgt[GATE_I + hd:GATE_I + hd + 1, :] - xgt[GATE_F + hd:GATE_F + hd + 1, :]
                      for hd in heads])
        b_col.append([xg[:, GATE_F + hd:GATE_F + hd + 1] for hd in heads])
        rmax.append([jnp.max(jnp.where(causal, g_row[c][hd], -jnp.inf), axis=1, keepdims=True)
                     for hd in heads])

    m_run = [mm_ref[hd] for hd in heads]
    m_prev, s_old, w_row = [], [], []
    for c in chunks:
        m_prev.append(list(m_run))
        s_old.append([])
        w_row.append([])
        for hd in heads:
            g_max = rmax[c][hd][LC - 1:LC, :]
            b_last = b_col[c][hd][LC - 1:LC, :]
            a_max = b_last + g_max
            m_new = jnp.maximum(b_last + m_run[hd], a_max)
            s_old[c].append(jnp.exp(b_last + m_run[hd] - m_new))
            w_row[c].append(jnp.exp(g_row[c][hd] - g_max) * jnp.exp(a_max - m_new))
            m_run[hd] = m_new
    for hd in heads:
        mm_ref[hd] = m_run[hd]

    mu = [[jnp.maximum(rmax[c][hd], m_prev[c][hd]) for hd in heads] for c in chunks]
    p_m = []
    for c in chunks:
        p_m.append([_bf(jnp.exp(jnp.where(causal, g_row[c][hd], -jnp.inf) - mu[c][hd]) * s_m[c][hd])
                    for hd in heads])

    for c in chunks:
        rq.append(pr_ref[rows[c], C_RQ:C_RQ + R_QK_W])
        rk.append(pr_ref[rows[c], C_RK:C_RK + R_QK_W])
        s_r.append(_dot_nt(_bf(rq[c]), masked_rhs(_bf(rk[c]))))
        gq.append(pg_ref[rows[c], C_GQ:C_GQ + G_QK_W])
        gk.append(pg_ref[rows[c], C_GK:C_GK + G_QK_W])

    RB = 16
    n_rb = LC // RB

    def add_level(acc_c, x, li, blocks):
        for hd in range(G_HEADS):
            for n, b in enumerate(blocks):
                term = (lmb_ref[li, b * RB:(b + 1) * RB, :]
                        * _bf(x[n * RB:(n + 1) * RB, hd * LC:(hd + 1) * LC]))
                acc_c[hd][b] = term if acc_c[hd][b] is None else acc_c[hd][b] + term

    acc = []
    for c in chunks:
        acc.append([[None] * n_rb for _ in range(G_HEADS)])
        add_level(acc[c], _dot_nt(_bf(gq[c]), masked_rhs(_bf(gk[c]))), N_LEVELS, list(range(n_rb)))
    for li in range(N_LEVELS):
        seg = LC >> (li + 1)
        if seg >= RB:
            blocks = [b for b in range(n_rb) if ((b * RB) // seg) % 2 == 1]
        else:
            blocks = list(range(n_rb))
        for c in chunks:
            pw = jnp.exp(e_ref[c, li * LC:(li + 1) * LC, :])
            qs = _bf(gq[c] * pw)
            if len(blocks) < n_rb:
                qs = jnp.concatenate([qs[b * RB:(b + 1) * RB] for b in blocks], axis=0)
            add_level(acc[c], _dot_nt(qs, masked_rhs(_bf(gk[c] * pw))), li, blocks)
    acc = [[jnp.concatenate(acc[c][hd], axis=0) for hd in range(G_HEADS)] for c in chunks]

    vexts, u_m, intra_g, u_g, qin_g, dec_g, intra_r, u_r, qin_r = ([] for _ in range(9))
    for c in chunks:
        vext = [vext_ref[rows[c], 2 * hd * M_V:(2 * hd + 2) * M_V] for hd in heads]
        vexts.append(vext)
        u_m.append([])
        for tp in range(2):
            kw_t = _bf(m_k[c][tp].T * jnp.where(top_rows, w_row[c][2 * tp], w_row[c][2 * tp + 1]))
            for e in range(2):
                u_m[c].append(_dot(kw_t[e * M_QK:(e + 1) * M_QK], vext[2 * tp + e]))
        p_ret = [_bf(d_ref[hd] * s_r[c][:, hd * LC:(hd + 1) * LC]) for hd in range(R_HEADS)]
        rv = pr_ref[rows[c], C_RV:C_RV + R_V_W]
        intra_r.append(pair_tiles(p_ret, _bf(rv)))
        u_r.append(blk_ref[...] * _dot(_bf(rv.T), _bf(rk[c] * kdec_ref[...])))
        qin_r.append(_bf(rq[c] * qdec_ref[...]))
        pb = jnp.exp(e_ref[c, N_LEVELS * LC:(N_LEVELS + 1) * LC, :])
        pk = jnp.exp(e_ref[c, (N_LEVELS + 1) * LC:(N_LEVELS + 2) * LC, :])
        gv = pg_ref[rows[c], C_GV:C_GV + G_V_W]
        intra_g.append(pair_tiles(acc[c], _bf(gv)))
        u_g.append(blk_ref[...] * _dot(_bf(gv.T), _bf(gk[c] * pk)))
        qin_g.append(_bf(gq[c] * pb))
        dec_g.append(pb[LC - 1:LC, :])

    c_st = [mc_ref[hd] for hd in heads]
    g_st = gs_ref[...]
    r_st = rs_ref[...]
    zero_half = jnp.zeros((M_QK, 2 * M_V), bf16)
    ne_m, inter_g, inter_r = [], [], []
    for c in chunks:
        ne_m.append([])
        for hd in heads:
            cb = _bf(c_st[hd])
            padded = jnp.concatenate([cb, zero_half] if hd % 2 == 0 else [zero_half, cb], axis=0)
            q_in = _bf(qm[c][hd] * jnp.exp(m_prev[c][hd] - mu[c][hd]))
            ne_m[c].append(_dot(jnp.concatenate([p_m[c][hd], q_in], axis=1),
                                jnp.concatenate([vexts[c][hd], padded], axis=0)))
            c_st[hd] = s_old[c][hd] * c_st[hd] + u_m[c][hd]
        inter_g.append(_dot_nt(qin_g[c], _bf(g_st)))
        g_st = g_st * dec_g[c] + u_g[c]
        inter_r.append(_dot_nt(qin_r[c], _bf(r_st)))
        r_st = r_st * sdec_ref[...] + u_r[c]
    for hd in heads:
        mc_ref[hd] = c_st[hd]
    gs_ref[...] = g_st
    rs_ref[...] = r_st

    for c in chunks:
        for tp in range(2):
            hm = []
            for hd in (2 * tp, 2 * tp + 1):
                ne = ne_m[c][hd]
                num = ne[:, :M_V]
                den = ne[:, M_V:]
                hm.append(num / jnp.maximum(jnp.abs(den), jnp.exp(-(b_col[c][hd] + mu[c][hd]))))
            hm = jnp.concatenate(hm, axis=1)
            ms = _dot(_bf(hm * hm), onesm_ref[...])
            cols = slice(2 * tp * M_V, (2 * tp + 2) * M_V)
            o_gate = _sigmoid(pm_ref[rows[c], C_MO + 2 * tp * M_V:C_MO + (2 * tp + 2) * M_V])
            y_ref[rows[c], cols] = _bf(hm * lax.rsqrt(ms + EPS) * mnorm_ref[0, :, cols] * o_gate)
        y_ref[rows[c], M_V_W:M_V_W + G_V_W] = head_norm_gate(
            intra_g[c] + inter_g[c], gnorm_ref[0], pg_ref[rows[c], C_GG:C_GG + G_V_W])
        y_ref[rows[c], M_V_W + G_V_W:M_V_W + G_V_W + R_V_W] = head_norm_gate(
            intra_r[c] + inter_r[c], rnorm_ref[0], pr_ref[rows[c], C_RG:C_RG + R_V_W])

    mixed = _dot(y_ref[...], wout_ref[...])
    out_ref[0] = h_in + _rms(mixed, gpost_ref[0])


def _mixer_layer(layer, h, cosf, sins, params, consts):
    B, S, D = h.shape
    n_t = S // TS
    n_steps = B * n_t

    def tile_of(step):
        return step // n_t, step % n_t, 0

    tile = pl.BlockSpec((1, TS, D), tile_of)
    rope = pl.BlockSpec((1, TS, LANES), tile_of)

    def weight_spec(a):
        return pl.BlockSpec((None,) + a.shape[1:], lambda s: (layer, 0, 0))

    def small_spec(a):
        return pl.BlockSpec((1,) + a.shape[1:], lambda s: (layer, 0, 0))

    def const_spec(a):
        nd = a.ndim
        return pl.BlockSpec(a.shape, lambda s: (0,) * nd)

    gpre, gpost, win, wout, convw, ifb, wup, gab, mnorm, gnorm, rnorm = params
    in_specs = ([tile, rope, rope, small_spec(gpre), small_spec(gpost),
                 weight_spec(win), weight_spec(wout)]
                + [small_spec(a) for a in (convw, ifb, wup, gab, mnorm, gnorm, rnorm)]
                + [const_spec(a) for a in consts])
    return pl.pallas_call(
        functools.partial(_mixer_kernel, n_t),
        grid=(n_steps,),
        in_specs=in_specs,
        out_specs=tile,
        out_shape=jax.ShapeDtypeStruct(h.shape, h.dtype),
        scratch_shapes=[
            pltpu.VMEM((TS, W_G0 - W_M0), f32),
            pltpu.VMEM((TS, W_R0 - W_G0), f32),
            pltpu.VMEM((TS, P_COLS - W_R0), f32),
            pltpu.VMEM((TS + 8, 2 * M_QK_W), f32),
            pltpu.VMEM((TS, LANES), f32),
            pltpu.VMEM((TS // LC, (N_LEVELS + 2) * LC, LANES), f32),
            pltpu.VMEM((TS, 2 * M_V_W), bf16),
            pltpu.VMEM((TS, D_MODEL), bf16),
            pltpu.VMEM((M_HEADS, M_QK, 2 * M_V), f32),
            pltpu.VMEM((M_HEADS, 1, 1), f32),
            pltpu.VMEM((G_V_W, G_QK_W), f32),
            pltpu.VMEM((R_V_W, R_QK_W), f32),
        ],
        compiler_params=pltpu.CompilerParams(
            dimension_semantics=("arbitrary",), vmem_limit_bytes=VMEM_LIMIT),
        name="mixer_layer",
    )(h, cosf, sins, gpre, gpost, win, wout, convw, ifb, wup, gab, mnorm, gnorm, rnorm, *consts)


def _ffn_kernel(h_ref, gpre_ref, gpost_ref, w1_ref, w2_ref, out_ref):
    h_in = h_ref[...]
    u = _bf(_rms(h_in, gpre_ref[0]))
    acc = jnp.zeros((TM, D_MODEL), f32)
    for lo in range(0, D_FF, FF_BLK):
        hid = _dot(u, w1_ref[:, lo:lo + FF_BLK])
        hid = jnp.square(jnp.maximum(hid, 0.0))
        acc = acc + _dot(_bf(hid), w2_ref[lo:lo + FF_BLK, :])
    out_ref[...] = h_in + _rms(acc, gpost_ref[0])


def _ffn_layer(layer, h2, gpre, gpost, w1, w2):
    n_tok, D = h2.shape
    tile = pl.BlockSpec((TM, D), lambda i: (i, 0))
    return pl.pallas_call(
        _ffn_kernel,
        grid=(n_tok // TM,),
        in_specs=[tile,
                  pl.BlockSpec((1, 1, D), lambda i: (layer, 0, 0)),
                  pl.BlockSpec((1, 1, D), lambda i: (layer, 0, 0)),
                  pl.BlockSpec((None, D, D_FF), lambda i: (layer, 0, 0)),
                  pl.BlockSpec((None, D_FF, D), lambda i: (layer, 0, 0))],
        out_specs=tile,
        out_shape=jax.ShapeDtypeStruct(h2.shape, h2.dtype),
        compiler_params=pltpu.CompilerParams(
            dimension_semantics=("arbitrary",), vmem_limit_bytes=VMEM_LIMIT),
        name="ffn_layer",
    )(h2, gpre, gpost, w1, w2)


PACK_ROWS = 256


def _pack_kernel(w_ref, o_ref):
    def put(dst, first, last):
        src, n = IN_OFFS[first], IN_OFFS[last + 1] - IN_OFFS[first]
        o_ref[0, :, dst:dst + n] = w_ref[0, :, src:src + n].astype(bf16)

    put(C_MQK, 0, 2)
    put(C_GATE + GATE_I, 3, 4)
    put(C_GATE + GATE_A, 9, 9)
    used = GATE_A + G_RANK
    o_ref[0, :, C_GATE + used:C_GATE + LANES] = jnp.zeros((PACK_ROWS, LANES - used), bf16)
    put(C_MO, 5, 8)
    put(W_G0 + C_GG, 10, 14)


def _pack_w_in(w_in):
    depth, d, n_in = w_in.shape
    return pl.pallas_call(
        _pack_kernel,
        grid=(depth, d // PACK_ROWS),
        in_specs=[pl.BlockSpec((1, PACK_ROWS, n_in), lambda l, r: (l, r, 0))],
        out_specs=pl.BlockSpec((1, PACK_ROWS, P_COLS), lambda l, r: (l, r, 0)),
        out_shape=jax.ShapeDtypeStruct((depth, d, P_COLS), bf16),
        name="pack_w_in",
    )(w_in)


def kernel(x, positions, norm_pre_mix, norm_post_mix, norm_pre_ffn, norm_post_ffn, w_in, mlstm_conv_w, mlstm_i_bias, mlstm_f_bias, mlstm_norm, gla_w_up, gla_gate_bias, gla_norm, ret_norm, w_out, w_ff1, w_ff2):
    B, S, D = x.shape
    t_np, m_np = _chunk_constants()
    blk_np, seg_np, ones_np = _head_constants()
    consts = ((jnp.asarray(t_np, bf16), jnp.asarray(m_np, bf16))
              + tuple(jnp.asarray(a) for a in _ret_constants())
              + (jnp.asarray(blk_np), jnp.asarray(seg_np, bf16), jnp.asarray(ones_np, bf16)))

    cosf, sins = _rope_tables(positions)

    ifb = jnp.concatenate([mlstm_i_bias, mlstm_f_bias,
                           jnp.zeros((DEPTH, LANES - 2 * M_HEADS), f32)], axis=-1)
    wup = jnp.zeros((DEPTH, LANES, G_QK_W), f32).at[:, GATE_A:GATE_A + G_RANK, :].set(gla_w_up)
    row = lambda a: a[:, None, :]
    params = (row(norm_pre_mix), row(norm_post_mix), _pack_w_in(w_in), w_out.astype(bf16),
              mlstm_conv_w, row(ifb), wup.astype(bf16), row(gla_gate_bias),
              row(mlstm_norm), row(gla_norm), row(ret_norm))
    gpre_f, gpost_f = row(norm_pre_ffn), row(norm_post_ffn)
    w1_b = w_ff1.astype(bf16)
    w2_b = w_ff2.astype(bf16)

    h = x
    for l in range(DEPTH):
        h = _mixer_layer(l, h, cosf, sins, params, consts)
        h = _ffn_layer(l, h.reshape(B * S, D), gpre_f, gpost_f, w1_b, w2_b).reshape(B, S, D)
    return h
```

```python
import functools

import numpy as np
import jax
import jax.numpy as jnp
from jax import lax
from jax.experimental import pallas as pl
from jax.experimental.pallas import tpu as pltpu

D_MODEL = 1024
DEPTH = 4
EPS = 1e-6
M_HEADS, M_V, M_QK, M_CONV = 4, 128, 64, 4
G_HEADS, G_V, G_QK, G_RANK = 4, 64, 32, 16
G_NORMALIZER = 16.0
R_HEADS, R_V, R_QK = 4, 64, 32
ROPE_BASE = 10000.0
D_FF = 4 * D_MODEL

M_QK_W, M_V_W = M_HEADS * M_QK, M_HEADS * M_V
G_QK_W, G_V_W = G_HEADS * G_QK, G_HEADS * G_V
R_QK_W, R_V_W = R_HEADS * R_QK, R_HEADS * R_V
IN_SIZES = (M_QK_W, M_QK_W, M_V_W, M_HEADS, M_HEADS, M_V_W,
            G_QK_W, G_QK_W, G_V_W, G_RANK, G_V_W,
            R_QK_W, R_QK_W, R_V_W, R_V_W)
IN_OFFS = tuple(int(v) for v in np.cumsum((0,) + IN_SIZES))

LANES = 128
C_MQK = 0
C_MV = C_MQK + 2 * M_QK_W
C_GATE = C_MV + M_V_W
C_MO = C_GATE + LANES
C_GQ = 0
C_GK = C_GQ + G_QK_W
C_GV = C_GK + G_QK_W
C_GG = C_GV + G_V_W
C_RQ = 0
C_RK = C_RQ + R_QK_W
C_RV = C_RK + R_QK_W
C_RG = C_RV + R_V_W
W_M0 = 0
W_G0 = C_MO + M_V_W
W_R0 = W_G0 + C_GG + G_V_W
P_COLS = W_R0 + C_RG + R_V_W
GATE_I, GATE_F, GATE_A = 0, M_HEADS, 2 * M_HEADS

LC = 128
N_LEVELS = 7
TS = 512
TM = 512
FF_BLK = 1024
VMEM_LIMIT = 56 * 1024 * 1024

_NT = (((1,), (1,)), ((), ()))
f32 = jnp.float32
bf16 = jnp.bfloat16


def _bf(x):
    return x.astype(bf16)


def _dot(a, b):
    return jnp.dot(a, b, preferred_element_type=f32)


def _dot_nt(a, b):
    return lax.dot_general(a, b, _NT, preferred_element_type=f32)


def _rms(x, g):
    return x * lax.rsqrt(jnp.mean(x * x, axis=-1, keepdims=True) + EPS) * g


def _log_sigmoid(x):
    return jnp.minimum(x, 0.0) - jnp.log(1.0 + jnp.exp(-jnp.abs(x)))


def _sigmoid(x):
    return 0.5 * jnp.tanh(0.5 * x) + 0.5


def _silu(x):
    half = 0.5 * x
    return half + half * jnp.tanh(half)


def _split3(x):
    hi = _bf(x)
    r1 = x - hi.astype(f32)
    mid = _bf(r1)
    lo = _bf(r1 - mid.astype(f32))
    return jnp.concatenate([hi, mid, lo], axis=0)


def _split2(x):
    hi = _bf(x)
    return jnp.concatenate([hi, _bf(x - hi.astype(f32))], axis=0)


def _chunk_constants():
    L = LC
    T = np.zeros((N_LEVELS + 2, L, L), np.float32)
    M = np.zeros((N_LEVELS + 1, L, L), np.float32)
    idx = np.arange(L)
    for li in range(N_LEVELS):
        s = L >> (li + 1)
        for i in range(L):
            seg = i // s
            if seg % 2 == 1:
                T[li, i, seg * s:i + 1] = 1.0
            else:
                T[li, i, i + 1:seg * s + s] = 1.0
        same_blk = (idx[:, None] // (2 * s)) == (idx[None, :] // (2 * s))
        M[li] = same_blk & ((idx[:, None] // s) % 2 == 1) & ((idx[None, :] // s) % 2 == 0)
    M[N_LEVELS] = np.eye(L)
    for i in range(L):
        T[N_LEVELS, i, :i + 1] = 1.0
        T[N_LEVELS + 1, i, i + 1:] = 1.0
    T = T.reshape((N_LEVELS + 2) * L, L)
    return np.concatenate([T, T, T], axis=1), M


def _ret_constants():
    L = LC
    log_gamma = np.log1p(-np.exp2(-5.0 - np.arange(R_HEADS, dtype=np.float64)))
    i = np.arange(L)
    rel = (i[:, None] - i[None, :]).astype(np.float64)
    D = np.where(rel[None] >= 0, np.exp(log_gamma[:, None, None] * rel[None]), 0.0)
    lg_lane = log_gamma[np.repeat(np.arange(R_HEADS), R_QK)][None, :]
    qdec = np.exp(lg_lane * (i[:, None] + 1.0))
    kdec = np.exp(lg_lane * (L - 1.0 - i[:, None]))
    sdec = np.exp(lg_lane * L)
    return tuple(a.astype(np.float32) for a in (D, qdec, kdec, sdec))


def _head_constants():
    blk = (np.arange(G_V_W)[:, None] // G_V) == (np.arange(G_QK_W)[None, :] // G_QK)
    seg = ((np.arange(G_V_W)[:, None] // G_V) == (np.arange(G_V_W)[None, :] // G_V)) / float(G_V)
    ones_mean = np.kron(np.eye(2), np.full((M_V, M_V), 1.0 / M_V))
    return blk.astype(np.float32), seg.astype(np.float32), ones_mean.astype(np.float32)


def _rope_kernel(pos_ref, invf_ref, sgn_ref, cos_ref, sin_ref):
    ang = pos_ref[0].astype(f32) * invf_ref[...]
    cos_ref[0] = jnp.cos(ang)
    sin_ref[0] = jnp.sin(ang) * sgn_ref[...]


def _rope_tables(positions):
    B, S = positions.shape
    r = np.arange(LANES) % R_QK
    inv_freq = ROPE_BASE ** (-jnp.arange(0, R_QK, 2, dtype=f32) / R_QK)
    invf = jnp.tile(inv_freq, LANES // (R_QK // 2))[None, :]
    sgn = jnp.asarray(np.where(r < R_QK // 2, -1.0, 1.0).astype(np.float32)[None, :])
    return pl.pallas_call(
        _rope_kernel,
        grid=(B,),
        in_specs=[pl.BlockSpec((1, S, 1), lambda b: (b, 0, 0)),
                  pl.BlockSpec((1, LANES), lambda b: (0, 0)),
                  pl.BlockSpec((1, LANES), lambda b: (0, 0))],
        out_specs=[pl.BlockSpec((1, S, LANES), lambda b: (b, 0, 0)),
                   pl.BlockSpec((1, S, LANES), lambda b: (b, 0, 0))],
        out_shape=[jax.ShapeDtypeStruct((B, S, LANES), f32)] * 2,
        name="rope_tables",
    )(positions[:, :, None], invf, sgn)


def _mixer_kernel(n_t, hc_ref, cos_ref, sin_ref, gpre_ref, gpost_ref, win_ref, wout_ref,
                  convw_ref, ifb_ref, wup_ref, gab_ref, mnorm_ref, gnorm_ref, rnorm_ref,
                  t_ref, lmb_ref, d_ref, qdec_ref, kdec_ref, sdec_ref, blk_ref, seg_ref, onesm_ref,
                  out_ref,
                  pm_ref, pg_ref, pr_ref, qkpad_ref, la_ref, e_ref, vext_ref, y_ref,
                  mc_ref, mm_ref, gs_ref, rs_ref):
    step = pl.program_id(0)
    rm_ref, rg_ref, rr_ref = pm_ref, pg_ref, pr_ref

    def project(h_tile):
        u = _bf(_rms(h_tile, gpre_ref[0]))
        rm_ref[...] = _dot_nt(u, win_ref[W_M0:W_G0, :])
        rg_ref[...] = _dot_nt(u, win_ref[W_G0:W_R0, :])
        rr_ref[...] = _dot_nt(u, win_ref[W_R0:P_COLS, :])

    def preprocess(cosf, sins):
        qkpad_ref[8:8 + TS, :] = rm_ref[:, C_MQK:C_MQK + 2 * M_QK_W]
        conv = jnp.zeros((TS, 2 * M_QK_W), f32)
        for j in range(M_CONV):
            conv = conv + convw_ref[0, j:j + 1, :] * qkpad_ref[5 + j:5 + j + TS, :]
        qkpad_ref[0:8, :] = qkpad_ref[TS:TS + 8, :]
        qk = _silu(conv)
        lane_qk = lax.broadcasted_iota(jnp.int32, (1, 2 * M_QK_W), 1)
        pm_ref[:, C_MQK:C_MQK + 2 * M_QK_W] = qk * jnp.where(lane_qk < M_QK_W, 1.0, M_QK ** -0.5)
        for hd in range(M_HEADS):
            vext_ref[:, 2 * hd * M_V:(2 * hd + 1) * M_V] = _bf(
                rm_ref[:, C_MV + hd * M_V:C_MV + (hd + 1) * M_V])
        graw = rm_ref[:, C_GATE:C_GATE + LANES]
        gb = graw + ifb_ref[0]
        lane_g = lax.broadcasted_iota(jnp.int32, (TS, LANES), 1)
        pm_ref[:, C_GATE:C_GATE + LANES] = jnp.where(lane_g < GATE_F, gb, _log_sigmoid(gb))
        a_pre = _dot(_bf(graw), wup_ref[0]) + gab_ref[0]
        la_ref[...] = _log_sigmoid(a_pre) * (1.0 / G_NORMALIZER)
        pg_ref[:, C_GQ:C_GQ + G_QK_W] = rg_ref[:, C_GQ:C_GQ + G_QK_W] * (G_QK ** -0.5)
        first_half = (lane_g % R_QK) < (R_QK // 2)
        for col, scale in ((C_RQ, 1.0), (C_RK, R_QK ** -0.5)):
            xr = rr_ref[:, col:col + R_QK_W]
            swapped = jnp.where(first_half, pltpu.roll(xr, LANES - R_QK // 2, axis=1),
                                pltpu.roll(xr, R_QK // 2, axis=1))
            pr_ref[:, col:col + R_QK_W] = (xr * cosf + swapped * sins) * scale

    @pl.when(step == 0)
    def _():
        for hd in range(M_HEADS):
            vext_ref[:, (2 * hd + 1) * M_V:(2 * hd + 2) * M_V] = jnp.ones((TS, M_V), bf16)

    @pl.when(step % n_t == 0)
    def _():
        qkpad_ref[0:8, :] = jnp.zeros((8, 2 * M_QK_W), f32)
        mc_ref[...] = jnp.zeros_like(mc_ref)
        mm_ref[...] = jnp.zeros_like(mm_ref)
        gs_ref[...] = jnp.zeros_like(gs_ref)
        rs_ref[...] = jnp.zeros_like(rs_ref)

    h_in = hc_ref[0]
    project(h_in)
    preprocess(cos_ref[0], sin_ref[0])

    row_i = lax.broadcasted_iota(jnp.int32, (LC, LC), 0)
    col_j = lax.broadcasted_iota(jnp.int32, (LC, LC), 1)
    causal = row_i >= col_j
    top_rows = row_i < M_QK
    left_lanes = col_j < G_V
    lane1 = lax.broadcasted_iota(jnp.int32, (1, LANES), 1)
    pair_mask = [(lane1 < M_QK).astype(f32), (lane1 >= M_QK).astype(f32)]
    pair_mask_b = [m.astype(bf16) for m in pair_mask]
    head_mask_b = [((lane1 // G_QK) == hd).astype(bf16) for hd in range(G_HEADS)]

    def masked_rhs(kb):
        return jnp.concatenate([kb * m for m in head_mask_b], axis=0)

    def pair_tiles(p_heads, vb):
        tiles = []
        for tp in range(2):
            vt = vb[:, tp * LANES:(tp + 1) * LANES]
            tiles.append(_dot(jnp.concatenate([p_heads[2 * tp], p_heads[2 * tp + 1]], axis=1),
                              jnp.concatenate([vt * pair_mask_b[0], vt * pair_mask_b[1]], axis=0)))
        return jnp.concatenate(tiles, axis=1)

    def head_norm_gate(hcat, gain, gate):
        ms = _dot(_bf(hcat * hcat), seg_ref[...])
        return _bf(hcat * lax.rsqrt(ms + EPS) * gain * _silu(gate))

    n_ch = TS // LC
    chunks = range(n_ch)
    heads = range(M_HEADS)
    rows = [slice(c * LC, (c + 1) * LC) for c in chunks]
    t_cum = t_ref[N_LEVELS * LC:(N_LEVELS + 1) * LC, :]

    gblk, bcum, qm, s_m, s_r, m_k, gq, gk, rq, rk = ([] for _ in range(10))
    for c in range(0, n_ch, 2):
        gblk += [pm_ref[rows[c + i], C_GATE:C_GATE + LANES] for i in range(2)]
        cum2 = _dot(t_cum, jnp.concatenate([_split3(gblk[c]), _split3(gblk[c + 1])], axis=1))
        bcum += [cum2[:, :LANES], cum2[:, LANES:]]
        e2 = _dot(t_ref[:, :2 * LC], jnp.concatenate(
            [_split2(la_ref[rows[c], :]), _split2(la_ref[rows[c + 1], :])], axis=1))
        e_ref[c] = e2[:, :LANES]
        e_ref[c + 1] = e2[:, LANES:]
    for c in chunks:
        m_q = [pm_ref[rows[c], C_MQK + tp * LANES:C_MQK + (tp + 1) * LANES] for tp in range(2)]
        m_k.append([pm_ref[rows[c], C_MQK + M_QK_W + tp * LANES:C_MQK + M_QK_W + (tp + 1) * LANES]
                    for tp in range(2)])
        qm.append([m_q[hd // 2] * pair_mask[hd % 2] for hd in heads])
        s_m.append([])
        for tp in range(2):
            kb = _bf(m_k[c][tp])
            s2 = _dot_nt(_bf(m_q[tp]), jnp.concatenate([kb * pair_mask_b[0], kb * pair_mask_b[1]], axis=0))
            s_m[c] += [s2[:, :LC], s2[:, LC:]]

    g_row, b_col, rmax = [], [], []
    for c in chunks:
        xg = jnp.where(col_j < GATE_F, gblk[c], bcum[c])
        xgt = xg.T
        g_row.append([xgt[GATE_I + hd:GATE_I + hd + 1, :] - xgt[GATE_F + hd:GATE_F + hd + 1, :]
                      for hd in heads])
        b_col.append([xg[:, GATE_F + hd:GATE_F + hd + 1] for hd in heads])
        rmax.append([jnp.max(jnp.where(causal, g_row[c][hd], -jnp.inf), axis=1, keepdims=True)
                     for hd in heads])

    m_run = [mm_ref[hd] for hd in heads]
    m_prev, s_old, w_row = [], [], []
    for c in chunks:
        m_prev.append(list(m_run))
        s_old.append([])
        w_row.append([])
        for hd in heads:
            g_max = rmax[c][hd][LC - 1:LC, :]
            b_last = b_col[c][hd][LC - 1:LC, :]
            a_max = b_last + g_max
            m_new = jnp.maximum(b_last + m_run[hd], a_max)
            s_old[c].append(jnp.exp(b_last + m_run[hd] - m_new))
            w_row[c].append(jnp.exp(g_row[c][hd] - g_max) * jnp.exp(a_max - m_new))
            m_run[hd] = m_new
    for hd in heads:
        mm_ref[hd] = m_run[hd]

    mu = [[jnp.maximum(rmax[c][hd], m_prev[c][hd]) for hd in heads] for c in chunks]
    p_m = []
    for c in chunks:
        p_m.append([_bf(jnp.exp(jnp.where(causal, g_row[c][hd], -jnp.inf) - mu[c][hd]) * s_m[c][hd])
                    for hd in heads])

    for c in chunks:
        rq.append(pr_ref[rows[c], C_RQ:C_RQ + R_QK_W])
        rk.append(pr_ref[rows[c], C_RK:C_RK + R_QK_W])
        s_r.append(_dot_nt(_bf(rq[c]), masked_rhs(_bf(rk[c]))))
        gq.append(pg_ref[rows[c], C_GQ:C_GQ + G_QK_W])
        gk.append(pg_ref[rows[c], C_GK:C_GK + G_QK_W])

    RB = 16
    n_rb = LC // RB

    def add_level(acc_c, x, li, blocks):
        for hd in range(G_HEADS):
            for n, b in enumerate(blocks):
                term = (lmb_ref[li, b * RB:(b + 1) * RB, :]
                        * _bf(x[n * RB:(n + 1) * RB, hd * LC:(hd + 1) * LC]))
                acc_c[hd][b] = term if acc_c[hd][b] is None else acc_c[hd][b] + term

    acc = []
    for c in chunks:
        acc.append([[None] * n_rb for _ in range(G_HEADS)])
        add_level(acc[c], _dot_nt(_bf(gq[c]), masked_rhs(_bf(gk[c]))), N_LEVELS, list(range(n_rb)))
    for li in range(N_LEVELS):
        seg = LC >> (li + 1)
        if seg >= RB:
            blocks = [b for b in range(n_rb) if ((b * RB) // seg) % 2 == 1]
        else:
            blocks = list(range(n_rb))
        for c in chunks:
            pw = jnp.exp(e_ref[c, li * LC:(li + 1) * LC, :])
            qs = _bf(gq[c] * pw)
            if len(blocks) < n_rb:
                qs = jnp.concatenate([qs[b * RB:(b + 1) * RB] for b in blocks], axis=0)
            add_level(acc[c], _dot_nt(qs, masked_rhs(_bf(gk[c] * pw))), li, blocks)
    acc = [[jnp.concatenate(acc[c][hd], axis=0) for hd in range(G_HEADS)] for c in chunks]

    vexts, u_m, intra_g, u_g, qin_g, dec_g, intra_r, u_r, qin_r = ([] for _ in range(9))
    for c in chunks:
        vext = [vext_ref[rows[c], 2 * hd * M_V:(2 * hd + 2) * M_V] for hd in heads]
        vexts.append(vext)
        u_m.append([])
        for tp in range(2):
            kw_t = _bf(m_k[c][tp].T * jnp.where(top_rows, w_row[c][2 * tp], w_row[c][2 * tp + 1]))
            for e in range(2):
                u_m[c].append(_dot(kw_t[e * M_QK:(e + 1) * M_QK], vext[2 * tp + e]))
        p_ret = [_bf(d_ref[hd] * s_r[c][:, hd * LC:(hd + 1) * LC]) for hd in range(R_HEADS)]
        rv = pr_ref[rows[c], C_RV:C_RV + R_V_W]
        intra_r.append(pair_tiles(p_ret, _bf(rv)))
        u_r.append(blk_ref[...] * _dot(_bf(rv.T), _bf(rk[c] * kdec_ref[...])))
        qin_r.append(_bf(rq[c] * qdec_ref[...]))
        pb = jnp.exp(e_ref[c, N_LEVELS * LC:(N_LEVELS + 1) * LC, :])
        pk = jnp.exp(e_ref[c, (N_LEVELS + 1) * LC:(N_LEVELS + 2) * LC, :])
        gv = pg_ref[rows[c], C_GV:C_GV + G_V_W]
        intra_g.append(pair_tiles(acc[c], _bf(gv)))
        u_g.append(blk_ref[...] * _dot(_bf(gv.T), _bf(gk[c] * pk)))
        qin_g.append(_bf(gq[c] * pb))
        dec_g.append(pb[LC - 1:LC, :])

    c_st = [mc_ref[hd] for hd in heads]
    g_st = gs_ref[...]
    r_st = rs_ref[...]
    zero_half = jnp.zeros((M_QK, 2 * M_V), bf16)
    ne_m, inter_g, inter_r = [], [], []
    for c in chunks:
        ne_m.append([])
        for hd in heads:
            cb = _bf(c_st[hd])
            padded = jnp.concatenate([cb, zero_half] if hd % 2 == 0 else [zero_half, cb], axis=0)
            q_in = _bf(qm[c][hd] * jnp.exp(m_prev[c][hd] - mu[c][hd]))
            ne_m[c].append(_dot(jnp.concatenate([p_m[c][hd], q_in], axis=1),
                                jnp.concatenate([vexts[c][hd], padded], axis=0)))
            c_st[hd] = s_old[c][hd] * c_st[hd] + u_m[c][hd]
        inter_g.append(_dot_nt(qin_g[c], _bf(g_st)))
        g_st = g_st * dec_g[c] + u_g[c]
        inter_r.append(_dot_nt(qin_r[c], _bf(r_st)))
        r_st = r_st * sdec_ref[...] + u_r[c]
    for hd in heads:
        mc_ref[hd] = c_st[hd]
    gs_ref[...] = g_st
    rs_ref[...] = r_st

    for c in chunks:
        for tp in range(2):
            hm = []
            for hd in (2 * tp, 2 * tp + 1):
                ne = ne_m[c][hd]
                num = ne[:, :M_V]
                den = ne[:, M_V:]
                hm.append(num / jnp.maximum(jnp.abs(den), jnp.exp(-(b_col[c][hd] + mu[c][hd]))))
            hm = jnp.concatenate(hm, axis=1)
            ms = _dot(_bf(hm * hm), onesm_ref[...])
            cols = slice(2 * tp * M_V, (2 * tp + 2) * M_V)
            o_gate = _sigmoid(pm_ref[rows[c], C_MO + 2 * tp * M_V:C_MO + (2 * tp + 2) * M_V])
            y_ref[rows[c], cols] = _bf(hm * lax.rsqrt(ms + EPS) * mnorm_ref[0, :, cols] * o_gate)
        y_ref[rows[c], M_V_W:M_V_W + G_V_W] = head_norm_gate(
            intra_g[c] + inter_g[c], gnorm_ref[0], pg_ref[rows[c], C_GG:C_GG + G_V_W])
        y_ref[rows[c], M_V_W + G_V_W:M_V_W + G_V_W + R_V_W] = head_norm_gate(
            intra_r[c] + inter_r[c], rnorm_ref[0], pr_ref[rows[c], C_RG:C_RG + R_V_W])

    mixed = _dot(y_ref[...], wout_ref[...])
    out_ref[0] = h_in + _rms(mixed, gpost_ref[0])


def _mixer_layer(layer, h, cosf, sins, params, consts):
    B, S, D = h.shape
    n_t = S // TS
    n_steps = B * n_t

    def tile_of(step):
        return step // n_t, step % n_t, 0

    tile = pl.BlockSpec((1, TS, D), tile_of)
    rope = pl.BlockSpec((1, TS, LANES), tile_of)

    def weight_spec(a):
        return pl.BlockSpec((None,) + a.shape[1:], lambda s: (layer, 0, 0))

    def small_spec(a):
        return pl.BlockSpec((1,) + a.shape[1:], lambda s: (layer, 0, 0))

    def const_spec(a):
        nd = a.ndim
        return pl.BlockSpec(a.shape, lambda s: (0,) * nd)

    gpre, gpost, win, wout, convw, ifb, wup, gab, mnorm, gnorm, rnorm = params
    in_specs = ([tile, rope, rope, small_spec(gpre), small_spec(gpost),
                 weight_spec(win), weight_spec(wout)]
                + [small_spec(a) for a in (convw, ifb, wup, gab, mnorm, gnorm, rnorm)]
                + [const_spec(a) for a in consts])
    return pl.pallas_call(
        functools.partial(_mixer_kernel, n_t),
        grid=(n_steps,),
        in_specs=in_specs,
        out_specs=tile,
        out_shape=jax.ShapeDtypeStruct(h.shape, h.dtype),
        scratch_shapes=[
            pltpu.VMEM((TS, W_G0 - W_M0), f32),
            pltpu.VMEM((TS, W_R0 - W_G0), f32),
            pltpu.VMEM((TS, P_COLS - W_R0), f32),
            pltpu.VMEM((TS + 8, 2 * M_QK_W), f32),
            pltpu.VMEM((TS, LANES), f32),
            pltpu.VMEM((TS // LC, (N_LEVELS + 2) * LC, LANES), f32),
            pltpu.VMEM((TS, 2 * M_V_W), bf16),
            pltpu.VMEM((TS, D_MODEL), bf16),
            pltpu.VMEM((M_HEADS, M_QK, 2 * M_V), f32),
            pltpu.VMEM((M_HEADS, 1, 1), f32),
            pltpu.VMEM((G_V_W, G_QK_W), f32),
            pltpu.VMEM((R_V_W, R_QK_W), f32),
        ],
        compiler_params=pltpu.CompilerParams(
            dimension_semantics=("arbitrary",), vmem_limit_bytes=VMEM_LIMIT),
        name="mixer_layer",
    )(h, cosf, sins, gpre, gpost, win, wout, convw, ifb, wup, gab, mnorm, gnorm, rnorm, *consts)


def _ffn_kernel(h_ref, gpre_ref, gpost_ref, w1_ref, w2_ref, out_ref):
    h_in = h_ref[...]
    u = _bf(_rms(h_in, gpre_ref[0]))
    acc = jnp.zeros((TM, D_MODEL), f32)
    for lo in range(0, D_FF, FF_BLK):
        hid = _dot(u, w1_ref[:, lo:lo + FF_BLK])
        hid = jnp.square(jnp.maximum(hid, 0.0))
        acc = acc + _dot(_bf(hid), w2_ref[lo:lo + FF_BLK, :])
    out_ref[...] = h_in + _rms(acc, gpost_ref[0])


def _ffn_layer(layer, h2, gpre, gpost, w1, w2):
    n_tok, D = h2.shape
    tile = pl.BlockSpec((TM, D), lambda i: (i, 0))
    return pl.pallas_call(
        _ffn_kernel,
        grid=(n_tok // TM,),
        in_specs=[tile,
                  pl.BlockSpec((1, 1, D), lambda i: (layer, 0, 0)),
                  pl.BlockSpec((1, 1, D), lambda i: (layer, 0, 0)),
                  pl.BlockSpec((None, D, D_FF), lambda i: (layer, 0, 0)),
                  pl.BlockSpec((None, D_FF, D), lambda i: (layer, 0, 0))],
        out_specs=tile,
        out_shape=jax.ShapeDtypeStruct(h2.shape, h2.dtype),
        compiler_params=pltpu.CompilerParams(
            dimension_semantics=("arbitrary",), vmem_limit_bytes=VMEM_LIMIT),
        name="ffn_layer",
    )(h2, gpre, gpost, w1, w2)


PACK_COLS = 256


def _pack_kernel(w_ref, o_ref):
    def put(dst, first, last):
        src, n = IN_OFFS[first], IN_OFFS[last + 1] - IN_OFFS[first]
        o_ref[0, dst:dst + n, :] = w_ref[0, src:src + n, :].astype(bf16)

    put(C_MQK, 0, 2)
    gate = jnp.concatenate([w_ref[0, IN_OFFS[3]:IN_OFFS[5], :], w_ref[0, IN_OFFS[9]:IN_OFFS[10], :],
                            jnp.zeros((LANES - 2 * M_HEADS - G_RANK, PACK_COLS), f32)], axis=0)
    o_ref[0, C_GATE:C_GATE + LANES, :] = gate.astype(bf16)
    put(C_MO, 5, 8)
    put(W_G0 + C_GG, 10, 14)


def _pack_w_in(w_in):
    w_t = jnp.transpose(w_in, (0, 2, 1))
    depth, n_in, d = w_t.shape
    return pl.pallas_call(
        _pack_kernel,
        grid=(depth, d // PACK_COLS),
        in_specs=[pl.BlockSpec((1, n_in, PACK_COLS), lambda l, r: (l, 0, r))],
        out_specs=pl.BlockSpec((1, P_COLS, PACK_COLS), lambda l, r: (l, 0, r)),
        out_shape=jax.ShapeDtypeStruct((depth, P_COLS, d), bf16),
        name="pack_w_in",
    )(w_t)


def kernel(x, positions, norm_pre_mix, norm_post_mix, norm_pre_ffn, norm_post_ffn, w_in, mlstm_conv_w, mlstm_i_bias, mlstm_f_bias, mlstm_norm, gla_w_up, gla_gate_bias, gla_norm, ret_norm, w_out, w_ff1, w_ff2):
    B, S, D = x.shape
    t_np, m_np = _chunk_constants()
    blk_np, seg_np, ones_np = _head_constants()
    consts = ((jnp.asarray(t_np, bf16), jnp.asarray(m_np, bf16))
              + tuple(jnp.asarray(a) for a in _ret_constants())
              + (jnp.asarray(blk_np), jnp.asarray(seg_np, bf16), jnp.asarray(ones_np, bf16)))

    cosf, sins = _rope_tables(positions)

    ifb = jnp.concatenate([mlstm_i_bias, mlstm_f_bias,
                           jnp.zeros((DEPTH, LANES - 2 * M_HEADS), f32)], axis=-1)
    wup = jnp.zeros((DEPTH, LANES, G_QK_W), f32).at[:, GATE_A:GATE_A + G_RANK, :].set(gla_w_up)
    row = lambda a: a[:, None, :]
    params = (row(norm_pre_mix), row(norm_post_mix), _pack_w_in(w_in), w_out.astype(bf16),
              mlstm_conv_w, row(ifb), wup.astype(bf16), row(gla_gate_bias),
              row(mlstm_norm), row(gla_norm), row(ret_norm))
    gpre_f, gpost_f = row(norm_pre_ffn), row(norm_post_ffn)
    w1_b = w_ff1.astype(bf16)
    w2_b = w_ff2.astype(bf16)

    h = x
    for l in range(DEPTH):
        h = _mixer_layer(l, h, cosf, sins, params, consts)
        h = _ffn_layer(l, h.reshape(B * S, D), gpre_f, gpost_f, w1_b, w2_b).reshape(B, S, D)
    return h
```

```python
import functools

import numpy as np
import jax
import jax.numpy as jnp
from jax import lax
from jax.experimental import pallas as pl
from jax.experimental.pallas import tpu as pltpu

D_MODEL = 1024
DEPTH = 4
EPS = 1e-6
M_HEADS, M_V, M_QK, M_CONV = 4, 128, 64, 4
G_HEADS, G_V, G_QK, G_RANK = 4, 64, 32, 16
G_NORMALIZER = 16.0
R_HEADS, R_V, R_QK = 4, 64, 32
ROPE_BASE = 10000.0
D_FF = 4 * D_MODEL

M_QK_W, M_V_W = M_HEADS * M_QK, M_HEADS * M_V
G_QK_W, G_V_W = G_HEADS * G_QK, G_HEADS * G_V
R_QK_W, R_V_W = R_HEADS * R_QK, R_HEADS * R_V
IN_SIZES = (M_QK_W, M_QK_W, M_V_W, M_HEADS, M_HEADS, M_V_W,
            G_QK_W, G_QK_W, G_V_W, G_RANK, G_V_W,
            R_QK_W, R_QK_W, R_V_W, R_V_W)
IN_OFFS = tuple(int(v) for v in np.cumsum((0,) + IN_SIZES))

LANES = 128
C_MQK = 0
C_MV = C_MQK + 2 * M_QK_W
C_GATE = C_MV + M_V_W
C_MO = C_GATE + LANES
C_GQ = 0
C_GK = C_GQ + G_QK_W
C_GV = C_GK + G_QK_W
C_GG = C_GV + G_V_W
C_RQ = 0
C_RK = C_RQ + R_QK_W
C_RV = C_RK + R_QK_W
C_RG = C_RV + R_V_W
W_M0 = 0
W_G0 = C_MO + M_V_W
W_R0 = W_G0 + C_GG + G_V_W
P_COLS = W_R0 + C_RG + R_V_W
GATE_I, GATE_F, GATE_A = 0, M_HEADS, 2 * M_HEADS

LC = 128
N_LEVELS = 7
TS = 512
TM = 512
FF_BLK = 1024
VMEM_LIMIT = 56 * 1024 * 1024

LOG2E = 1.4426950408889634
_NT = (((1,), (1,)), ((), ()))
f32 = jnp.float32
bf16 = jnp.bfloat16


def _bf(x):
    return x.astype(bf16)


def _dot(a, b):
    return jnp.dot(a, b, preferred_element_type=f32)


def _dot_nt(a, b):
    return lax.dot_general(a, b, _NT, preferred_element_type=f32)


def _rms(x, g):
    return x * lax.rsqrt(jnp.mean(x * x, axis=-1, keepdims=True) + EPS) * g


def _log_sigmoid(x):
    return jnp.minimum(x, 0.0) - jnp.log(1.0 + jnp.exp(-jnp.abs(x)))


def _sigmoid(x):
    return 0.5 * jnp.tanh(0.5 * x) + 0.5


def _silu(x):
    half = 0.5 * x
    return half + half * jnp.tanh(half)


def _split3(x):
    hi = _bf(x)
    r1 = x - hi.astype(f32)
    mid = _bf(r1)
    lo = _bf(r1 - mid.astype(f32))
    return jnp.concatenate([hi, mid, lo], axis=0)


def _split2(x):
    hi = _bf(x)
    return jnp.concatenate([hi, _bf(x - hi.astype(f32))], axis=0)


def _chunk_constants():
    L = LC
    T = np.zeros((N_LEVELS + 2, L, L), np.float32)
    M = np.zeros((N_LEVELS + 1, L, L), np.float32)
    idx = np.arange(L)
    for li in range(N_LEVELS):
        s = L >> (li + 1)
        for i in range(L):
            seg = i // s
            if seg % 2 == 1:
                T[li, i, seg * s:i + 1] = 1.0
            else:
                T[li, i, i + 1:seg * s + s] = 1.0
        same_blk = (idx[:, None] // (2 * s)) == (idx[None, :] // (2 * s))
        M[li] = same_blk & ((idx[:, None] // s) % 2 == 1) & ((idx[None, :] // s) % 2 == 0)
    M[N_LEVELS] = np.eye(L)
    for i in range(L):
        T[N_LEVELS, i, :i + 1] = 1.0
        T[N_LEVELS + 1, i, i + 1:] = 1.0
    T = T.reshape((N_LEVELS + 2) * L, L)
    return np.concatenate([T, T, T], axis=1), M


def _ret_constants():
    L = LC
    log_gamma = np.log1p(-np.exp2(-5.0 - np.arange(R_HEADS, dtype=np.float64)))
    i = np.arange(L)
    rel = (i[:, None] - i[None, :]).astype(np.float64)
    D = np.where(rel[None] >= 0, np.exp(log_gamma[:, None, None] * rel[None]), 0.0)
    lg_lane = log_gamma[np.repeat(np.arange(R_HEADS), R_QK)][None, :]
    qdec = np.exp(lg_lane * (i[:, None] + 1.0))
    kdec = np.exp(lg_lane * (L - 1.0 - i[:, None]))
    sdec = np.exp(lg_lane * L)
    return tuple(a.astype(np.float32) for a in (D, qdec, kdec, sdec))


def _head_constants():
    blk = (np.arange(G_V_W)[:, None] // G_V) == (np.arange(G_QK_W)[None, :] // G_QK)
    seg = ((np.arange(G_V_W)[:, None] // G_V) == (np.arange(G_V_W)[None, :] // G_V)) / float(G_V)
    ones_mean = np.kron(np.eye(2), np.full((M_V, M_V), 1.0 / M_V))
    return blk.astype(np.float32), seg.astype(np.float32), ones_mean.astype(np.float32)


def _rope_kernel(pos_ref, invf_ref, sgn_ref, cos_ref, sin_ref):
    ang = pos_ref[0].astype(f32) * invf_ref[...]
    cos_ref[0] = jnp.cos(ang)
    sin_ref[0] = jnp.sin(ang) * sgn_ref[...]


def _rope_tables(positions):
    B, S = positions.shape
    r = np.arange(LANES) % R_QK
    inv_freq = ROPE_BASE ** (-jnp.arange(0, R_QK, 2, dtype=f32) / R_QK)
    invf = jnp.tile(inv_freq, LANES // (R_QK // 2))[None, :]
    sgn = jnp.asarray(np.where(r < R_QK // 2, -1.0, 1.0).astype(np.float32)[None, :])
    return pl.pallas_call(
        _rope_kernel,
        grid=(B,),
        in_specs=[pl.BlockSpec((1, S, 1), lambda b: (b, 0, 0)),
                  pl.BlockSpec((1, LANES), lambda b: (0, 0)),
                  pl.BlockSpec((1, LANES), lambda b: (0, 0))],
        out_specs=[pl.BlockSpec((1, S, LANES), lambda b: (b, 0, 0)),
                   pl.BlockSpec((1, S, LANES), lambda b: (b, 0, 0))],
        out_shape=[jax.ShapeDtypeStruct((B, S, LANES), f32)] * 2,
        name="rope_tables",
    )(positions[:, :, None], invf, sgn)


def _mixer_kernel(n_t, hc_ref, cos_ref, sin_ref, gpre_ref, gpost_ref, win_ref, wout_ref,
                  convw_ref, ifb_ref, wup_ref, gab_ref, mnorm_ref, gnorm_ref, rnorm_ref,
                  t_ref, lmb_ref, d_ref, qdec_ref, kdec_ref, sdec_ref, blk_ref, seg_ref, onesm_ref,
                  out_ref,
                  pm_ref, pg_ref, pr_ref, qkpad_ref, la_ref, e_ref, vext_ref, y_ref,
                  mc_ref, mm_ref, gs_ref, rs_ref):
    step = pl.program_id(0)
    rm_ref, rg_ref, rr_ref = pm_ref, pg_ref, pr_ref

    def preprocess_mlstm():
        qkpad_ref[8:8 + TS, :] = rm_ref[:, C_MQK:C_MQK + 2 * M_QK_W]
        conv = jnp.zeros((TS, 2 * M_QK_W), f32)
        for j in range(M_CONV):
            conv = conv + convw_ref[0, j:j + 1, :] * qkpad_ref[5 + j:5 + j + TS, :]
        qkpad_ref[0:8, :] = qkpad_ref[TS:TS + 8, :]
        qk = _silu(conv)
        lane_qk = lax.broadcasted_iota(jnp.int32, (1, 2 * M_QK_W), 1)
        pm_ref[:, C_MQK:C_MQK + 2 * M_QK_W] = qk * jnp.where(lane_qk < M_QK_W, 1.0, M_QK ** -0.5)
        for hd in range(M_HEADS):
            vext_ref[:, 2 * hd * M_V:(2 * hd + 1) * M_V] = _bf(
                rm_ref[:, C_MV + hd * M_V:C_MV + (hd + 1) * M_V])
        graw = rm_ref[:, C_GATE:C_GATE + LANES]
        gb = graw + ifb_ref[0]
        lane_g = lax.broadcasted_iota(jnp.int32, (TS, LANES), 1)
        pm_ref[:, C_GATE:C_GATE + LANES] = jnp.where(lane_g < GATE_F, gb, _log_sigmoid(gb)) * LOG2E
        a_pre = _dot(_bf(graw), wup_ref[0]) + gab_ref[0]
        la_ref[...] = _log_sigmoid(a_pre) * (LOG2E / G_NORMALIZER)

    def preprocess_gla_ret(cosf, sins):
        lane_g = lax.broadcasted_iota(jnp.int32, (TS, LANES), 1)
        pg_ref[:, C_GQ:C_GQ + G_QK_W] = rg_ref[:, C_GQ:C_GQ + G_QK_W] * (G_QK ** -0.5)
        first_half = (lane_g % R_QK) < (R_QK // 2)
        for col, scale in ((C_RQ, 1.0), (C_RK, R_QK ** -0.5)):
            xr = rr_ref[:, col:col + R_QK_W]
            swapped = jnp.where(first_half, pltpu.roll(xr, LANES - R_QK // 2, axis=1),
                                pltpu.roll(xr, R_QK // 2, axis=1))
            pr_ref[:, col:col + R_QK_W] = (xr * cosf + swapped * sins) * scale

    @pl.when(step == 0)
    def _():
        for hd in range(M_HEADS):
            vext_ref[:, (2 * hd + 1) * M_V:(2 * hd + 2) * M_V] = jnp.ones((TS, M_V), bf16)

    @pl.when(step % n_t == 0)
    def _():
        qkpad_ref[0:8, :] = jnp.zeros((8, 2 * M_QK_W), f32)
        mc_ref[...] = jnp.zeros_like(mc_ref)
        mm_ref[...] = jnp.zeros_like(mm_ref)
        gs_ref[...] = jnp.zeros_like(gs_ref)
        rs_ref[...] = jnp.zeros_like(rs_ref)

    h_in = hc_ref[0]
    u = _bf(_rms(h_in, gpre_ref[0]))
    rm_ref[...] = _dot_nt(u, win_ref[W_M0:W_G0, :])
    preprocess_mlstm()

    row_i = lax.broadcasted_iota(jnp.int32, (LC, LC), 0)
    col_j = lax.broadcasted_iota(jnp.int32, (LC, LC), 1)
    causal = row_i >= col_j
    top_rows = row_i < M_QK
    left_lanes = col_j < G_V
    lane1 = lax.broadcasted_iota(jnp.int32, (1, LANES), 1)
    pair_mask = [(lane1 < M_QK).astype(f32), (lane1 >= M_QK).astype(f32)]
    pair_mask_b = [m.astype(bf16) for m in pair_mask]
    head_mask_b = [((lane1 // G_QK) == hd).astype(bf16) for hd in range(G_HEADS)]

    def masked_rhs(kb):
        return jnp.concatenate([kb * m for m in head_mask_b], axis=0)

    def pair_tiles(p_heads, vb):
        tiles = []
        for tp in range(2):
            vt = vb[:, tp * LANES:(tp + 1) * LANES]
            tiles.append(_dot(jnp.concatenate([p_heads[2 * tp], p_heads[2 * tp + 1]], axis=1),
                              jnp.concatenate([vt * pair_mask_b[0], vt * pair_mask_b[1]], axis=0)))
        return jnp.concatenate(tiles, axis=1)

    def head_norm_gate(hcat, gain, gate):
        ms = _dot(_bf(hcat * hcat), seg_ref[...])
        return _bf(hcat * lax.rsqrt(ms + EPS) * gain * _silu(gate))

    n_ch = TS // LC
    chunks = range(n_ch)
    heads = range(M_HEADS)
    rows = [slice(c * LC, (c + 1) * LC) for c in chunks]
    t_cum = t_ref[N_LEVELS * LC:(N_LEVELS + 1) * LC, :]

    gblk, bcum, qm, s_m, s_r, m_k, gq, gk, rq, rk = ([] for _ in range(10))
    for c in range(0, n_ch, 2):
        gblk += [pm_ref[rows[c + i], C_GATE:C_GATE + LANES] for i in range(2)]
        cum2 = _dot(t_cum, jnp.concatenate([_split3(gblk[c]), _split3(gblk[c + 1])], axis=1))
        bcum += [cum2[:, :LANES], cum2[:, LANES:]]
        e2 = _dot(t_ref[:, :2 * LC], jnp.concatenate(
            [_split2(la_ref[rows[c], :]), _split2(la_ref[rows[c + 1], :])], axis=1))
        e_ref[c] = e2[:, :LANES]
        e_ref[c + 1] = e2[:, LANES:]
    for c in chunks:
        m_q = [pm_ref[rows[c], C_MQK + tp * LANES:C_MQK + (tp + 1) * LANES] for tp in range(2)]
        m_k.append([pm_ref[rows[c], C_MQK + M_QK_W + tp * LANES:C_MQK + M_QK_W + (tp + 1) * LANES]
                    for tp in range(2)])
        qm.append([m_q[hd // 2] * pair_mask[hd % 2] for hd in heads])
        s_m.append([])
        for tp in range(2):
            kb = _bf(m_k[c][tp])
            s2 = _dot_nt(_bf(m_q[tp]), jnp.concatenate([kb * pair_mask_b[0], kb * pair_mask_b[1]], axis=0))
            s_m[c] += [s2[:, :LC], s2[:, LC:]]

    rg_ref[...] = _dot_nt(u, win_ref[W_G0:W_R0, :])
    rr_ref[...] = _dot_nt(u, win_ref[W_R0:P_COLS, :])

    g_row, b_col, rmax = [], [], []
    for c in chunks:
        xg = jnp.where(col_j < GATE_F, gblk[c], bcum[c])
        xgt = xg.T
        g_row.append([xgt[GATE_I + hd:GATE_I + hd + 1, :] - xgt[GATE_F + hd:GATE_F + hd + 1, :]
                      for hd in heads])
        b_col.append([xg[:, GATE_F + hd:GATE_F + hd + 1] for hd in heads])
        rmax.append([jnp.max(jnp.where(causal, g_row[c][hd], -jnp.inf), axis=1, keepdims=True)
                     for hd in heads])

    m_run = [mm_ref[hd] for hd in heads]
    m_prev, s_old, w_row = [], [], []
    for c in chunks:
        m_prev.append(list(m_run))
        s_old.append([])
        w_row.append([])
        for hd in heads:
            g_max = rmax[c][hd][LC - 1:LC, :]
            b_last = b_col[c][hd][LC - 1:LC, :]
            a_max = b_last + g_max
            m_new = jnp.maximum(b_last + m_run[hd], a_max)
            s_old[c].append(jnp.exp2(b_last + m_run[hd] - m_new))
            w_row[c].append(jnp.exp2(g_row[c][hd] - g_max) * jnp.exp2(a_max - m_new))
            m_run[hd] = m_new
    for hd in heads:
        mm_ref[hd] = m_run[hd]

    mu = [[jnp.maximum(rmax[c][hd], m_prev[c][hd]) for hd in heads] for c in chunks]
    p_m = []
    for c in chunks:
        p_m.append([_bf(jnp.exp2(jnp.where(causal, g_row[c][hd], -jnp.inf) - mu[c][hd]) * s_m[c][hd])
                    for hd in heads])

    preprocess_gla_ret(cos_ref[0], sin_ref[0])
    for c in chunks:
        rq.append(pr_ref[rows[c], C_RQ:C_RQ + R_QK_W])
        rk.append(pr_ref[rows[c], C_RK:C_RK + R_QK_W])
        s_r.append(_dot_nt(_bf(rq[c]), masked_rhs(_bf(rk[c]))))
        gq.append(pg_ref[rows[c], C_GQ:C_GQ + G_QK_W])
        gk.append(pg_ref[rows[c], C_GK:C_GK + G_QK_W])

    RB = 16
    n_rb = LC // RB

    def add_level(acc_c, x, li, blocks):
        for hd in range(G_HEADS):
            for n, b in enumerate(blocks):
                term = (lmb_ref[li, b * RB:(b + 1) * RB, :]
                        * _bf(x[n * RB:(n + 1) * RB, hd * LC:(hd + 1) * LC]))
                acc_c[hd][b] = term if acc_c[hd][b] is None else acc_c[hd][b] + term

    acc = []
    for c in chunks:
        acc.append([[None] * n_rb for _ in range(G_HEADS)])
        add_level(acc[c], _dot_nt(_bf(gq[c]), masked_rhs(_bf(gk[c]))), N_LEVELS, list(range(n_rb)))
    for li in range(N_LEVELS):
        seg = LC >> (li + 1)
        if seg >= RB:
            blocks = [b for b in range(n_rb) if ((b * RB) // seg) % 2 == 1]
        else:
            blocks = list(range(n_rb))
        for c in chunks:
            pw = jnp.exp2(e_ref[c, li * LC:(li + 1) * LC, :])
            qs = _bf(gq[c] * pw)
            if len(blocks) < n_rb:
                qs = jnp.concatenate([qs[b * RB:(b + 1) * RB] for b in blocks], axis=0)
            add_level(acc[c], _dot_nt(qs, masked_rhs(_bf(gk[c] * pw))), li, blocks)
    acc = [[jnp.concatenate(acc[c][hd], axis=0) for hd in range(G_HEADS)] for c in chunks]

    vexts, u_m, intra_g, u_g, qin_g, dec_g, intra_r, u_r, qin_r = ([] for _ in range(9))
    for c in chunks:
        vext = [vext_ref[rows[c], 2 * hd * M_V:(2 * hd + 2) * M_V] for hd in heads]
        vexts.append(vext)
        u_m.append([])
        for tp in range(2):
            kw_t = _bf(m_k[c][tp].T * jnp.where(top_rows, w_row[c][2 * tp], w_row[c][2 * tp + 1]))
            for e in range(2):
                u_m[c].append(_dot(kw_t[e * M_QK:(e + 1) * M_QK], vext[2 * tp + e]))
        p_ret = [_bf(d_ref[hd] * s_r[c][:, hd * LC:(hd + 1) * LC]) for hd in range(R_HEADS)]
        rv = pr_ref[rows[c], C_RV:C_RV + R_V_W]
        intra_r.append(pair_tiles(p_ret, _bf(rv)))
        u_r.append(blk_ref[...] * _dot(_bf(rv.T), _bf(rk[c] * kdec_ref[...])))
        qin_r.append(_bf(rq[c] * qdec_ref[...]))
        pb = jnp.exp2(e_ref[c, N_LEVELS * LC:(N_LEVELS + 1) * LC, :])
        pk = jnp.exp2(e_ref[c, (N_LEVELS + 1) * LC:(N_LEVELS + 2) * LC, :])
        gv = pg_ref[rows[c], C_GV:C_GV + G_V_W]
        intra_g.append(pair_tiles(acc[c], _bf(gv)))
        u_g.append(blk_ref[...] * _dot(_bf(gv.T), _bf(gk[c] * pk)))
        qin_g.append(_bf(gq[c] * pb))
        dec_g.append(pb[LC - 1:LC, :])

    c_st = [mc_ref[hd] for hd in heads]
    g_st = gs_ref[...]
    r_st = rs_ref[...]
    zero_half = jnp.zeros((M_QK, 2 * M_V), bf16)
    ne_m, inter_g, inter_r = [], [], []
    for c in chunks:
        ne_m.append([])
        for hd in heads:
            cb = _bf(c_st[hd])
            padded = jnp.concatenate([cb, zero_half] if hd % 2 == 0 else [zero_half, cb], axis=0)
            q_in = _bf(qm[c][hd] * jnp.exp2(m_prev[c][hd] - mu[c][hd]))
            ne_m[c].append(_dot(jnp.concatenate([p_m[c][hd], q_in], axis=1),
                                jnp.concatenate([vexts[c][hd], padded], axis=0)))
            c_st[hd] = s_old[c][hd] * c_st[hd] + u_m[c][hd]
        inter_g.append(_dot_nt(qin_g[c], _bf(g_st)))
        g_st = g_st * dec_g[c] + u_g[c]
        inter_r.append(_dot_nt(qin_r[c], _bf(r_st)))
        r_st = r_st * sdec_ref[...] + u_r[c]
    for hd in heads:
        mc_ref[hd] = c_st[hd]
    gs_ref[...] = g_st
    rs_ref[...] = r_st

    for c in chunks:
        for tp in range(2):
            hm = []
            for hd in (2 * tp, 2 * tp + 1):
                ne = ne_m[c][hd]
                num = ne[:, :M_V]
                den = ne[:, M_V:]
                hm.append(num / jnp.maximum(jnp.abs(den), jnp.exp2(-(b_col[c][hd] + mu[c][hd]))))
            hm = jnp.concatenate(hm, axis=1)
            ms = _dot(_bf(hm * hm), onesm_ref[...])
            cols = slice(2 * tp * M_V, (2 * tp + 2) * M_V)
            o_gate = _sigmoid(pm_ref[rows[c], C_MO + 2 * tp * M_V:C_MO + (2 * tp + 2) * M_V])
            y_ref[rows[c], cols] = _bf(hm * lax.rsqrt(ms + EPS) * mnorm_ref[0, :, cols] * o_gate)
        y_ref[rows[c], M_V_W:M_V_W + G_V_W] = head_norm_gate(
            intra_g[c] + inter_g[c], gnorm_ref[0], pg_ref[rows[c], C_GG:C_GG + G_V_W])
        y_ref[rows[c], M_V_W + G_V_W:M_V_W + G_V_W + R_V_W] = head_norm_gate(
            intra_r[c] + inter_r[c], rnorm_ref[0], pr_ref[rows[c], C_RG:C_RG + R_V_W])

    mixed = _dot(y_ref[...], wout_ref[...])
    out_ref[0] = h_in + _rms(mixed, gpost_ref[0])


def _mixer_layer(layer, h, cosf, sins, params, consts):
    B, S, D = h.shape
    n_t = S // TS
    n_steps = B * n_t

    def tile_of(step):
        return step // n_t, step % n_t, 0

    tile = pl.BlockSpec((1, TS, D), tile_of)
    rope = pl.BlockSpec((1, TS, LANES), tile_of)

    def weight_spec(a):
        return pl.BlockSpec((None,) + a.shape[1:], lambda s: (layer, 0, 0))

    def small_spec(a):
        return pl.BlockSpec((1,) + a.shape[1:], lambda s: (layer, 0, 0))

    def const_spec(a):
        nd = a.ndim
        return pl.BlockSpec(a.shape, lambda s: (0,) * nd)

    gpre, gpost, win, wout, convw, ifb, wup, gab, mnorm, gnorm, rnorm = params
    in_specs = ([tile, rope, rope, small_spec(gpre), small_spec(gpost),
                 weight_spec(win), weight_spec(wout)]
                + [small_spec(a) for a in (convw, ifb, wup, gab, mnorm, gnorm, rnorm)]
                + [const_spec(a) for a in consts])
    return pl.pallas_call(
        functools.partial(_mixer_kernel, n_t),
        grid=(n_steps,),
        in_specs=in_specs,
        out_specs=tile,
        out_shape=jax.ShapeDtypeStruct(h.shape, h.dtype),
        scratch_shapes=[
            pltpu.VMEM((TS, W_G0 - W_M0), f32),
            pltpu.VMEM((TS, W_R0 - W_G0), f32),
            pltpu.VMEM((TS, P_COLS - W_R0), f32),
            pltpu.VMEM((TS + 8, 2 * M_QK_W), f32),
            pltpu.VMEM((TS, LANES), f32),
            pltpu.VMEM((TS // LC, (N_LEVELS + 2) * LC, LANES), f32),
            pltpu.VMEM((TS, 2 * M_V_W), bf16),
            pltpu.VMEM((TS, D_MODEL), bf16),
            pltpu.VMEM((M_HEADS, M_QK, 2 * M_V), f32),
            pltpu.VMEM((M_HEADS, 1, 1), f32),
            pltpu.VMEM((G_V_W, G_QK_W), f32),
            pltpu.VMEM((R_V_W, R_QK_W), f32),
        ],
        compiler_params=pltpu.CompilerParams(
            dimension_semantics=("arbitrary",), vmem_limit_bytes=VMEM_LIMIT),
        name="mixer_layer",
    )(h, cosf, sins, gpre, gpost, win, wout, convw, ifb, wup, gab, mnorm, gnorm, rnorm, *consts)


def _ffn_kernel(h_ref, gpre_ref, gpost_ref, w1_ref, w2_ref, out_ref):
    h_in = h_ref[...]
    u = _bf(_rms(h_in, gpre_ref[0]))
    acc = jnp.zeros((TM, D_MODEL), f32)
    for lo in range(0, D_FF, FF_BLK):
        hid = _dot(u, w1_ref[:, lo:lo + FF_BLK])
        hid = jnp.square(jnp.maximum(hid, 0.0))
        acc = acc + _dot(_bf(hid), w2_ref[lo:lo + FF_BLK, :])
    out_ref[...] = h_in + _rms(acc, gpost_ref[0])


def _ffn_layer(layer, h2, gpre, gpost, w1, w2):
    n_tok, D = h2.shape
    tile = pl.BlockSpec((TM, D), lambda i: (i, 0))
    return pl.pallas_call(
        _ffn_kernel,
        grid=(n_tok // TM,),
        in_specs=[tile,
                  pl.BlockSpec((1, 1, D), lambda i: (layer, 0, 0)),
                  pl.BlockSpec((1, 1, D), lambda i: (layer, 0, 0)),
                  pl.BlockSpec((None, D, D_FF), lambda i: (layer, 0, 0)),
                  pl.BlockSpec((None, D_FF, D), lambda i: (layer, 0, 0))],
        out_specs=tile,
        out_shape=jax.ShapeDtypeStruct(h2.shape, h2.dtype),
        compiler_params=pltpu.CompilerParams(
            dimension_semantics=("arbitrary",), vmem_limit_bytes=VMEM_LIMIT),
        name="ffn_layer",
    )(h2, gpre, gpost, w1, w2)


PACK_COLS = 256


def _pack_kernel(w_ref, o_ref):
    def put(dst, first, last):
        src, n = IN_OFFS[first], IN_OFFS[last + 1] - IN_OFFS[first]
        o_ref[0, dst:dst + n, :] = w_ref[0, src:src + n, :].astype(bf16)

    put(C_MQK, 0, 2)
    gate = jnp.concatenate([w_ref[0, IN_OFFS[3]:IN_OFFS[5], :], w_ref[0, IN_OFFS[9]:IN_OFFS[10], :],
                            jnp.zeros((LANES - 2 * M_HEADS - G_RANK, PACK_COLS), f32)], axis=0)
    o_ref[0, C_GATE:C_GATE + LANES, :] = gate.astype(bf16)
    put(C_MO, 5, 8)
    put(W_G0 + C_GG, 10, 14)


def _pack_w_in(w_in):
    w_t = jnp.transpose(w_in, (0, 2, 1))
    depth, n_in, d = w_t.shape
    return pl.pallas_call(
        _pack_kernel,
        grid=(depth, d // PACK_COLS),
        in_specs=[pl.BlockSpec((1, n_in, PACK_COLS), lambda l, r: (l, 0, r))],
        out_specs=pl.BlockSpec((1, P_COLS, PACK_COLS), lambda l, r: (l, 0, r)),
        out_shape=jax.ShapeDtypeStruct((depth, P_COLS, d), bf16),
        name="pack_w_in",
    )(w_t)


def kernel(x, positions, norm_pre_mix, norm_post_mix, norm_pre_ffn, norm_post_ffn, w_in, mlstm_conv_w, mlstm_i_bias, mlstm_f_bias, mlstm_norm, gla_w_up, gla_gate_bias, gla_norm, ret_norm, w_out, w_ff1, w_ff2):
    B, S, D = x.shape
    t_np, m_np = _chunk_constants()
    blk_np, seg_np, ones_np = _head_constants()
    consts = ((jnp.asarray(t_np, bf16), jnp.asarray(m_np, bf16))
              + tuple(jnp.asarray(a) for a in _ret_constants())
              + (jnp.asarray(blk_np), jnp.asarray(seg_np, bf16), jnp.asarray(ones_np, bf16)))

    cosf, sins = _rope_tables(positions)

    ifb = jnp.concatenate([mlstm_i_bias, mlstm_f_bias,
                           jnp.zeros((DEPTH, LANES - 2 * M_HEADS), f32)], axis=-1)
    wup = jnp.zeros((DEPTH, LANES, G_QK_W), f32).at[:, GATE_A:GATE_A + G_RANK, :].set(gla_w_up)
    row = lambda a: a[:, None, :]
    params = (row(norm_pre_mix), row(norm_post_mix), _pack_w_in(w_in), w_out.astype(bf16),
              mlstm_conv_w, row(ifb), wup.astype(bf16), row(gla_gate_bias),
              row(mlstm_norm), row(gla_norm), row(ret_norm))
    gpre_f, gpost_f = row(norm_pre_ffn), row(norm_post_ffn)
    w1_b = w_ff1.astype(bf16)
    w2_b = w_ff2.astype(bf16)

    h = x
    for l in range(DEPTH):
        h = _mixer_layer(l, h, cosf, sins, params, consts)
        h = _ffn_layer(l, h.reshape(B * S, D), gpre_f, gpost_f, w1_b, w2_b).reshape(B, S, D)
    return h
```

```python
import functools

import numpy as np
import jax
import jax.numpy as jnp
from jax import lax
from jax.experimental import pallas as pl
from jax.experimental.pallas import tpu as pltpu

D_MODEL = 1024
DEPTH = 4
EPS = 1e-6
M_HEADS, M_V, M_QK, M_CONV = 4, 128, 64, 4
G_HEADS, G_V, G_QK, G_RANK = 4, 64, 32, 16
G_NORMALIZER = 16.0
R_HEADS, R_V, R_QK = 4, 64, 32
ROPE_BASE = 10000.0
D_FF = 4 * D_MODEL

M_QK_W, M_V_W = M_HEADS * M_QK, M_HEADS * M_V
G_QK_W, G_V_W = G_HEADS * G_QK, G_HEADS * G_V
R_QK_W, R_V_W = R_HEADS * R_QK, R_HEADS * R_V
IN_SIZES = (M_QK_W, M_QK_W, M_V_W, M_HEADS, M_HEADS, M_V_W,
            G_QK_W, G_QK_W, G_V_W, G_RANK, G_V_W,
            R_QK_W, R_QK_W, R_V_W, R_V_W)
IN_OFFS = tuple(int(v) for v in np.cumsum((0,) + IN_SIZES))

LANES = 128
C_MQK = 0
C_MV = C_MQK + 2 * M_QK_W
C_GATE = C_MV + M_V_W
C_MO = C_GATE + LANES
C_GQ = 0
C_GK = C_GQ + G_QK_W
C_GV = C_GK + G_QK_W
C_GG = C_GV + G_V_W
C_RQ = 0
C_RK = C_RQ + R_QK_W
C_RV = C_RK + R_QK_W
C_RG = C_RV + R_V_W
W_M0 = 0
W_G0 = C_MO + M_V_W
W_R0 = W_G0 + C_GG + G_V_W
P_COLS = W_R0 + C_RG + R_V_W
GATE_I, GATE_F, GATE_A = 0, M_HEADS, 2 * M_HEADS

LC = 128
N_LEVELS = 7
TS = 512
TM = 512
FF_BLK = 1024
VMEM_LIMIT = 56 * 1024 * 1024

LOG2E = 1.4426950408889634
_NT = (((1,), (1,)), ((), ()))
f32 = jnp.float32
bf16 = jnp.bfloat16


def _bf(x):
    return x.astype(bf16)


def _dot(a, b):
    return jnp.dot(a, b, preferred_element_type=f32)


def _dot_nt(a, b):
    return lax.dot_general(a, b, _NT, preferred_element_type=f32)


def _rms(x, g):
    return x * lax.rsqrt(jnp.mean(x * x, axis=-1, keepdims=True) + EPS) * g


def _log_sigmoid(x):
    return jnp.minimum(x, 0.0) - jnp.log(1.0 + jnp.exp(-jnp.abs(x)))


def _sigmoid(x):
    return 0.5 * jnp.tanh(0.5 * x) + 0.5


def _silu(x):
    half = 0.5 * x
    return half + half * jnp.tanh(half)


def _split3(x):
    hi = _bf(x)
    r1 = x - hi.astype(f32)
    mid = _bf(r1)
    lo = _bf(r1 - mid.astype(f32))
    return jnp.concatenate([hi, mid, lo], axis=0)


def _split2(x):
    hi = _bf(x)
    return jnp.concatenate([hi, _bf(x - hi.astype(f32))], axis=0)


def _chunk_constants():
    L = LC
    T = np.zeros((N_LEVELS + 2, L, L), np.float32)
    M = np.zeros((N_LEVELS + 1, L, L), np.float32)
    idx = np.arange(L)
    for li in range(N_LEVELS):
        s = L >> (li + 1)
        for i in range(L):
            seg = i // s
            if seg % 2 == 1:
                T[li, i, seg * s:i + 1] = 1.0
            else:
                T[li, i, i + 1:seg * s + s] = 1.0
        same_blk = (idx[:, None] // (2 * s)) == (idx[None, :] // (2 * s))
        M[li] = same_blk & ((idx[:, None] // s) % 2 == 1) & ((idx[None, :] // s) % 2 == 0)
    M[N_LEVELS] = np.eye(L)
    for i in range(L):
        T[N_LEVELS, i, :i + 1] = 1.0
        T[N_LEVELS + 1, i, i + 1:] = 1.0
    T = T.reshape((N_LEVELS + 2) * L, L)
    return np.concatenate([T, T, T], axis=1), M


def _ret_constants():
    L = LC
    log_gamma = np.log1p(-np.exp2(-5.0 - np.arange(R_HEADS, dtype=np.float64)))
    i = np.arange(L)
    rel = (i[:, None] - i[None, :]).astype(np.float64)
    D = np.where(rel[None] >= 0, np.exp(log_gamma[:, None, None] * rel[None]), 0.0)
    lg_lane = log_gamma[np.repeat(np.arange(R_HEADS), R_QK)][None, :]
    qdec = np.exp(lg_lane * (i[:, None] + 1.0))
    kdec = np.exp(lg_lane * (L - 1.0 - i[:, None]))
    sdec = np.exp(lg_lane * L)
    return tuple(a.astype(np.float32) for a in (D, qdec, kdec, sdec))


def _head_constants():
    blk = (np.arange(G_V_W)[:, None] // G_V) == (np.arange(G_QK_W)[None, :] // G_QK)
    seg = ((np.arange(G_V_W)[:, None] // G_V) == (np.arange(G_V_W)[None, :] // G_V)) / float(G_V)
    return blk.astype(np.float32), seg.astype(np.float32)


def _rope_kernel(pos_ref, invf_ref, sgn_ref, cos_ref, sin_ref):
    ang = pos_ref[0].astype(f32) * invf_ref[...]
    cos_ref[0] = jnp.cos(ang)
    sin_ref[0] = jnp.sin(ang) * sgn_ref[...]


def _rope_tables(positions):
    B, S = positions.shape
    r = np.arange(LANES) % R_QK
    inv_freq = ROPE_BASE ** (-jnp.arange(0, R_QK, 2, dtype=f32) / R_QK)
    invf = jnp.tile(inv_freq, LANES // (R_QK // 2))[None, :]
    sgn = jnp.asarray(np.where(r < R_QK // 2, -1.0, 1.0).astype(np.float32)[None, :])
    return pl.pallas_call(
        _rope_kernel,
        grid=(B,),
        in_specs=[pl.BlockSpec((1, S, 1), lambda b: (b, 0, 0)),
                  pl.BlockSpec((1, LANES), lambda b: (0, 0)),
                  pl.BlockSpec((1, LANES), lambda b: (0, 0))],
        out_specs=[pl.BlockSpec((1, S, LANES), lambda b: (b, 0, 0)),
                   pl.BlockSpec((1, S, LANES), lambda b: (b, 0, 0))],
        out_shape=[jax.ShapeDtypeStruct((B, S, LANES), f32)] * 2,
        name="rope_tables",
    )(positions[:, :, None], invf, sgn)


def _mixer_kernel(n_t, hc_ref, cos_ref, sin_ref, gpre_ref, gpost_ref, win_ref, wout_ref,
                  convw_ref, ifb_ref, wup_ref, gab_ref, mnorm_ref, gnorm_ref, rnorm_ref,
                  t_ref, lmb_ref, d_ref, qdec_ref, kdec_ref, sdec_ref, blk_ref, seg_ref,
                  out_ref,
                  pm_ref, pg_ref, pr_ref, qkpad_ref, la_ref, e_ref, vext_ref, y_ref,
                  mc_ref, mm_ref, gs_ref, rs_ref):
    step = pl.program_id(0)
    rm_ref, rg_ref, rr_ref = pm_ref, pg_ref, pr_ref

    def preprocess_mlstm():
        qkpad_ref[8:8 + TS, :] = rm_ref[:, C_MQK:C_MQK + 2 * M_QK_W]
        conv = jnp.zeros((TS, 2 * M_QK_W), f32)
        for j in range(M_CONV):
            conv = conv + convw_ref[0, j:j + 1, :] * qkpad_ref[5 + j:5 + j + TS, :]
        qkpad_ref[0:8, :] = qkpad_ref[TS:TS + 8, :]
        qk = _silu(conv)
        lane_qk = lax.broadcasted_iota(jnp.int32, (1, 2 * M_QK_W), 1)
        pm_ref[:, C_MQK:C_MQK + 2 * M_QK_W] = qk * jnp.where(lane_qk < M_QK_W, 1.0, M_QK ** -0.5)
        for hd in range(M_HEADS):
            vext_ref[:, 2 * hd * M_V:(2 * hd + 1) * M_V] = _bf(
                rm_ref[:, C_MV + hd * M_V:C_MV + (hd + 1) * M_V])
        graw = rm_ref[:, C_GATE:C_GATE + LANES]
        gb = graw + ifb_ref[0]
        lane_g = lax.broadcasted_iota(jnp.int32, (TS, LANES), 1)
        pm_ref[:, C_GATE:C_GATE + LANES] = jnp.where(lane_g < GATE_F, gb, _log_sigmoid(gb)) * LOG2E
        a_pre = _dot(_bf(graw), wup_ref[0]) + gab_ref[0]
        la_ref[...] = _log_sigmoid(a_pre) * (LOG2E / G_NORMALIZER)

    def preprocess_gla_ret(cosf, sins):
        lane_g = lax.broadcasted_iota(jnp.int32, (TS, LANES), 1)
        pg_ref[:, C_GQ:C_GQ + G_QK_W] = rg_ref[:, C_GQ:C_GQ + G_QK_W] * (G_QK ** -0.5)
        first_half = (lane_g % R_QK) < (R_QK // 2)
        for col, scale in ((C_RQ, 1.0), (C_RK, R_QK ** -0.5)):
            xr = rr_ref[:, col:col + R_QK_W]
            swapped = jnp.where(first_half, pltpu.roll(xr, LANES - R_QK // 2, axis=1),
                                pltpu.roll(xr, R_QK // 2, axis=1))
            pr_ref[:, col:col + R_QK_W] = (xr * cosf + swapped * sins) * scale

    @pl.when(step == 0)
    def _():
        for hd in range(M_HEADS):
            vext_ref[:, (2 * hd + 1) * M_V:(2 * hd + 2) * M_V] = jnp.ones((TS, M_V), bf16)

    @pl.when(step % n_t == 0)
    def _():
        qkpad_ref[0:8, :] = jnp.zeros((8, 2 * M_QK_W), f32)
        mc_ref[...] = jnp.zeros_like(mc_ref)
        mm_ref[...] = jnp.zeros_like(mm_ref)
        gs_ref[...] = jnp.zeros_like(gs_ref)
        rs_ref[...] = jnp.zeros_like(rs_ref)

    h_in = hc_ref[0]
    u = _bf(_rms(h_in, gpre_ref[0]))
    rm_ref[...] = _dot_nt(u, win_ref[W_M0:W_G0, :])
    preprocess_mlstm()

    row_i = lax.broadcasted_iota(jnp.int32, (LC, LC), 0)
    col_j = lax.broadcasted_iota(jnp.int32, (LC, LC), 1)
    causal = row_i >= col_j
    top_rows = row_i < M_QK
    lane1 = lax.broadcasted_iota(jnp.int32, (1, LANES), 1)
    pair_mask = [(lane1 < M_QK).astype(f32), (lane1 >= M_QK).astype(f32)]
    pair_mask_b = [m.astype(bf16) for m in pair_mask]
    head_mask_b = [((lane1 // G_QK) == hd).astype(bf16) for hd in range(G_HEADS)]

    def masked_rhs(kb):
        return jnp.concatenate([kb * m for m in head_mask_b], axis=0)

    def pair_tiles(p_heads, vb):
        tiles = []
        for tp in range(2):
            vt = vb[:, tp * LANES:(tp + 1) * LANES]
            tiles.append(_dot(jnp.concatenate([p_heads[2 * tp], p_heads[2 * tp + 1]], axis=1),
                              jnp.concatenate([vt * pair_mask_b[0], vt * pair_mask_b[1]], axis=0)))
        return jnp.concatenate(tiles, axis=1)

    def head_norm_gate(hcat, gain, gate):
        ms = _dot(_bf(hcat * hcat), seg_ref[...])
        return _bf(hcat * lax.rsqrt(ms + EPS) * gain * _silu(gate))

    n_ch = TS // LC
    chunks = range(n_ch)
    heads = range(M_HEADS)
    rows = [slice(c * LC, (c + 1) * LC) for c in chunks]
    t_cum = t_ref[N_LEVELS * LC:(N_LEVELS + 1) * LC, :]

    gblk, bcum, qm, s_m, s_r, m_k, gq, gk, rq, rk = ([] for _ in range(10))
    for c in range(0, n_ch, 2):
        gblk += [pm_ref[rows[c + i], C_GATE:C_GATE + LANES] for i in range(2)]
        cum2 = _dot(t_cum, jnp.concatenate([_split3(gblk[c]), _split3(gblk[c + 1])], axis=1))
        bcum += [cum2[:, :LANES], cum2[:, LANES:]]
        e2 = _dot(t_ref[:, :2 * LC], jnp.concatenate(
            [_split2(la_ref[rows[c], :]), _split2(la_ref[rows[c + 1], :])], axis=1))
        e_ref[c] = e2[:, :LANES]
        e_ref[c + 1] = e2[:, LANES:]
    for c in chunks:
        m_q = [pm_ref[rows[c], C_MQK + tp * LANES:C_MQK + (tp + 1) * LANES] for tp in range(2)]
        m_k.append([pm_ref[rows[c], C_MQK + M_QK_W + tp * LANES:C_MQK + M_QK_W + (tp + 1) * LANES]
                    for tp in range(2)])
        qm.append([m_q[hd // 2] * pair_mask[hd % 2] for hd in heads])
        s_m.append([])
        for tp in range(2):
            kb = _bf(m_k[c][tp])
            s2 = _dot_nt(_bf(m_q[tp]), jnp.concatenate([kb * pair_mask_b[0], kb * pair_mask_b[1]], axis=0))
            s_m[c] += [s2[:, :LC], s2[:, LC:]]

    rg_ref[...] = _dot_nt(u, win_ref[W_G0:W_R0, :])
    rr_ref[...] = _dot_nt(u, win_ref[W_R0:P_COLS, :])

    g_row, b_col, rmax = [], [], []
    for c in chunks:
        xg = jnp.where(col_j < GATE_F, gblk[c], bcum[c])
        xgt = xg.T
        g_row.append([xgt[GATE_I + hd:GATE_I + hd + 1, :] - xgt[GATE_F + hd:GATE_F + hd + 1, :]
                      for hd in heads])
        b_col.append([xg[:, GATE_F + hd:GATE_F + hd + 1] for hd in heads])
        rmax.append([jnp.max(jnp.where(causal, g_row[c][hd], -jnp.inf), axis=1, keepdims=True)
                     for hd in heads])

    m_run = [mm_ref[hd] for hd in heads]
    m_prev, s_old, w_row = [], [], []
    for c in chunks:
        m_prev.append(list(m_run))
        s_old.append([])
        w_row.append([])
        for hd in heads:
            g_max = rmax[c][hd][LC - 1:LC, :]
            b_last = b_col[c][hd][LC - 1:LC, :]
            a_max = b_last + g_max
            m_new = jnp.maximum(b_last + m_run[hd], a_max)
            s_old[c].append(jnp.exp2(b_last + m_run[hd] - m_new))
            w_row[c].append(jnp.exp2(g_row[c][hd] - g_max) * jnp.exp2(a_max - m_new))
            m_run[hd] = m_new
    for hd in heads:
        mm_ref[hd] = m_run[hd]

    mu = [[jnp.maximum(rmax[c][hd], m_prev[c][hd]) for hd in heads] for c in chunks]
    p_m = []
    for c in chunks:
        p_m.append([_bf(jnp.exp2(jnp.where(causal, g_row[c][hd], -jnp.inf) - mu[c][hd]) * s_m[c][hd])
                    for hd in heads])

    vexts, u_m = [], []
    for c in chunks:
        vext = [vext_ref[rows[c], 2 * hd * M_V:(2 * hd + 2) * M_V] for hd in heads]
        vexts.append(vext)
        u_m.append([])
        for tp in range(2):
            kw_t = _bf(m_k[c][tp].T * jnp.where(top_rows, w_row[c][2 * tp], w_row[c][2 * tp + 1]))
            for e in range(2):
                u_m[c].append(_dot(kw_t[e * M_QK:(e + 1) * M_QK], vext[2 * tp + e]))
    c_st = [mc_ref[hd] for hd in heads]
    zero_half = jnp.zeros((M_QK, 2 * M_V), bf16)
    ne_m = []
    for c in chunks:
        ne_m.append([])
        for hd in heads:
            cb = _bf(c_st[hd])
            padded = jnp.concatenate([cb, zero_half] if hd % 2 == 0 else [zero_half, cb], axis=0)
            q_in = _bf(qm[c][hd] * jnp.exp2(m_prev[c][hd] - mu[c][hd]))
            ne_m[c].append(_dot(jnp.concatenate([p_m[c][hd], q_in], axis=1),
                                jnp.concatenate([vexts[c][hd], padded], axis=0)))
            c_st[hd] = s_old[c][hd] * c_st[hd] + u_m[c][hd]
    for hd in heads:
        mc_ref[hd] = c_st[hd]
    for c in chunks:
        for tp in range(2):
            hm = []
            for hd in (2 * tp, 2 * tp + 1):
                ne = ne_m[c][hd]
                num = ne[:, :M_V]
                den = ne[:, M_V:]
                h1 = num / jnp.maximum(jnp.abs(den), jnp.exp2(-(b_col[c][hd] + mu[c][hd])))
                hm.append(h1 * lax.rsqrt(jnp.mean(h1 * h1, axis=-1, keepdims=True) + EPS))
            hm = jnp.concatenate(hm, axis=1)
            cols = slice(2 * tp * M_V, (2 * tp + 2) * M_V)
            o_gate = _sigmoid(pm_ref[rows[c], C_MO + 2 * tp * M_V:C_MO + (2 * tp + 2) * M_V])
            y_ref[rows[c], cols] = _bf(hm * mnorm_ref[0, :, cols] * o_gate)

    preprocess_gla_ret(cos_ref[0], sin_ref[0])
    for c in chunks:
        rq.append(pr_ref[rows[c], C_RQ:C_RQ + R_QK_W])
        rk.append(pr_ref[rows[c], C_RK:C_RK + R_QK_W])
        s_r.append(_dot_nt(_bf(rq[c]), masked_rhs(_bf(rk[c]))))
        gq.append(pg_ref[rows[c], C_GQ:C_GQ + G_QK_W])
        gk.append(pg_ref[rows[c], C_GK:C_GK + G_QK_W])

    RB = 16
    n_rb = LC // RB

    def add_level(acc_c, x, li, blocks):
        for hd in range(G_HEADS):
            for n, b in enumerate(blocks):
                term = (lmb_ref[li, b * RB:(b + 1) * RB, :]
                        * _bf(x[n * RB:(n + 1) * RB, hd * LC:(hd + 1) * LC]))
                acc_c[hd][b] = term if acc_c[hd][b] is None else acc_c[hd][b] + term

    acc = []
    for c in chunks:
        acc.append([[None] * n_rb for _ in range(G_HEADS)])
        add_level(acc[c], _dot_nt(_bf(gq[c]), masked_rhs(_bf(gk[c]))), N_LEVELS, list(range(n_rb)))
    for li in range(N_LEVELS):
        seg = LC >> (li + 1)
        if seg >= RB:
            blocks = [b for b in range(n_rb) if ((b * RB) // seg) % 2 == 1]
        else:
            blocks = list(range(n_rb))
        for c in chunks:
            pw = jnp.exp2(e_ref[c, li * LC:(li + 1) * LC, :])
            qs = _bf(gq[c] * pw)
            if len(blocks) < n_rb:
                qs = jnp.concatenate([qs[b * RB:(b + 1) * RB] for b in blocks], axis=0)
            add_level(acc[c], _dot_nt(qs, masked_rhs(_bf(gk[c] * pw))), li, blocks)
    acc = [[jnp.concatenate(acc[c][hd], axis=0) for hd in range(G_HEADS)] for c in chunks]

    intra_g, u_g, qin_g, dec_g, intra_r, u_r, qin_r = ([] for _ in range(7))
    for c in chunks:
        p_ret = [_bf(d_ref[hd] * s_r[c][:, hd * LC:(hd + 1) * LC]) for hd in range(R_HEADS)]
        rv = pr_ref[rows[c], C_RV:C_RV + R_V_W]
        intra_r.append(pair_tiles(p_ret, _bf(rv)))
        u_r.append(blk_ref[...] * _dot(_bf(rv.T), _bf(rk[c] * kdec_ref[...])))
        qin_r.append(_bf(rq[c] * qdec_ref[...]))
        pb = jnp.exp2(e_ref[c, N_LEVELS * LC:(N_LEVELS + 1) * LC, :])
        pk = jnp.exp2(e_ref[c, (N_LEVELS + 1) * LC:(N_LEVELS + 2) * LC, :])
        gv = pg_ref[rows[c], C_GV:C_GV + G_V_W]
        intra_g.append(pair_tiles(acc[c], _bf(gv)))
        u_g.append(blk_ref[...] * _dot(_bf(gv.T), _bf(gk[c] * pk)))
        qin_g.append(_bf(gq[c] * pb))
        dec_g.append(pb[LC - 1:LC, :])

    g_st = gs_ref[...]
    r_st = rs_ref[...]
    inter_g, inter_r = [], []
    for c in chunks:
        inter_g.append(_dot_nt(qin_g[c], _bf(g_st)))
        g_st = g_st * dec_g[c] + u_g[c]
        inter_r.append(_dot_nt(qin_r[c], _bf(r_st)))
        r_st = r_st * sdec_ref[...] + u_r[c]
    gs_ref[...] = g_st
    rs_ref[...] = r_st

    for c in chunks:
        y_ref[rows[c], M_V_W:M_V_W + G_V_W] = head_norm_gate(
            intra_g[c] + inter_g[c], gnorm_ref[0], pg_ref[rows[c], C_GG:C_GG + G_V_W])
        y_ref[rows[c], M_V_W + G_V_W:M_V_W + G_V_W + R_V_W] = head_norm_gate(
            intra_r[c] + inter_r[c], rnorm_ref[0], pr_ref[rows[c], C_RG:C_RG + R_V_W])

    mixed = _dot(y_ref[...], wout_ref[...])
    out_ref[0] = h_in + _rms(mixed, gpost_ref[0])


def _mixer_layer(layer, h, cosf, sins, params, consts):
    B, S, D = h.shape
    n_t = S // TS
    n_steps = B * n_t

    def tile_of(step):
        return step // n_t, step % n_t, 0

    tile = pl.BlockSpec((1, TS, D), tile_of)
    rope = pl.BlockSpec((1, TS, LANES), tile_of)

    def weight_spec(a):
        return pl.BlockSpec((None,) + a.shape[1:], lambda s: (layer, 0, 0))

    def small_spec(a):
        return pl.BlockSpec((1,) + a.shape[1:], lambda s: (layer, 0, 0))

    def const_spec(a):
        nd = a.ndim
        return pl.BlockSpec(a.shape, lambda s: (0,) * nd)

    gpre, gpost, win, wout, convw, ifb, wup, gab, mnorm, gnorm, rnorm = params
    in_specs = ([tile, rope, rope, small_spec(gpre), small_spec(gpost),
                 weight_spec(win), weight_spec(wout)]
                + [small_spec(a) for a in (convw, ifb, wup, gab, mnorm, gnorm, rnorm)]
                + [const_spec(a) for a in consts])
    return pl.pallas_call(
        functools.partial(_mixer_kernel, n_t),
        grid=(n_steps,),
        in_specs=in_specs,
        out_specs=tile,
        out_shape=jax.ShapeDtypeStruct(h.shape, h.dtype),
        scratch_shapes=[
            pltpu.VMEM((TS, W_G0 - W_M0), f32),
            pltpu.VMEM((TS, W_R0 - W_G0), f32),
            pltpu.VMEM((TS, P_COLS - W_R0), f32),
            pltpu.VMEM((TS + 8, 2 * M_QK_W), f32),
            pltpu.VMEM((TS, LANES), f32),
            pltpu.VMEM((TS // LC, (N_LEVELS + 2) * LC, LANES), f32),
            pltpu.VMEM((TS, 2 * M_V_W), bf16),
            pltpu.VMEM((TS, D_MODEL), bf16),
            pltpu.VMEM((M_HEADS, M_QK, 2 * M_V), f32),
            pltpu.VMEM((M_HEADS, 1, 1), f32),
            pltpu.VMEM((G_V_W, G_QK_W), f32),
            pltpu.VMEM((R_V_W, R_QK_W), f32),
        ],
        compiler_params=pltpu.CompilerParams(
            dimension_semantics=("arbitrary",), vmem_limit_bytes=VMEM_LIMIT),
        name="mixer_layer",
    )(h, cosf, sins, gpre, gpost, win, wout, convw, ifb, wup, gab, mnorm, gnorm, rnorm, *consts)


def _ffn_kernel(h_ref, gpre_ref, gpost_ref, w1_ref, w2_ref, out_ref):
    h_in = h_ref[...]
    u = _bf(_rms(h_in, gpre_ref[0]))
    acc = jnp.zeros((TM, D_MODEL), f32)
    for lo in range(0, D_FF, FF_BLK):
        hid = _dot(u, w1_ref[:, lo:lo + FF_BLK])
        hid = jnp.square(jnp.maximum(hid, 0.0))
        acc = acc + _dot(_bf(hid), w2_ref[lo:lo + FF_BLK, :])
    out_ref[...] = h_in + _rms(acc, gpost_ref[0])


def _ffn_layer(layer, h2, gpre, gpost, w1, w2):
    n_tok, D = h2.shape
    tile = pl.BlockSpec((TM, D), lambda i: (i, 0))
    return pl.pallas_call(
        _ffn_kernel,
        grid=(n_tok // TM,),
        in_specs=[tile,
                  pl.BlockSpec((1, 1, D), lambda i: (layer, 0, 0)),
                  pl.BlockSpec((1, 1, D), lambda i: (layer, 0, 0)),
                  pl.BlockSpec((None, D, D_FF), lambda i: (layer, 0, 0)),
                  pl.BlockSpec((None, D_FF, D), lambda i: (layer, 0, 0))],
        out_specs=tile,
        out_shape=jax.ShapeDtypeStruct(h2.shape, h2.dtype),
        compiler_params=pltpu.CompilerParams(
            dimension_semantics=("arbitrary",), vmem_limit_bytes=VMEM_LIMIT),
        name="ffn_layer",
    )(h2, gpre, gpost, w1, w2)


PACK_COLS = 256


def _pack_kernel(w_ref, o_ref):
    def put(dst, first, last):
        src, n = IN_OFFS[first], IN_OFFS[last + 1] - IN_OFFS[first]
        o_ref[0, dst:dst + n, :] = w_ref[0, src:src + n, :].astype(bf16)

    put(C_MQK, 0, 2)
    gate = jnp.concatenate([w_ref[0, IN_OFFS[3]:IN_OFFS[5], :], w_ref[0, IN_OFFS[9]:IN_OFFS[10], :],
                            jnp.zeros((LANES - 2 * M_HEADS - G_RANK, PACK_COLS), f32)], axis=0)
    o_ref[0, C_GATE:C_GATE + LANES, :] = gate.astype(bf16)
    put(C_MO, 5, 8)
    put(W_G0 + C_GG, 10, 14)


def _pack_w_in(w_in):
    w_t = jnp.transpose(w_in, (0, 2, 1))
    depth, n_in, d = w_t.shape
    return pl.pallas_call(
        _pack_kernel,
        grid=(depth, d // PACK_COLS),
        in_specs=[pl.BlockSpec((1, n_in, PACK_COLS), lambda l, r: (l, 0, r))],
        out_specs=pl.BlockSpec((1, P_COLS, PACK_COLS), lambda l, r: (l, 0, r)),
        out_shape=jax.ShapeDtypeStruct((depth, P_COLS, d), bf16),
        name="pack_w_in",
    )(w_t)


def kernel(x, positions, norm_pre_mix, norm_post_mix, norm_pre_ffn, norm_post_ffn, w_in, mlstm_conv_w, mlstm_i_bias, mlstm_f_bias, mlstm_norm, gla_w_up, gla_gate_bias, gla_norm, ret_norm, w_out, w_ff1, w_ff2):
    B, S, D = x.shape
    t_np, m_np = _chunk_constants()
    blk_np, seg_np = _head_constants()
    consts = ((jnp.asarray(t_np, bf16), jnp.asarray(m_np, bf16))
              + tuple(jnp.asarray(a) for a in _ret_constants())
              + (jnp.asarray(blk_np), jnp.asarray(seg_np, bf16)))

    cosf, sins = _rope_tables(positions)

    ifb = jnp.concatenate([mlstm_i_bias, mlstm_f_bias,
                           jnp.zeros((DEPTH, LANES - 2 * M_HEADS), f32)], axis=-1)
    wup = jnp.zeros((DEPTH, LANES, G_QK_W), f32).at[:, GATE_A:GATE_A + G_RANK, :].set(gla_w_up)
    row = lambda a: a[:, None, :]
    params = (row(norm_pre_mix), row(norm_post_mix), _pack_w_in(w_in), w_out.astype(bf16),
              mlstm_conv_w, row(ifb), wup.astype(bf16), row(gla_gate_bias),
              row(mlstm_norm), row(gla_norm), row(ret_norm))
    gpre_f, gpost_f = row(norm_pre_ffn), row(norm_post_ffn)
    w1_b = w_ff1.astype(bf16)
    w2_b = w_ff2.astype(bf16)

    h = x
    for l in range(DEPTH):
        h = _mixer_layer(l, h, cosf, sins, params, consts)
        h = _ffn_layer(l, h.reshape(B * S, D), gpre_f, gpost_f, w1_b, w2_b).reshape(B, S, D)
    return h
```

```python
import functools

import numpy as np
import jax
import jax.numpy as jnp
from jax import lax
from jax.experimental import pallas as pl
from jax.experimental.pallas import tpu as pltpu

D_MODEL = 1024
DEPTH = 4
EPS = 1e-6
M_HEADS, M_V, M_QK, M_CONV = 4, 128, 64, 4
G_HEADS, G_V, G_QK, G_RANK = 4, 64, 32, 16
G_NORMALIZER = 16.0
R_HEADS, R_V, R_QK = 4, 64, 32
ROPE_BASE = 10000.0
D_FF = 4 * D_MODEL

M_QK_W, M_V_W = M_HEADS * M_QK, M_HEADS * M_V
G_QK_W, G_V_W = G_HEADS * G_QK, G_HEADS * G_V
R_QK_W, R_V_W = R_HEADS * R_QK, R_HEADS * R_V
IN_SIZES = (M_QK_W, M_QK_W, M_V_W, M_HEADS, M_HEADS, M_V_W,
            G_QK_W, G_QK_W, G_V_W, G_RANK, G_V_W,
            R_QK_W, R_QK_W, R_V_W, R_V_W)
IN_OFFS = tuple(int(v) for v in np.cumsum((0,) + IN_SIZES))

LANES = 128
C_MQK = 0
C_MV = C_MQK + 2 * M_QK_W
C_GATE = C_MV + M_V_W
C_MO = C_GATE + LANES
C_GQ = 0
C_GK = C_GQ + G_QK_W
C_GV = C_GK + G_QK_W
C_GG = C_GV + G_V_W
C_RQ = 0
C_RK = C_RQ + R_QK_W
C_RV = C_RK + R_QK_W
C_RG = C_RV + R_V_W
W_M0 = 0
W_G0 = C_MO + M_V_W
W_R0 = W_G0 + C_GG + G_V_W
P_COLS = W_R0 + C_RG + R_V_W
GATE_I, GATE_F, GATE_A = 0, M_HEADS, 2 * M_HEADS

LC = 128
N_LEVELS = 7
TS = 512
TM = 512
FF_BLK = 1024
VMEM_LIMIT = 56 * 1024 * 1024

LOG2E = 1.4426950408889634
_NT = (((1,), (1,)), ((), ()))
f32 = jnp.float32
bf16 = jnp.bfloat16


def _bf(x):
    return x.astype(bf16)


def _dot(a, b):
    return jnp.dot(a, b, preferred_element_type=f32)


def _dot_nt(a, b):
    return lax.dot_general(a, b, _NT, preferred_element_type=f32)


def _rms(x, g):
    return x * lax.rsqrt(jnp.mean(x * x, axis=-1, keepdims=True) + EPS) * g


def _log_sigmoid(x):
    return jnp.minimum(x, 0.0) - jnp.log(1.0 + jnp.exp(-jnp.abs(x)))


def _sigmoid(x):
    return 0.5 * jnp.tanh(0.5 * x) + 0.5


def _silu(x):
    half = 0.5 * x
    return half + half * jnp.tanh(half)


def _split3(x):
    hi = _bf(x)
    r1 = x - hi.astype(f32)
    mid = _bf(r1)
    lo = _bf(r1 - mid.astype(f32))
    return jnp.concatenate([hi, mid, lo], axis=0)


def _split2(x):
    hi = _bf(x)
    return jnp.concatenate([hi, _bf(x - hi.astype(f32))], axis=0)


def _chunk_constants():
    L = LC
    T = np.zeros((N_LEVELS + 2, L, L), np.float32)
    M = np.zeros((N_LEVELS + 1, L, L), np.float32)
    idx = np.arange(L)
    for li in range(N_LEVELS):
        s = L >> (li + 1)
        for i in range(L):
            seg = i // s
            if seg % 2 == 1:
                T[li, i, seg * s:i + 1] = 1.0
            else:
                T[li, i, i + 1:seg * s + s] = 1.0
        same_blk = (idx[:, None] // (2 * s)) == (idx[None, :] // (2 * s))
        M[li] = same_blk & ((idx[:, None] // s) % 2 == 1) & ((idx[None, :] // s) % 2 == 0)
    M[N_LEVELS] = np.eye(L)
    for i in range(L):
        T[N_LEVELS, i, :i + 1] = 1.0
        T[N_LEVELS + 1, i, i + 1:] = 1.0
    T = T.reshape((N_LEVELS + 2) * L, L)
    return np.concatenate([T, T, T], axis=1), M


def _ret_constants():
    L = LC
    log_gamma = np.log1p(-np.exp2(-5.0 - np.arange(R_HEADS, dtype=np.float64)))
    i = np.arange(L)
    rel = (i[:, None] - i[None, :]).astype(np.float64)
    D = np.where(rel[None] >= 0, np.exp(log_gamma[:, None, None] * rel[None]), 0.0)
    lg_lane = log_gamma[np.repeat(np.arange(R_HEADS), R_QK)][None, :]
    qdec = np.exp(lg_lane * (i[:, None] + 1.0))
    kdec = np.exp(lg_lane * (L - 1.0 - i[:, None]))
    sdec = np.exp(lg_lane * L)
    return tuple(a.astype(np.float32) for a in (D, qdec, kdec, sdec))


def _head_constants():
    blk = (np.arange(G_V_W)[:, None] // G_V) == (np.arange(G_QK_W)[None, :] // G_QK)
    seg = ((np.arange(G_V_W)[:, None] // G_V) == (np.arange(G_V_W)[None, :] // G_V)) / float(G_V)
    return blk.astype(np.float32), seg.astype(np.float32)


def _rope_kernel(pos_ref, invf_ref, sgn_ref, cos_ref, sin_ref):
    ang = pos_ref[0].astype(f32) * invf_ref[...]
    cos_ref[0] = jnp.cos(ang)
    sin_ref[0] = jnp.sin(ang) * sgn_ref[...]


def _rope_tables(positions):
    B, S = positions.shape
    r = np.arange(LANES) % R_QK
    inv_freq = ROPE_BASE ** (-jnp.arange(0, R_QK, 2, dtype=f32) / R_QK)
    invf = jnp.tile(inv_freq, LANES // (R_QK // 2))[None, :]
    sgn = jnp.asarray(np.where(r < R_QK // 2, -1.0, 1.0).astype(np.float32)[None, :])
    return pl.pallas_call(
        _rope_kernel,
        grid=(B,),
        in_specs=[pl.BlockSpec((1, S, 1), lambda b: (b, 0, 0)),
                  pl.BlockSpec((1, LANES), lambda b: (0, 0)),
                  pl.BlockSpec((1, LANES), lambda b: (0, 0))],
        out_specs=[pl.BlockSpec((1, S, LANES), lambda b: (b, 0, 0)),
                   pl.BlockSpec((1, S, LANES), lambda b: (b, 0, 0))],
        out_shape=[jax.ShapeDtypeStruct((B, S, LANES), f32)] * 2,
        name="rope_tables",
    )(positions[:, :, None], invf, sgn)


def _mixer_kernel(n_t, hc_ref, cos_ref, sin_ref, gpre_ref, gpost_ref, win_ref, wout_ref,
                  convw_ref, ifb_ref, wup_ref, gab_ref, mnorm_ref, gnorm_ref, rnorm_ref,
                  t_ref, lmb_ref, d_ref, qdec_ref, kdec_ref, sdec_ref, blk_ref, seg_ref,
                  out_ref,
                  pm_ref, pg_ref, pr_ref, qkpad_ref, la_ref, e_ref, vext_ref, y_ref,
                  mc_ref, mm_ref, gs_ref, rs_ref):
    step = pl.program_id(0)
    rm_ref, rg_ref, rr_ref = pm_ref, pg_ref, pr_ref

    def preprocess_mlstm():
        qkpad_ref[8:8 + TS, :] = rm_ref[:, C_MQK:C_MQK + 2 * M_QK_W]
        conv = jnp.zeros((TS, 2 * M_QK_W), f32)
        for j in range(M_CONV):
            conv = conv + convw_ref[0, j:j + 1, :] * qkpad_ref[5 + j:5 + j + TS, :]
        qkpad_ref[0:8, :] = qkpad_ref[TS:TS + 8, :]
        qk = _silu(conv)
        lane_qk = lax.broadcasted_iota(jnp.int32, (1, 2 * M_QK_W), 1)
        pm_ref[:, C_MQK:C_MQK + 2 * M_QK_W] = qk * jnp.where(lane_qk < M_QK_W, 1.0, M_QK ** -0.5)
        for hd in range(M_HEADS):
            vext_ref[:, 2 * hd * M_V:(2 * hd + 1) * M_V] = _bf(
                rm_ref[:, C_MV + hd * M_V:C_MV + (hd + 1) * M_V])
        graw = rm_ref[:, C_GATE:C_GATE + LANES]
        gb = graw + ifb_ref[0]
        lane_g = lax.broadcasted_iota(jnp.int32, (TS, LANES), 1)
        pm_ref[:, C_GATE:C_GATE + LANES] = jnp.where(lane_g < GATE_F, gb, _log_sigmoid(gb)) * LOG2E
        a_pre = _dot(_bf(graw), wup_ref[0]) + gab_ref[0]
        la_ref[...] = _log_sigmoid(a_pre) * (LOG2E / G_NORMALIZER)

    def preprocess_gla_ret(cosf, sins):
        lane_g = lax.broadcasted_iota(jnp.int32, (TS, LANES), 1)
        pg_ref[:, C_GQ:C_GQ + G_QK_W] = rg_ref[:, C_GQ:C_GQ + G_QK_W] * (G_QK ** -0.5)
        first_half = (lane_g % R_QK) < (R_QK // 2)
        for col, scale in ((C_RQ, 1.0), (C_RK, R_QK ** -0.5)):
            xr = rr_ref[:, col:col + R_QK_W]
            swapped = jnp.where(first_half, pltpu.roll(xr, LANES - R_QK // 2, axis=1),
                                pltpu.roll(xr, R_QK // 2, axis=1))
            pr_ref[:, col:col + R_QK_W] = (xr * cosf + swapped * sins) * scale

    @pl.when(step == 0)
    def _():
        for hd in range(M_HEADS):
            vext_ref[:, (2 * hd + 1) * M_V:(2 * hd + 2) * M_V] = jnp.ones((TS, M_V), bf16)

    @pl.when(step % n_t == 0)
    def _():
        qkpad_ref[0:8, :] = jnp.zeros((8, 2 * M_QK_W), f32)
        mc_ref[...] = jnp.zeros_like(mc_ref)
        mm_ref[...] = jnp.zeros_like(mm_ref)
        gs_ref[...] = jnp.zeros_like(gs_ref)
        rs_ref[...] = jnp.zeros_like(rs_ref)

    h_in = hc_ref[0]
    u = _bf(_rms(h_in, gpre_ref[0]))
    rm_ref[...] = _dot_nt(u, win_ref[W_M0:W_G0, :])
    preprocess_mlstm()

    row_i = lax.broadcasted_iota(jnp.int32, (LC, LC), 0)
    col_j = lax.broadcasted_iota(jnp.int32, (LC, LC), 1)
    causal = row_i >= col_j
    top_rows = row_i < M_QK
    lane1 = lax.broadcasted_iota(jnp.int32, (1, LANES), 1)
    pair_mask = [(lane1 < M_QK).astype(f32), (lane1 >= M_QK).astype(f32)]
    pair_mask_b = [m.astype(bf16) for m in pair_mask]
    head_mask_b = [((lane1 // G_QK) == hd).astype(bf16) for hd in range(G_HEADS)]

    def masked_rhs(kb):
        return jnp.concatenate([kb * m for m in head_mask_b], axis=0)

    def pair_tiles(p_heads, vb):
        tiles = []
        for tp in range(2):
            vt = vb[:, tp * LANES:(tp + 1) * LANES]
            tiles.append(_dot(jnp.concatenate([p_heads[2 * tp], p_heads[2 * tp + 1]], axis=1),
                              jnp.concatenate([vt * pair_mask_b[0], vt * pair_mask_b[1]], axis=0)))
        return jnp.concatenate(tiles, axis=1)

    def head_norm_gate(hcat, gain, gate):
        ms = _dot(_bf(hcat * hcat), seg_ref[...])
        return _bf(hcat * lax.rsqrt(ms + EPS) * gain * _silu(gate))

    n_ch = TS // LC
    chunks = range(n_ch)
    heads = range(M_HEADS)
    rows = [slice(c * LC, (c + 1) * LC) for c in chunks]
    t_cum = t_ref[N_LEVELS * LC:(N_LEVELS + 1) * LC, :]

    gblk, bcum, qm, s_m, s_r, m_k, gq, gk, rq, rk = ([] for _ in range(10))
    for c in range(0, n_ch, 2):
        gblk += [pm_ref[rows[c + i], C_GATE:C_GATE + LANES] for i in range(2)]
        cum2 = _dot(t_cum, jnp.concatenate([_split3(gblk[c]), _split3(gblk[c + 1])], axis=1))
        bcum += [cum2[:, :LANES], cum2[:, LANES:]]
        e2 = _dot(t_ref[:, :2 * LC], jnp.concatenate(
            [_split2(la_ref[rows[c], :]), _split2(la_ref[rows[c + 1], :])], axis=1))
        e_ref[c] = e2[:, :LANES]
        e_ref[c + 1] = e2[:, LANES:]
    for c in chunks:
        m_q = [pm_ref[rows[c], C_MQK + tp * LANES:C_MQK + (tp + 1) * LANES] for tp in range(2)]
        m_k.append([pm_ref[rows[c], C_MQK + M_QK_W + tp * LANES:C_MQK + M_QK_W + (tp + 1) * LANES]
                    for tp in range(2)])
        qm.append([m_q[hd // 2] * pair_mask[hd % 2] for hd in heads])
        s_m.append([])
        for tp in range(2):
            kb = _bf(m_k[c][tp])
            s2 = _dot_nt(_bf(m_q[tp]), jnp.concatenate([kb * pair_mask_b[0], kb * pair_mask_b[1]], axis=0))
            s_m[c] += [s2[:, :LC], s2[:, LC:]]

    rg_ref[...] = _dot_nt(u, win_ref[W_G0:W_R0, :])
    rr_ref[...] = _dot_nt(u, win_ref[W_R0:P_COLS, :])

    g_row, b_col, rmax = [], [], []
    for c in chunks:
        xg = jnp.where(col_j < GATE_F, gblk[c], bcum[c])
        xgt = xg.T
        g_row.append([xgt[GATE_I + hd:GATE_I + hd + 1, :] - xgt[GATE_F + hd:GATE_F + hd + 1, :]
                      for hd in heads])
        b_col.append([xg[:, GATE_F + hd:GATE_F + hd + 1] for hd in heads])
        rmax.append([jnp.max(jnp.where(causal, g_row[c][hd], -jnp.inf), axis=1, keepdims=True)
                     for hd in heads])

    m_run = [mm_ref[hd] for hd in heads]
    m_prev, s_old, w_row = [], [], []
    for c in chunks:
        m_prev.append(list(m_run))
        s_old.append([])
        w_row.append([])
        for hd in heads:
            g_max = rmax[c][hd][LC - 1:LC, :]
            b_last = b_col[c][hd][LC - 1:LC, :]
            a_max = b_last + g_max
            m_new = jnp.maximum(b_last + m_run[hd], a_max)
            s_old[c].append(jnp.exp2(b_last + m_run[hd] - m_new))
            w_row[c].append(jnp.exp2(g_row[c][hd] - g_max) * jnp.exp2(a_max - m_new))
            m_run[hd] = m_new
    for hd in heads:
        mm_ref[hd] = m_run[hd]

    mu = [[jnp.maximum(rmax[c][hd], m_prev[c][hd]) for hd in heads] for c in chunks]
    p_m = []
    for c in chunks:
        p_m.append([_bf(jnp.exp2(jnp.where(causal, g_row[c][hd], -jnp.inf) - mu[c][hd]) * s_m[c][hd])
                    for hd in heads])

    vexts, u_m = [], []
    for c in chunks:
        vext = [vext_ref[rows[c], 2 * hd * M_V:(2 * hd + 2) * M_V] for hd in heads]
        vexts.append(vext)
        u_m.append([])
        for tp in range(2):
            kw_t = _bf(m_k[c][tp].T * jnp.where(top_rows, w_row[c][2 * tp], w_row[c][2 * tp + 1]))
            for e in range(2):
                u_m[c].append(_dot(kw_t[e * M_QK:(e + 1) * M_QK], vext[2 * tp + e]))
    c_st = [mc_ref[hd] for hd in heads]
    zero_half = jnp.zeros((M_QK, 2 * M_V), bf16)
    ne_m = []
    for c in chunks:
        ne_m.append([])
        for hd in heads:
            cb = _bf(c_st[hd])
            padded = jnp.concatenate([cb, zero_half] if hd % 2 == 0 else [zero_half, cb], axis=0)
            q_in = _bf(qm[c][hd] * jnp.exp2(m_prev[c][hd] - mu[c][hd]))
            ne_m[c].append(_dot(jnp.concatenate([p_m[c][hd], q_in], axis=1),
                                jnp.concatenate([vexts[c][hd], padded], axis=0)))
            c_st[hd] = s_old[c][hd] * c_st[hd] + u_m[c][hd]
    for hd in heads:
        mc_ref[hd] = c_st[hd]
    for c in chunks:
        for tp in range(2):
            hm = []
            for hd in (2 * tp, 2 * tp + 1):
                ne = ne_m[c][hd]
                num = ne[:, :M_V]
                den = ne[:, M_V:]
                h1 = num / jnp.maximum(jnp.abs(den), jnp.exp2(-(b_col[c][hd] + mu[c][hd])))
                hm.append(h1 * lax.rsqrt(jnp.mean(h1 * h1, axis=-1, keepdims=True) + EPS))
            hm = jnp.concatenate(hm, axis=1)
            cols = slice(2 * tp * M_V, (2 * tp + 2) * M_V)
            o_gate = _sigmoid(pm_ref[rows[c], C_MO + 2 * tp * M_V:C_MO + (2 * tp + 2) * M_V])
            y_ref[rows[c], cols] = _bf(hm * mnorm_ref[0, :, cols] * o_gate)

    preprocess_gla_ret(cos_ref[0], sin_ref[0])
    for c in chunks:
        rq.append(pr_ref[rows[c], C_RQ:C_RQ + R_QK_W])
        rk.append(pr_ref[rows[c], C_RK:C_RK + R_QK_W])
        s_r.append(_dot_nt(_bf(rq[c]), masked_rhs(_bf(rk[c]))))
        gq.append(pg_ref[rows[c], C_GQ:C_GQ + G_QK_W])
        gk.append(pg_ref[rows[c], C_GK:C_GK + G_QK_W])

    RB = 16
    n_rb = LC // RB

    def add_level(acc_c, x, li, blocks):
        for hd in range(G_HEADS):
            for n, b in enumerate(blocks):
                term = (lmb_ref[li, b * RB:(b + 1) * RB, :]
                        * _bf(x[n * RB:(n + 1) * RB, hd * LC:(hd + 1) * LC]))
                acc_c[hd][b] = term if acc_c[hd][b] is None else acc_c[hd][b] + term

    acc = []
    for c in chunks:
        acc.append([[None] * n_rb for _ in range(G_HEADS)])
        add_level(acc[c], _dot_nt(_bf(gq[c]), masked_rhs(_bf(gk[c]))), N_LEVELS, list(range(n_rb)))
    for li in range(N_LEVELS):
        seg = LC >> (li + 1)
        if seg >= RB:
            blocks = [b for b in range(n_rb) if ((b * RB) // seg) % 2 == 1]
        else:
            blocks = list(range(n_rb))
        for c in chunks:
            pw = jnp.exp2(e_ref[c, li * LC:(li + 1) * LC, :])
            qs = _bf(gq[c] * pw)
            if len(blocks) < n_rb:
                qs = jnp.concatenate([qs[b * RB:(b + 1) * RB] for b in blocks], axis=0)
            add_level(acc[c], _dot_nt(qs, masked_rhs(_bf(gk[c] * pw))), li, blocks)
    acc = [[jnp.concatenate(acc[c][hd], axis=0) for hd in range(G_HEADS)] for c in chunks]

    intra_g, u_g, qin_g, dec_g, intra_r, u_r, qin_r = ([] for _ in range(7))
    for c in chunks:
        p_ret = [_bf(d_ref[hd] * s_r[c][:, hd * LC:(hd + 1) * LC]) for hd in range(R_HEADS)]
        rv = pr_ref[rows[c], C_RV:C_RV + R_V_W]
        intra_r.append(pair_tiles(p_ret, _bf(rv)))
        u_r.append(blk_ref[...] * _dot(_bf(rv.T), _bf(rk[c] * kdec_ref[...])))
        qin_r.append(_bf(rq[c] * qdec_ref[...]))
        pb = jnp.exp2(e_ref[c, N_LEVELS * LC:(N_LEVELS + 1) * LC, :])
        pk = jnp.exp2(e_ref[c, (N_LEVELS + 1) * LC:(N_LEVELS + 2) * LC, :])
        gv = pg_ref[rows[c], C_GV:C_GV + G_V_W]
        intra_g.append(pair_tiles(acc[c], _bf(gv)))
        u_g.append(blk_ref[...] * _dot(_bf(gv.T), _bf(gk[c] * pk)))
        qin_g.append(_bf(gq[c] * pb))
        dec_g.append(pb[LC - 1:LC, :])

    g_st = gs_ref[...]
    r_st = rs_ref[...]
    inter_g, inter_r = [], []
    for c in chunks:
        inter_g.append(_dot_nt(qin_g[c], _bf(g_st)))
        g_st = g_st * dec_g[c] + u_g[c]
        inter_r.append(_dot_nt(qin_r[c], _bf(r_st)))
        r_st = r_st * sdec_ref[...] + u_r[c]
    gs_ref[...] = g_st
    rs_ref[...] = r_st

    for c in chunks:
        y_ref[rows[c], M_V_W:M_V_W + G_V_W] = head_norm_gate(
            intra_g[c] + inter_g[c], gnorm_ref[0], pg_ref[rows[c], C_GG:C_GG + G_V_W])
        y_ref[rows[c], M_V_W + G_V_W:M_V_W + G_V_W + R_V_W] = head_norm_gate(
            intra_r[c] + inter_r[c], rnorm_ref[0], pr_ref[rows[c], C_RG:C_RG + R_V_W])

    mixed = _dot(y_ref[...], wout_ref[...])
    out_ref[0] = h_in + _rms(mixed, gpost_ref[0])


def _mixer_layer(layer, h, cosf, sins, params, consts):
    B, S, D = h.shape
    n_t = S // TS
    n_steps = B * n_t

    def tile_of(step):
        return step // n_t, step % n_t, 0

    tile = pl.BlockSpec((1, TS, D), tile_of)
    rope = pl.BlockSpec((1, TS, LANES), tile_of)

    def weight_spec(a):
        return pl.BlockSpec((None,) + a.shape[1:], lambda s: (layer, 0, 0))

    def small_spec(a):
        return pl.BlockSpec((1,) + a.shape[1:], lambda s: (layer, 0, 0))

    def const_spec(a):
        nd = a.ndim
        return pl.BlockSpec(a.shape, lambda s: (0,) * nd)

    gpre, gpost, win, wout, convw, ifb, wup, gab, mnorm, gnorm, rnorm = params
    in_specs = ([tile, rope, rope, small_spec(gpre), small_spec(gpost),
                 weight_spec(win), weight_spec(wout)]
                + [small_spec(a) for a in (convw, ifb, wup, gab, mnorm, gnorm, rnorm)]
                + [const_spec(a) for a in consts])
    return pl.pallas_call(
        functools.partial(_mixer_kernel, n_t),
        grid=(n_steps,),
        in_specs=in_specs,
        out_specs=tile,
        out_shape=jax.ShapeDtypeStruct(h.shape, h.dtype),
        scratch_shapes=[
            pltpu.VMEM((TS, W_G0 - W_M0), f32),
            pltpu.VMEM((TS, W_R0 - W_G0), f32),
            pltpu.VMEM((TS, P_COLS - W_R0), f32),
            pltpu.VMEM((TS + 8, 2 * M_QK_W), f32),
            pltpu.VMEM((TS, LANES), f32),
            pltpu.VMEM((TS // LC, (N_LEVELS + 2) * LC, LANES), f32),
            pltpu.VMEM((TS, 2 * M_V_W), bf16),
            pltpu.VMEM((TS, D_MODEL), bf16),
            pltpu.VMEM((M_HEADS, M_QK, 2 * M_V), f32),
            pltpu.VMEM((M_HEADS, 1, 1), f32),
            pltpu.VMEM((G_V_W, G_QK_W), f32),
            pltpu.VMEM((R_V_W, R_QK_W), f32),
        ],
        compiler_params=pltpu.CompilerParams(
            dimension_semantics=("arbitrary",), vmem_limit_bytes=VMEM_LIMIT),
        name="mixer_layer",
    )(h, cosf, sins, gpre, gpost, win, wout, convw, ifb, wup, gab, mnorm, gnorm, rnorm, *consts)


N_FF_BLK = D_FF // FF_BLK


def _ffn_kernel(layer, h_ref, gpre_ref, gpost_ref, w1_hbm, w2_hbm, out_ref,
                w1b_ref, w2b_ref, st1_ref, st2_ref, sem1, sem2):
    def w1_copy(j, slot):
        return pltpu.make_async_copy(w1_hbm.at[layer, :, pl.ds(j * FF_BLK, FF_BLK)],
                                     st1_ref.at[slot], sem1.at[slot])

    def w2_copy(j, slot):
        return pltpu.make_async_copy(w2_hbm.at[layer, pl.ds(j * FF_BLK, FF_BLK), :],
                                     st2_ref.at[slot], sem2.at[slot])

    @pl.when(pl.program_id(0) == 0)
    def _():
        w1_copy(0, 0).start()
        w2_copy(0, 0).start()
        for j in range(N_FF_BLK):
            slot = j % 2
            if j + 1 < N_FF_BLK:
                w1_copy(j + 1, 1 - slot).start()
                w2_copy(j + 1, 1 - slot).start()
            w1_copy(j, slot).wait()
            w1b_ref[:, j * FF_BLK:(j + 1) * FF_BLK] = _bf(st1_ref[slot])
            w2_copy(j, slot).wait()
            w2b_ref[j * FF_BLK:(j + 1) * FF_BLK, :] = _bf(st2_ref[slot])

    h_in = h_ref[...]
    u = _bf(_rms(h_in, gpre_ref[0]))
    acc = jnp.zeros((TM, D_MODEL), f32)
    for lo in range(0, D_FF, FF_BLK):
        hid = _dot(u, w1b_ref[:, lo:lo + FF_BLK])
        hid = jnp.square(jnp.maximum(hid, 0.0))
        acc = acc + _dot(_bf(hid), w2b_ref[lo:lo + FF_BLK, :])
    out_ref[...] = h_in + _rms(acc, gpost_ref[0])


def _ffn_layer(layer, h2, gpre, gpost, w1, w2):
    n_tok, D = h2.shape
    tile = pl.BlockSpec((TM, D), lambda i: (i, 0))
    return pl.pallas_call(
        functools.partial(_ffn_kernel, layer),
        grid=(n_tok // TM,),
        in_specs=[tile,
                  pl.BlockSpec((1, 1, D), lambda i: (layer, 0, 0)),
                  pl.BlockSpec((1, 1, D), lambda i: (layer, 0, 0)),
                  pl.BlockSpec(memory_space=pl.ANY),
                  pl.BlockSpec(memory_space=pl.ANY)],
        out_specs=tile,
        out_shape=jax.ShapeDtypeStruct(h2.shape, h2.dtype),
        scratch_shapes=[
            pltpu.VMEM((D, D_FF), bf16),
            pltpu.VMEM((D_FF, D), bf16),
            pltpu.VMEM((2, D, FF_BLK), f32),
            pltpu.VMEM((2, FF_BLK, D), f32),
            pltpu.SemaphoreType.DMA((2,)),
            pltpu.SemaphoreType.DMA((2,)),
        ],
        compiler_params=pltpu.CompilerParams(
            dimension_semantics=("arbitrary",), vmem_limit_bytes=VMEM_LIMIT),
        name="ffn_layer",
    )(h2, gpre, gpost, w1, w2)


PACK_COLS = 256


def _pack_kernel(w_ref, o_ref):
    def put(dst, first, last):
        src, n = IN_OFFS[first], IN_OFFS[last + 1] - IN_OFFS[first]
        o_ref[0, dst:dst + n, :] = w_ref[0, src:src + n, :].astype(bf16)

    put(C_MQK, 0, 2)
    gate = jnp.concatenate([w_ref[0, IN_OFFS[3]:IN_OFFS[5], :], w_ref[0, IN_OFFS[9]:IN_OFFS[10], :],
                            jnp.zeros((LANES - 2 * M_HEADS - G_RANK, PACK_COLS), f32)], axis=0)
    o_ref[0, C_GATE:C_GATE + LANES, :] = gate.astype(bf16)
    put(C_MO, 5, 8)
    put(W_G0 + C_GG, 10, 14)


def _pack_w_in(w_in):
    w_t = jnp.transpose(w_in, (0, 2, 1))
    depth, n_in, d = w_t.shape
    return pl.pallas_call(
        _pack_kernel,
        grid=(depth, d // PACK_COLS),
        in_specs=[pl.BlockSpec((1, n_in, PACK_COLS), lambda l, r: (l, 0, r))],
        out_specs=pl.BlockSpec((1, P_COLS, PACK_COLS), lambda l, r: (l, 0, r)),
        out_shape=jax.ShapeDtypeStruct((depth, P_COLS, d), bf16),
        name="pack_w_in",
    )(w_t)


def kernel(x, positions, norm_pre_mix, norm_post_mix, norm_pre_ffn, norm_post_ffn, w_in, mlstm_conv_w, mlstm_i_bias, mlstm_f_bias, mlstm_norm, gla_w_up, gla_gate_bias, gla_norm, ret_norm, w_out, w_ff1, w_ff2):
    B, S, D = x.shape
    t_np, m_np = _chunk_constants()
    blk_np, seg_np = _head_constants()
    consts = ((jnp.asarray(t_np, bf16), jnp.asarray(m_np, bf16))
              + tuple(jnp.asarray(a) for a in _ret_constants())
              + (jnp.asarray(blk_np), jnp.asarray(seg_np, bf16)))

    cosf, sins = _rope_tables(positions)

    ifb = jnp.concatenate([mlstm_i_bias, mlstm_f_bias,
                           jnp.zeros((DEPTH, LANES - 2 * M_HEADS), f32)], axis=-1)
    wup = jnp.zeros((DEPTH, LANES, G_QK_W), f32).at[:, GATE_A:GATE_A + G_RANK, :].set(gla_w_up)
    row = lambda a: a[:, None, :]
    params = (row(norm_pre_mix), row(norm_post_mix), _pack_w_in(w_in), w_out.astype(bf16),
              mlstm_conv_w, row(ifb), wup.astype(bf16), row(gla_gate_bias),
              row(mlstm_norm), row(gla_norm), row(ret_norm))
    gpre_f, gpost_f = row(norm_pre_ffn), row(norm_post_ffn)

    h = x
    for l in range(DEPTH):
        h = _mixer_layer(l, h, cosf, sins, params, consts)
        h = _ffn_layer(l, h.reshape(B * S, D), gpre_f, gpost_f, w_ff1, w_ff2).reshape(B, S, D)
    return h
```

```python
import functools

import numpy as np
import jax
import jax.numpy as jnp
from jax import lax
from jax.experimental import pallas as pl
from jax.experimental.pallas import tpu as pltpu

D_MODEL = 1024
DEPTH = 4
EPS = 1e-6
M_HEADS, M_V, M_QK, M_CONV = 4, 128, 64, 4
G_HEADS, G_V, G_QK, G_RANK = 4, 64, 32, 16
G_NORMALIZER = 16.0
R_HEADS, R_V, R_QK = 4, 64, 32
ROPE_BASE = 10000.0
D_FF = 4 * D_MODEL

M_QK_W, M_V_W = M_HEADS * M_QK, M_HEADS * M_V
G_QK_W, G_V_W = G_HEADS * G_QK, G_HEADS * G_V
R_QK_W, R_V_W = R_HEADS * R_QK, R_HEADS * R_V
IN_SIZES = (M_QK_W, M_QK_W, M_V_W, M_HEADS, M_HEADS, M_V_W,
            G_QK_W, G_QK_W, G_V_W, G_RANK, G_V_W,
            R_QK_W, R_QK_W, R_V_W, R_V_W)
IN_OFFS = tuple(int(v) for v in np.cumsum((0,) + IN_SIZES))

LANES = 128
C_MQK = 0
C_MV = C_MQK + 2 * M_QK_W
C_GATE = C_MV + M_V_W
C_MO = C_GATE + LANES
C_GQ = 0
C_GK = C_GQ + G_QK_W
C_GV = C_GK + G_QK_W
C_GG = C_GV + G_V_W
C_RQ = 0
C_RK = C_RQ + R_QK_W
C_RV = C_RK + R_QK_W
C_RG = C_RV + R_V_W
W_M0 = 0
W_G0 = C_MO + M_V_W
W_R0 = W_G0 + C_GG + G_V_W
P_COLS = W_R0 + C_RG + R_V_W
GATE_I, GATE_F, GATE_A = 0, M_HEADS, 2 * M_HEADS

LC = 128
N_LEVELS = 7
TS = 512
TM = 512
FF_BLK = 1024
VMEM_LIMIT = 56 * 1024 * 1024

LOG2E = 1.4426950408889634
_NT = (((1,), (1,)), ((), ()))
f32 = jnp.float32
bf16 = jnp.bfloat16


def _bf(x):
    return x.astype(bf16)


def _dot(a, b):
    return jnp.dot(a, b, preferred_element_type=f32)


def _dot_nt(a, b):
    return lax.dot_general(a, b, _NT, preferred_element_type=f32)


def _rms(x, g):
    return x * lax.rsqrt(jnp.mean(x * x, axis=-1, keepdims=True) + EPS) * g


def _log_sigmoid(x):
    return jnp.minimum(x, 0.0) - jnp.log(1.0 + jnp.exp(-jnp.abs(x)))


def _sigmoid(x):
    return 0.5 * jnp.tanh(0.5 * x) + 0.5


def _silu(x):
    half = 0.5 * x
    return half + half * jnp.tanh(half)


def _split3(x):
    hi = _bf(x)
    r1 = x - hi.astype(f32)
    mid = _bf(r1)
    lo = _bf(r1 - mid.astype(f32))
    return jnp.concatenate([hi, mid, lo], axis=0)


def _split2(x):
    hi = _bf(x)
    return jnp.concatenate([hi, _bf(x - hi.astype(f32))], axis=0)


def _chunk_constants():
    L = LC
    T = np.zeros((N_LEVELS + 2, L, L), np.float32)
    M = np.zeros((N_LEVELS + 1, L, L), np.float32)
    idx = np.arange(L)
    for li in range(N_LEVELS):
        s = L >> (li + 1)
        for i in range(L):
            seg = i // s
            if seg % 2 == 1:
                T[li, i, seg * s:i + 1] = 1.0
            else:
                T[li, i, i + 1:seg * s + s] = 1.0
        same_blk = (idx[:, None] // (2 * s)) == (idx[None, :] // (2 * s))
        M[li] = same_blk & ((idx[:, None] // s) % 2 == 1) & ((idx[None, :] // s) % 2 == 0)
    M[N_LEVELS] = np.eye(L)
    for i in range(L):
        T[N_LEVELS, i, :i + 1] = 1.0
        T[N_LEVELS + 1, i, i + 1:] = 1.0
    T = T.reshape((N_LEVELS + 2) * L, L)
    return np.concatenate([T, T, T], axis=1), M


def _ret_constants():
    L = LC
    log_gamma = np.log1p(-np.exp2(-5.0 - np.arange(R_HEADS, dtype=np.float64)))
    i = np.arange(L)
    rel = (i[:, None] - i[None, :]).astype(np.float64)
    D = np.where(rel[None] >= 0, np.exp(log_gamma[:, None, None] * rel[None]), 0.0)
    lg_lane = log_gamma[np.repeat(np.arange(R_HEADS), R_QK)][None, :]
    qdec = np.exp(lg_lane * (i[:, None] + 1.0))
    kdec = np.exp(lg_lane * (L - 1.0 - i[:, None]))
    sdec = np.exp(lg_lane * L)
    return tuple(a.astype(np.float32) for a in (D, qdec, kdec, sdec))


def _head_constants():
    blk = (np.arange(G_V_W)[:, None] // G_V) == (np.arange(G_QK_W)[None, :] // G_QK)
    seg = ((np.arange(G_V_W)[:, None] // G_V) == (np.arange(G_V_W)[None, :] // G_V)) / float(G_V)
    return blk.astype(np.float32), seg.astype(np.float32)


def _rope_kernel(pos_ref, invf_ref, sgn_ref, cos_ref, sin_ref):
    ang = pos_ref[0].astype(f32) * invf_ref[...]
    cos_ref[0] = jnp.cos(ang)
    sin_ref[0] = jnp.sin(ang) * sgn_ref[...]


def _rope_tables(positions):
    B, S = positions.shape
    r = np.arange(LANES) % R_QK
    inv_freq = ROPE_BASE ** (-jnp.arange(0, R_QK, 2, dtype=f32) / R_QK)
    invf = jnp.tile(inv_freq, LANES // (R_QK // 2))[None, :]
    sgn = jnp.asarray(np.where(r < R_QK // 2, -1.0, 1.0).astype(np.float32)[None, :])
    return pl.pallas_call(
        _rope_kernel,
        grid=(B,),
        in_specs=[pl.BlockSpec((1, S, 1), lambda b: (b, 0, 0)),
                  pl.BlockSpec((1, LANES), lambda b: (0, 0)),
                  pl.BlockSpec((1, LANES), lambda b: (0, 0))],
        out_specs=[pl.BlockSpec((1, S, LANES), lambda b: (b, 0, 0)),
                   pl.BlockSpec((1, S, LANES), lambda b: (b, 0, 0))],
        out_shape=[jax.ShapeDtypeStruct((B, S, LANES), f32)] * 2,
        name="rope_tables",
    )(positions[:, :, None], invf, sgn)


def _mixer_kernel(n_t, hc_ref, cos_ref, sin_ref, gpre_ref, gpost_ref, win_ref, wout_ref,
                  convw_ref, ifb_ref, wup_ref, gab_ref, mnorm_ref, gnorm_ref, rnorm_ref,
                  t_ref, lmb_ref, d_ref, qdec_ref, kdec_ref, sdec_ref, blk_ref, seg_ref,
                  out_ref,
                  pm_ref, pg_ref, pr_ref, qkpad_ref, la_ref, e_ref, vext_ref, y_ref,
                  mc_ref, mm_ref, gs_ref, rs_ref):
    step = pl.program_id(0)
    rm_ref, rg_ref, rr_ref = pm_ref, pg_ref, pr_ref

    def preprocess_mlstm():
        qkpad_ref[8:8 + TS, :] = rm_ref[:, C_MQK:C_MQK + 2 * M_QK_W]
        conv = jnp.zeros((TS, 2 * M_QK_W), f32)
        for j in range(M_CONV):
            conv = conv + convw_ref[0, j:j + 1, :] * qkpad_ref[5 + j:5 + j + TS, :]
        qkpad_ref[0:8, :] = qkpad_ref[TS:TS + 8, :]
        qk = _silu(conv)
        lane_qk = lax.broadcasted_iota(jnp.int32, (1, 2 * M_QK_W), 1)
        pm_ref[:, C_MQK:C_MQK + 2 * M_QK_W] = qk * jnp.where(lane_qk < M_QK_W, 1.0, M_QK ** -0.5)
        for hd in range(M_HEADS):
            vext_ref[:, 2 * hd * M_V:(2 * hd + 1) * M_V] = _bf(
                rm_ref[:, C_MV + hd * M_V:C_MV + (hd + 1) * M_V])
        graw = rm_ref[:, C_GATE:C_GATE + LANES]
        gb = graw + ifb_ref[0]
        lane_g = lax.broadcasted_iota(jnp.int32, (TS, LANES), 1)
        pm_ref[:, C_GATE:C_GATE + LANES] = jnp.where(lane_g < GATE_F, gb, _log_sigmoid(gb)) * LOG2E
        a_pre = _dot(_bf(graw), wup_ref[0]) + gab_ref[0]
        la_ref[...] = _log_sigmoid(a_pre) * (LOG2E / G_NORMALIZER)

    def preprocess_gla_ret(cosf, sins):
        lane_g = lax.broadcasted_iota(jnp.int32, (TS, LANES), 1)
        pg_ref[:, C_GQ:C_GQ + G_QK_W] = rg_ref[:, C_GQ:C_GQ + G_QK_W] * (G_QK ** -0.5)
        first_half = (lane_g % R_QK) < (R_QK // 2)
        for col, scale in ((C_RQ, 1.0), (C_RK, R_QK ** -0.5)):
            xr = rr_ref[:, col:col + R_QK_W]
            swapped = jnp.where(first_half, pltpu.roll(xr, LANES - R_QK // 2, axis=1),
                                pltpu.roll(xr, R_QK // 2, axis=1))
            pr_ref[:, col:col + R_QK_W] = (xr * cosf + swapped * sins) * scale

    @pl.when(step == 0)
    def _():
        for hd in range(M_HEADS):
            vext_ref[:, (2 * hd + 1) * M_V:(2 * hd + 2) * M_V] = jnp.ones((TS, M_V), bf16)

    @pl.when(step % n_t == 0)
    def _():
        qkpad_ref[0:8, :] = jnp.zeros((8, 2 * M_QK_W), f32)
        mc_ref[...] = jnp.zeros_like(mc_ref)
        mm_ref[...] = jnp.zeros_like(mm_ref)
        gs_ref[...] = jnp.zeros_like(gs_ref)
        rs_ref[...] = jnp.zeros_like(rs_ref)

    h_in = hc_ref[0]
    u = _bf(_rms(h_in, gpre_ref[0]))
    rm_ref[...] = _dot_nt(u, win_ref[W_M0:W_G0, :])
    preprocess_mlstm()

    row_i = lax.broadcasted_iota(jnp.int32, (LC, LC), 0)
    col_j = lax.broadcasted_iota(jnp.int32, (LC, LC), 1)
    causal = row_i >= col_j
    top_rows = row_i < M_QK
    lane1 = lax.broadcasted_iota(jnp.int32, (1, LANES), 1)
    pair_mask = [(lane1 < M_QK).astype(f32), (lane1 >= M_QK).astype(f32)]
    pair_mask_b = [m.astype(bf16) for m in pair_mask]
    head_mask_b = [((lane1 // G_QK) == hd).astype(bf16) for hd in range(G_HEADS)]

    def masked_rhs(kb):
        return jnp.concatenate([kb * m for m in head_mask_b], axis=0)

    def pair_tiles(p_heads, vb):
        tiles = []
        for tp in range(2):
            vt = vb[:, tp * LANES:(tp + 1) * LANES]
            tiles.append(_dot(jnp.concatenate([p_heads[2 * tp], p_heads[2 * tp + 1]], axis=1),
                              jnp.concatenate([vt * pair_mask_b[0], vt * pair_mask_b[1]], axis=0)))
        return jnp.concatenate(tiles, axis=1)

    def head_norm_gate(hcat, gain, gate):
        ms = _dot(_bf(hcat * hcat), seg_ref[...])
        return _bf(hcat * lax.rsqrt(ms + EPS) * gain * _silu(gate))

    n_ch = TS // LC
    chunks = range(n_ch)
    heads = range(M_HEADS)
    rows = [slice(c * LC, (c + 1) * LC) for c in chunks]
    t_cum = t_ref[N_LEVELS * LC:(N_LEVELS + 1) * LC, :]

    gblk, bcum, qm, s_m, s_r, m_k, gq, gk, rq, rk = ([] for _ in range(10))
    for c in range(0, n_ch, 2):
        gblk += [pm_ref[rows[c + i], C_GATE:C_GATE + LANES] for i in range(2)]
        cum2 = _dot(t_cum, jnp.concatenate([_split3(gblk[c]), _split3(gblk[c + 1])], axis=1))
        bcum += [cum2[:, :LANES], cum2[:, LANES:]]
        e2 = _dot(t_ref[:, :2 * LC], jnp.concatenate(
            [_split2(la_ref[rows[c], :]), _split2(la_ref[rows[c + 1], :])], axis=1))
        e_ref[c] = e2[:, :LANES]
        e_ref[c + 1] = e2[:, LANES:]
    for c in chunks:
        m_q = [pm_ref[rows[c], C_MQK + tp * LANES:C_MQK + (tp + 1) * LANES] for tp in range(2)]
        m_k.append([pm_ref[rows[c], C_MQK + M_QK_W + tp * LANES:C_MQK + M_QK_W + (tp + 1) * LANES]
                    for tp in range(2)])
        qm.append([m_q[hd // 2] * pair_mask[hd % 2] for hd in heads])
        s_m.append([])
        for tp in range(2):
            kb = _bf(m_k[c][tp])
            s2 = _dot_nt(_bf(m_q[tp]), jnp.concatenate([kb * pair_mask_b[0], kb * pair_mask_b[1]], axis=0))
            s_m[c] += [s2[:, :LC], s2[:, LC:]]

    rg_ref[...] = _dot_nt(u, win_ref[W_G0:W_R0, :])
    rr_ref[...] = _dot_nt(u, win_ref[W_R0:P_COLS, :])

    g_row, b_col, rmax = [], [], []
    for c in chunks:
        xg = jnp.where(col_j < GATE_F, gblk[c], bcum[c])
        xgt = xg.T
        g_row.append([xgt[GATE_I + hd:GATE_I + hd + 1, :] - xgt[GATE_F + hd:GATE_F + hd + 1, :]
                      for hd in heads])
        b_col.append([xg[:, GATE_F + hd:GATE_F + hd + 1] for hd in heads])
        rmax.append([jnp.max(jnp.where(causal, g_row[c][hd], -jnp.inf), axis=1, keepdims=True)
                     for hd in heads])

    m_run = [mm_ref[hd] for hd in heads]
    m_prev, s_old, w_row = [], [], []
    for c in chunks:
        m_prev.append(list(m_run))
        s_old.append([])
        w_row.append([])
        for hd in heads:
            g_max = rmax[c][hd][LC - 1:LC, :]
            b_last = b_col[c][hd][LC - 1:LC, :]
            a_max = b_last + g_max
            m_new = jnp.maximum(b_last + m_run[hd], a_max)
            s_old[c].append(jnp.exp2(b_last + m_run[hd] - m_new))
            w_row[c].append(jnp.exp2(g_row[c][hd] - g_max) * jnp.exp2(a_max - m_new))
            m_run[hd] = m_new
    for hd in heads:
        mm_ref[hd] = m_run[hd]

    mu = [[jnp.maximum(rmax[c][hd], m_prev[c][hd]) for hd in heads] for c in chunks]
    p_m = []
    for c in chunks:
        p_m.append([_bf(jnp.exp2(jnp.where(causal, g_row[c][hd], -jnp.inf) - mu[c][hd]) * s_m[c][hd])
                    for hd in heads])

    vexts, u_m = [], []
    for c in chunks:
        vext = [vext_ref[rows[c], 2 * hd * M_V:(2 * hd + 2) * M_V] for hd in heads]
        vexts.append(vext)
        u_m.append([])
        for tp in range(2):
            kw_t = _bf(m_k[c][tp].T * jnp.where(top_rows, w_row[c][2 * tp], w_row[c][2 * tp + 1]))
            for e in range(2):
                u_m[c].append(_dot(kw_t[e * M_QK:(e + 1) * M_QK], vext[2 * tp + e]))
    c_st = [mc_ref[hd] for hd in heads]
    zero_half = jnp.zeros((M_QK, 2 * M_V), bf16)
    ne_m = []
    for c in chunks:
        ne_m.append([])
        for hd in heads:
            cb = _bf(c_st[hd])
            padded = jnp.concatenate([cb, zero_half] if hd % 2 == 0 else [zero_half, cb], axis=0)
            q_in = _bf(qm[c][hd] * jnp.exp2(m_prev[c][hd] - mu[c][hd]))
            ne_m[c].append(_dot(jnp.concatenate([p_m[c][hd], q_in], axis=1),
                                jnp.concatenate([vexts[c][hd], padded], axis=0)))
            c_st[hd] = s_old[c][hd] * c_st[hd] + u_m[c][hd]
    for hd in heads:
        mc_ref[hd] = c_st[hd]
    for c in chunks:
        for tp in range(2):
            hm = []
            for hd in (2 * tp, 2 * tp + 1):
                ne = ne_m[c][hd]
                num = ne[:, :M_V]
                den = ne[:, M_V:]
                h1 = num / jnp.maximum(jnp.abs(den), jnp.exp2(-(b_col[c][hd] + mu[c][hd])))
                hm.append(h1 * lax.rsqrt(jnp.mean(h1 * h1, axis=-1, keepdims=True) + EPS))
            hm = jnp.concatenate(hm, axis=1)
            cols = slice(2 * tp * M_V, (2 * tp + 2) * M_V)
            o_gate = _sigmoid(pm_ref[rows[c], C_MO + 2 * tp * M_V:C_MO + (2 * tp + 2) * M_V])
            y_ref[rows[c], cols] = _bf(hm * mnorm_ref[0, :, cols] * o_gate)

    preprocess_gla_ret(cos_ref[0], sin_ref[0])
    for c in chunks:
        rq.append(pr_ref[rows[c], C_RQ:C_RQ + R_QK_W])
        rk.append(pr_ref[rows[c], C_RK:C_RK + R_QK_W])
        s_r.append(_dot_nt(_bf(rq[c]), masked_rhs(_bf(rk[c]))))
        gq.append(pg_ref[rows[c], C_GQ:C_GQ + G_QK_W])
        gk.append(pg_ref[rows[c], C_GK:C_GK + G_QK_W])

    RB = 16
    n_rb = LC // RB

    def add_level(acc_c, x, li, blocks):
        for hd in range(G_HEADS):
            for n, b in enumerate(blocks):
                term = (lmb_ref[li, b * RB:(b + 1) * RB, :]
                        * _bf(x[n * RB:(n + 1) * RB, hd * LC:(hd + 1) * LC]))
                acc_c[hd][b] = term if acc_c[hd][b] is None else acc_c[hd][b] + term

    acc = []
    for c in chunks:
        acc.append([[None] * n_rb for _ in range(G_HEADS)])
        add_level(acc[c], _dot_nt(_bf(gq[c]), masked_rhs(_bf(gk[c]))), N_LEVELS, list(range(n_rb)))
    for li in range(N_LEVELS):
        seg = LC >> (li + 1)
        if seg >= RB:
            blocks = [b for b in range(n_rb) if ((b * RB) // seg) % 2 == 1]
        else:
            blocks = list(range(n_rb))
        for c in chunks:
            pw = jnp.exp2(e_ref[c, li * LC:(li + 1) * LC, :])
            qs = _bf(gq[c] * pw)
            if len(blocks) < n_rb:
                qs = jnp.concatenate([qs[b * RB:(b + 1) * RB] for b in blocks], axis=0)
            add_level(acc[c], _dot_nt(qs, masked_rhs(_bf(gk[c] * pw))), li, blocks)
    acc = [[jnp.concatenate(acc[c][hd], axis=0) for hd in range(G_HEADS)] for c in chunks]

    intra_g, u_g, qin_g, dec_g, intra_r, u_r, qin_r = ([] for _ in range(7))
    for c in chunks:
        p_ret = [_bf(d_ref[hd] * s_r[c][:, hd * LC:(hd + 1) * LC]) for hd in range(R_HEADS)]
        rv = pr_ref[rows[c], C_RV:C_RV + R_V_W]
        intra_r.append(pair_tiles(p_ret, _bf(rv)))
        u_r.append(blk_ref[...] * _dot(_bf(rv.T), _bf(rk[c] * kdec_ref[...])))
        qin_r.append(_bf(rq[c] * qdec_ref[...]))
        pb = jnp.exp2(e_ref[c, N_LEVELS * LC:(N_LEVELS + 1) * LC, :])
        pk = jnp.exp2(e_ref[c, (N_LEVELS + 1) * LC:(N_LEVELS + 2) * LC, :])
        gv = pg_ref[rows[c], C_GV:C_GV + G_V_W]
        intra_g.append(pair_tiles(acc[c], _bf(gv)))
        u_g.append(blk_ref[...] * _dot(_bf(gv.T), _bf(gk[c] * pk)))
        qin_g.append(_bf(gq[c] * pb))
        dec_g.append(pb[LC - 1:LC, :])

    g_st = gs_ref[...]
    r_st = rs_ref[...]
    inter_g, inter_r = [], []
    for c in chunks:
        inter_g.append(_dot_nt(qin_g[c], _bf(g_st)))
        g_st = g_st * dec_g[c] + u_g[c]
        inter_r.append(_dot_nt(qin_r[c], _bf(r_st)))
        r_st = r_st * sdec_ref[...] + u_r[c]
    gs_ref[...] = g_st
    rs_ref[...] = r_st

    for c in chunks:
        y_ref[rows[c], M_V_W:M_V_W + G_V_W] = head_norm_gate(
            intra_g[c] + inter_g[c], gnorm_ref[0], pg_ref[rows[c], C_GG:C_GG + G_V_W])
        y_ref[rows[c], M_V_W + G_V_W:M_V_W + G_V_W + R_V_W] = head_norm_gate(
            intra_r[c] + inter_r[c], rnorm_ref[0], pr_ref[rows[c], C_RG:C_RG + R_V_W])

    mixed = _dot(y_ref[...], wout_ref[...])
    out_ref[0] = h_in + _rms(mixed, gpost_ref[0])


def _mixer_layer(layer, h, cosf, sins, params, consts):
    B, S, D = h.shape
    n_t = S // TS
    n_steps = B * n_t

    def tile_of(step):
        return step // n_t, step % n_t, 0

    tile = pl.BlockSpec((1, TS, D), tile_of)
    rope = pl.BlockSpec((1, TS, LANES), tile_of)

    def weight_spec(a):
        return pl.BlockSpec((None,) + a.shape[1:], lambda s: (layer, 0, 0))

    def small_spec(a):
        return pl.BlockSpec((1,) + a.shape[1:], lambda s: (layer, 0, 0))

    def const_spec(a):
        nd = a.ndim
        return pl.BlockSpec(a.shape, lambda s: (0,) * nd)

    gpre, gpost, win, wout, convw, ifb, wup, gab, mnorm, gnorm, rnorm = params
    in_specs = ([tile, rope, rope, small_spec(gpre), small_spec(gpost),
                 weight_spec(win), weight_spec(wout)]
                + [small_spec(a) for a in (convw, ifb, wup, gab, mnorm, gnorm, rnorm)]
                + [const_spec(a) for a in consts])
    return pl.pallas_call(
        functools.partial(_mixer_kernel, n_t),
        grid=(n_steps,),
        in_specs=in_specs,
        out_specs=tile,
        out_shape=jax.ShapeDtypeStruct(h.shape, h.dtype),
        scratch_shapes=[
            pltpu.VMEM((TS, W_G0 - W_M0), f32),
            pltpu.VMEM((TS, W_R0 - W_G0), f32),
            pltpu.VMEM((TS, P_COLS - W_R0), f32),
            pltpu.VMEM((TS + 8, 2 * M_QK_W), f32),
            pltpu.VMEM((TS, LANES), f32),
            pltpu.VMEM((TS // LC, (N_LEVELS + 2) * LC, LANES), f32),
            pltpu.VMEM((TS, 2 * M_V_W), bf16),
            pltpu.VMEM((TS, D_MODEL), bf16),
            pltpu.VMEM((M_HEADS, M_QK, 2 * M_V), f32),
            pltpu.VMEM((M_HEADS, 1, 1), f32),
            pltpu.VMEM((G_V_W, G_QK_W), f32),
            pltpu.VMEM((R_V_W, R_QK_W), f32),
        ],
        compiler_params=pltpu.CompilerParams(
            dimension_semantics=("arbitrary",), vmem_limit_bytes=VMEM_LIMIT),
        name="mixer_layer",
    )(h, cosf, sins, gpre, gpost, win, wout, convw, ifb, wup, gab, mnorm, gnorm, rnorm, *consts)


N_FF_BLK = D_FF // FF_BLK


def _ffn_kernel(layer, h_ref, gpre_ref, gpost_ref, w1_hbm, w2_hbm, out_ref,
                w1b_ref, w2b_ref, st1_ref, st2_ref, sem1, sem2):
    def w1_copy(j, slot):
        return pltpu.make_async_copy(w1_hbm.at[layer, :, pl.ds(j * FF_BLK, FF_BLK)],
                                     st1_ref.at[slot], sem1.at[slot])

    def w2_copy(j, slot):
        return pltpu.make_async_copy(w2_hbm.at[layer, pl.ds(j * FF_BLK, FF_BLK), :],
                                     st2_ref.at[slot], sem2.at[slot])

    def body(fetch_weights):
        h_in = h_ref[...]
        u = _bf(_rms(h_in, gpre_ref[0]))
        acc = jnp.zeros((TM, D_MODEL), f32)
        if fetch_weights:
            w1_copy(0, 0).start()
            w2_copy(0, 0).start()
        for j in range(N_FF_BLK):
            cols = slice(j * FF_BLK, (j + 1) * FF_BLK)
            if fetch_weights:
                slot = j % 2
                if j + 1 < N_FF_BLK:
                    w1_copy(j + 1, 1 - slot).start()
                    w2_copy(j + 1, 1 - slot).start()
                w1_copy(j, slot).wait()
                w1b_ref[:, cols] = _bf(st1_ref[slot])
                w2_copy(j, slot).wait()
                w2b_ref[cols, :] = _bf(st2_ref[slot])
            hid = _dot(u, w1b_ref[:, cols])
            hid = jnp.square(jnp.maximum(hid, 0.0))
            acc = acc + _dot(_bf(hid), w2b_ref[cols, :])
        out_ref[...] = h_in + _rms(acc, gpost_ref[0])

    @pl.when(pl.program_id(0) == 0)
    def _():
        body(True)

    @pl.when(pl.program_id(0) > 0)
    def _():
        body(False)


def _ffn_layer(layer, h2, gpre, gpost, w1, w2):
    n_tok, D = h2.shape
    tile = pl.BlockSpec((TM, D), lambda i: (i, 0))
    return pl.pallas_call(
        functools.partial(_ffn_kernel, layer),
        grid=(n_tok // TM,),
        in_specs=[tile,
                  pl.BlockSpec((1, 1, D), lambda i: (layer, 0, 0)),
                  pl.BlockSpec((1, 1, D), lambda i: (layer, 0, 0)),
                  pl.BlockSpec(memory_space=pl.ANY),
                  pl.BlockSpec(memory_space=pl.ANY)],
        out_specs=tile,
        out_shape=jax.ShapeDtypeStruct(h2.shape, h2.dtype),
        scratch_shapes=[
            pltpu.VMEM((D, D_FF), bf16),
            pltpu.VMEM((D_FF, D), bf16),
            pltpu.VMEM((2, D, FF_BLK), f32),
            pltpu.VMEM((2, FF_BLK, D), f32),
            pltpu.SemaphoreType.DMA((2,)),
            pltpu.SemaphoreType.DMA((2,)),
        ],
        compiler_params=pltpu.CompilerParams(
            dimension_semantics=("arbitrary",), vmem_limit_bytes=VMEM_LIMIT),
        name="ffn_layer",
    )(h2, gpre, gpost, w1, w2)


PACK_COLS = 256


def _pack_kernel(w_ref, o_ref):
    def put(dst, first, last):
        src, n = IN_OFFS[first], IN_OFFS[last + 1] - IN_OFFS[first]
        o_ref[0, dst:dst + n, :] = w_ref[0, src:src + n, :].astype(bf16)

    put(C_MQK, 0, 2)
    gate = jnp.concatenate([w_ref[0, IN_OFFS[3]:IN_OFFS[5], :], w_ref[0, IN_OFFS[9]:IN_OFFS[10], :],
                            jnp.zeros((LANES - 2 * M_HEADS - G_RANK, PACK_COLS), f32)], axis=0)
    o_ref[0, C_GATE:C_GATE + LANES, :] = gate.astype(bf16)
    put(C_MO, 5, 8)
    put(W_G0 + C_GG, 10, 14)


def _pack_w_in(w_in):
    w_t = jnp.transpose(w_in, (0, 2, 1))
    depth, n_in, d = w_t.shape
    return pl.pallas_call(
        _pack_kernel,
        grid=(depth, d // PACK_COLS),
        in_specs=[pl.BlockSpec((1, n_in, PACK_COLS), lambda l, r: (l, 0, r))],
        out_specs=pl.BlockSpec((1, P_COLS, PACK_COLS), lambda l, r: (l, 0, r)),
        out_shape=jax.ShapeDtypeStruct((depth, P_COLS, d), bf16),
        name="pack_w_in",
    )(w_t)


def kernel(x, positions, norm_pre_mix, norm_post_mix, norm_pre_ffn, norm_post_ffn, w_in, mlstm_conv_w, mlstm_i_bias, mlstm_f_bias, mlstm_norm, gla_w_up, gla_gate_bias, gla_norm, ret_norm, w_out, w_ff1, w_ff2):
    B, S, D = x.shape
    t_np, m_np = _chunk_constants()
    blk_np, seg_np = _head_constants()
    consts = ((jnp.asarray(t_np, bf16), jnp.asarray(m_np, bf16))
              + tuple(jnp.asarray(a) for a in _ret_constants())
              + (jnp.asarray(blk_np), jnp.asarray(seg_np, bf16)))

    cosf, sins = _rope_tables(positions)

    ifb = jnp.concatenate([mlstm_i_bias, mlstm_f_bias,
                           jnp.zeros((DEPTH, LANES - 2 * M_HEADS), f32)], axis=-1)
    wup = jnp.zeros((DEPTH, LANES, G_QK_W), f32).at[:, GATE_A:GATE_A + G_RANK, :].set(gla_w_up)
    row = lambda a: a[:, None, :]
    params = (row(norm_pre_mix), row(norm_post_mix), _pack_w_in(w_in), w_out.astype(bf16),
              mlstm_conv_w, row(ifb), wup.astype(bf16), row(gla_gate_bias),
              row(mlstm_norm), row(gla_norm), row(ret_norm))
    gpre_f, gpost_f = row(norm_pre_ffn), row(norm_post_ffn)

    h = x
    for l in range(DEPTH):
        h = _mixer_layer(l, h, cosf, sins, params, consts)
        h = _ffn_layer(l, h.reshape(B * S, D), gpre_f, gpost_f, w_ff1, w_ff2).reshape(B, S, D)
    return h
```

```python
import functools

import numpy as np
import jax
import jax.numpy as jnp
from jax import lax
from jax.experimental import pallas as pl
from jax.experimental.pallas import tpu as pltpu

D_MODEL = 1024
DEPTH = 4
EPS = 1e-6
M_HEADS, M_V, M_QK, M_CONV = 4, 128, 64, 4
G_HEADS, G_V, G_QK, G_RANK = 4, 64, 32, 16
G_NORMALIZER = 16.0
R_HEADS, R_V, R_QK = 4, 64, 32
ROPE_BASE = 10000.0
D_FF = 4 * D_MODEL

M_QK_W, M_V_W = M_HEADS * M_QK, M_HEADS * M_V
G_QK_W, G_V_W = G_HEADS * G_QK, G_HEADS * G_V
R_QK_W, R_V_W = R_HEADS * R_QK, R_HEADS * R_V
IN_SIZES = (M_QK_W, M_QK_W, M_V_W, M_HEADS, M_HEADS, M_V_W,
            G_QK_W, G_QK_W, G_V_W, G_RANK, G_V_W,
            R_QK_W, R_QK_W, R_V_W, R_V_W)
IN_OFFS = tuple(int(v) for v in np.cumsum((0,) + IN_SIZES))

LANES = 128
C_MQK = 0
C_MV = C_MQK + 2 * M_QK_W
C_GATE = C_MV + M_V_W
C_MO = C_GATE + LANES
C_GQ = 0
C_GK = C_GQ + G_QK_W
C_GV = C_GK + G_QK_W
C_GG = C_GV + G_V_W
C_RQ = 0
C_RK = C_RQ + R_QK_W
C_RV = C_RK + R_QK_W
C_RG = C_RV + R_V_W
W_M0 = 0
W_G0 = C_MO + M_V_W
W_R0 = W_G0 + C_GG + G_V_W
P_COLS = W_R0 + C_RG + R_V_W
MIX = M_V_W + G_V_W + R_V_W
GATE_I, GATE_F, GATE_A = 0, M_HEADS, 2 * M_HEADS

LC = 128
N_LEVELS = 7
TS = 512
TM = 512
FF_BLK = 1024
VMEM_LIMIT = 56 * 1024 * 1024

LOG2E = 1.4426950408889634
_NT = (((1,), (1,)), ((), ()))
f32 = jnp.float32
bf16 = jnp.bfloat16


def _bf(x):
    return x.astype(bf16)


def _dot(a, b):
    return jnp.dot(a, b, preferred_element_type=f32)


def _dot_nt(a, b):
    return lax.dot_general(a, b, _NT, preferred_element_type=f32)


def _rms(x, g):
    return x * lax.rsqrt(jnp.mean(x * x, axis=-1, keepdims=True) + EPS) * g


def _log_sigmoid(x):
    return jnp.minimum(x, 0.0) - jnp.log(1.0 + jnp.exp(-jnp.abs(x)))


def _sigmoid(x):
    return 0.5 * jnp.tanh(0.5 * x) + 0.5


def _silu(x):
    half = 0.5 * x
    return half + half * jnp.tanh(half)


def _split3(x):
    hi = _bf(x)
    r1 = x - hi.astype(f32)
    mid = _bf(r1)
    lo = _bf(r1 - mid.astype(f32))
    return jnp.concatenate([hi, mid, lo], axis=0)


def _split2(x):
    hi = _bf(x)
    return jnp.concatenate([hi, _bf(x - hi.astype(f32))], axis=0)


def _chunk_constants():
    L = LC
    T = np.zeros((N_LEVELS + 2, L, L), np.float32)
    M = np.zeros((N_LEVELS + 1, L, L), np.float32)
    idx = np.arange(L)
    for li in range(N_LEVELS):
        s = L >> (li + 1)
        for i in range(L):
            seg = i // s
            if seg % 2 == 1:
                T[li, i, seg * s:i + 1] = 1.0
            else:
                T[li, i, i + 1:seg * s + s] = 1.0
        same_blk = (idx[:, None] // (2 * s)) == (idx[None, :] // (2 * s))
        M[li] = same_blk & ((idx[:, None] // s) % 2 == 1) & ((idx[None, :] // s) % 2 == 0)
    M[N_LEVELS] = np.eye(L)
    for i in range(L):
        T[N_LEVELS, i, :i + 1] = 1.0
        T[N_LEVELS + 1, i, i + 1:] = 1.0
    T = T.reshape((N_LEVELS + 2) * L, L)
    return np.concatenate([T, T, T], axis=1), M


def _ret_constants():
    L = LC
    log_gamma = np.log1p(-np.exp2(-5.0 - np.arange(R_HEADS, dtype=np.float64)))
    i = np.arange(L)
    rel = (i[:, None] - i[None, :]).astype(np.float64)
    D = np.where(rel[None] >= 0, np.exp(log_gamma[:, None, None] * rel[None]), 0.0)
    lg_lane = log_gamma[np.repeat(np.arange(R_HEADS), R_QK)][None, :]
    qdec = np.exp(lg_lane * (i[:, None] + 1.0))
    kdec = np.exp(lg_lane * (L - 1.0 - i[:, None]))
    sdec = np.exp(lg_lane * L)
    return tuple(a.astype(np.float32) for a in (D, qdec, kdec, sdec))


def _head_constants():
    blk = (np.arange(G_V_W)[:, None] // G_V) == (np.arange(G_QK_W)[None, :] // G_QK)
    seg = ((np.arange(G_V_W)[:, None] // G_V) == (np.arange(G_V_W)[None, :] // G_V)) / float(G_V)
    return blk.astype(np.float32), seg.astype(np.float32)


def _rope_kernel(pos_ref, invf_ref, sgn_ref, cos_ref, sin_ref):
    ang = pos_ref[0].astype(f32) * invf_ref[...]
    cos_ref[0] = jnp.cos(ang)
    sin_ref[0] = jnp.sin(ang) * sgn_ref[...]


def _rope_tables(positions):
    B, S = positions.shape
    r = np.arange(LANES) % R_QK
    inv_freq = ROPE_BASE ** (-jnp.arange(0, R_QK, 2, dtype=f32) / R_QK)
    invf = jnp.tile(inv_freq, LANES // (R_QK // 2))[None, :]
    sgn = jnp.asarray(np.where(r < R_QK // 2, -1.0, 1.0).astype(np.float32)[None, :])
    return pl.pallas_call(
        _rope_kernel,
        grid=(B,),
        in_specs=[pl.BlockSpec((1, S, 1), lambda b: (b, 0, 0)),
                  pl.BlockSpec((1, LANES), lambda b: (0, 0)),
                  pl.BlockSpec((1, LANES), lambda b: (0, 0))],
        out_specs=[pl.BlockSpec((1, S, LANES), lambda b: (b, 0, 0)),
                   pl.BlockSpec((1, S, LANES), lambda b: (b, 0, 0))],
        out_shape=[jax.ShapeDtypeStruct((B, S, LANES), f32)] * 2,
        name="rope_tables",
    )(positions[:, :, None], invf, sgn)


W_CHUNK = 512


def _mixer_kernel(n_t, layer, hc_ref, cos_ref, sin_ref, gpre_ref, gpost_ref, win_hbm, wout_hbm,
                  convw_ref, ifb_ref, wup_ref, gab_ref, mnorm_ref, gnorm_ref, rnorm_ref,
                  t_ref, lmb_ref, d_ref, qdec_ref, kdec_ref, sdec_ref, blk_ref, seg_ref,
                  out_ref,
                  win_ref, wout_ref, stg_ref, gstg_ref, wsem, gsem,
                  pm_ref, pg_ref, pr_ref, qkpad_ref, la_ref, e_ref, vext_ref, y_ref,
                  mc_ref, mm_ref, gs_ref, rs_ref):
    step = pl.program_id(0)
    rm_ref, rg_ref, rr_ref = pm_ref, pg_ref, pr_ref

    def stream_weights():
        moves = []
        for first, last, dst0 in ((0, 2, C_MQK), (5, 8, C_MO), (10, 14, W_G0 + C_GG)):
            for k in range(0, IN_OFFS[last + 1] - IN_OFFS[first], W_CHUNK):
                moves.append((win_hbm.at[layer, pl.ds(IN_OFFS[first] + k, W_CHUNK), :], win_ref, dst0 + k))
        for k in range(0, D_MODEL, W_CHUNK):
            moves.append((wout_hbm.at[layer, pl.ds(k, W_CHUNK), :], wout_ref, k))

        def chunk_copy(i):
            return pltpu.make_async_copy(moves[i][0], stg_ref.at[i % 2], wsem.at[i % 2])

        n_if = IN_OFFS[5] - IN_OFFS[3]
        gate_copies = [
            pltpu.make_async_copy(win_hbm.at[layer, pl.ds(IN_OFFS[3], n_if), :],
                                  gstg_ref.at[pl.ds(0, n_if), :], gsem.at[0]),
            pltpu.make_async_copy(win_hbm.at[layer, pl.ds(IN_OFFS[9], G_RANK), :],
                                  gstg_ref.at[pl.ds(n_if, G_RANK), :], gsem.at[1])]
        for cp in gate_copies:
            cp.start()
        chunk_copy(0).start()
        for i in range(len(moves)):
            if i + 1 < len(moves):
                chunk_copy(i + 1).start()
            chunk_copy(i).wait()
            _, dst_ref, r0 = moves[i]
            dst_ref[r0:r0 + W_CHUNK, :] = _bf(stg_ref[i % 2])
        for cp in gate_copies:
            cp.wait()
        used = n_if + G_RANK
        gate = jnp.concatenate([gstg_ref[0:used, :], jnp.zeros((LANES - used, D_MODEL), f32)], axis=0)
        win_ref[C_GATE:C_GATE + LANES, :] = _bf(gate)

    def preprocess_mlstm():
        qkpad_ref[8:8 + TS, :] = rm_ref[:, C_MQK:C_MQK + 2 * M_QK_W]
        conv = jnp.zeros((TS, 2 * M_QK_W), f32)
        for j in range(M_CONV):
            conv = conv + convw_ref[0, j:j + 1, :] * qkpad_ref[5 + j:5 + j + TS, :]
        qkpad_ref[0:8, :] = qkpad_ref[TS:TS + 8, :]
        qk = _silu(conv)
        lane_qk = lax.broadcasted_iota(jnp.int32, (1, 2 * M_QK_W), 1)
        pm_ref[:, C_MQK:C_MQK + 2 * M_QK_W] = qk * jnp.where(lane_qk < M_QK_W, 1.0, M_QK ** -0.5)
        for hd in range(M_HEADS):
            vext_ref[:, 2 * hd * M_V:(2 * hd + 1) * M_V] = _bf(
                rm_ref[:, C_MV + hd * M_V:C_MV + (hd + 1) * M_V])
        graw = rm_ref[:, C_GATE:C_GATE + LANES]
        gb = graw + ifb_ref[0]
        lane_g = lax.broadcasted_iota(jnp.int32, (TS, LANES), 1)
        pm_ref[:, C_GATE:C_GATE + LANES] = jnp.where(lane_g < GATE_F, gb, _log_sigmoid(gb)) * LOG2E
        a_pre = _dot(_bf(graw), wup_ref[0]) + gab_ref[0]
        la_ref[...] = _log_sigmoid(a_pre) * (LOG2E / G_NORMALIZER)

    def preprocess_gla_ret(cosf, sins):
        lane_g = lax.broadcasted_iota(jnp.int32, (TS, LANES), 1)
        pg_ref[:, C_GQ:C_GQ + G_QK_W] = rg_ref[:, C_GQ:C_GQ + G_QK_W] * (G_QK ** -0.5)
        first_half = (lane_g % R_QK) < (R_QK // 2)
        for col, scale in ((C_RQ, 1.0), (C_RK, R_QK ** -0.5)):
            xr = rr_ref[:, col:col + R_QK_W]
            swapped = jnp.where(first_half, pltpu.roll(xr, LANES - R_QK // 2, axis=1),
                                pltpu.roll(xr, R_QK // 2, axis=1))
            pr_ref[:, col:col + R_QK_W] = (xr * cosf + swapped * sins) * scale

    @pl.when(step == 0)
    def _():
        stream_weights()
        for hd in range(M_HEADS):
            vext_ref[:, (2 * hd + 1) * M_V:(2 * hd + 2) * M_V] = jnp.ones((TS, M_V), bf16)

    @pl.when(step % n_t == 0)
    def _():
        qkpad_ref[0:8, :] = jnp.zeros((8, 2 * M_QK_W), f32)
        mc_ref[...] = jnp.zeros_like(mc_ref)
        mm_ref[...] = jnp.zeros_like(mm_ref)
        gs_ref[...] = jnp.zeros_like(gs_ref)
        rs_ref[...] = jnp.zeros_like(rs_ref)

    h_in = hc_ref[0]
    u = _bf(_rms(h_in, gpre_ref[0]))
    rm_ref[...] = _dot_nt(u, win_ref[W_M0:W_G0, :])
    preprocess_mlstm()

    row_i = lax.broadcasted_iota(jnp.int32, (LC, LC), 0)
    col_j = lax.broadcasted_iota(jnp.int32, (LC, LC), 1)
    causal = row_i >= col_j
    top_rows = row_i < M_QK
    lane1 = lax.broadcasted_iota(jnp.int32, (1, LANES), 1)
    pair_mask = [(lane1 < M_QK).astype(f32), (lane1 >= M_QK).astype(f32)]
    pair_mask_b = [m.astype(bf16) for m in pair_mask]
    head_mask_b = [((lane1 // G_QK) == hd).astype(bf16) for hd in range(G_HEADS)]

    def masked_rhs(kb):
        return jnp.concatenate([kb * m for m in head_mask_b], axis=0)

    def pair_tiles(p_heads, vb):
        tiles = []
        for tp in range(2):
            vt = vb[:, tp * LANES:(tp + 1) * LANES]
            tiles.append(_dot(jnp.concatenate([p_heads[2 * tp], p_heads[2 * tp + 1]], axis=1),
                              jnp.concatenate([vt * pair_mask_b[0], vt * pair_mask_b[1]], axis=0)))
        return jnp.concatenate(tiles, axis=1)

    def head_norm_gate(hcat, gain, gate):
        ms = _dot(_bf(hcat * hcat), seg_ref[...])
        return _bf(hcat * lax.rsqrt(ms + EPS) * gain * _silu(gate))

    n_ch = TS // LC
    chunks = range(n_ch)
    heads = range(M_HEADS)
    rows = [slice(c * LC, (c + 1) * LC) for c in chunks]
    t_cum = t_ref[N_LEVELS * LC:(N_LEVELS + 1) * LC, :]

    gblk, bcum, qm, s_m, s_r, m_k, gq, gk, rq, rk = ([] for _ in range(10))
    for c in range(0, n_ch, 2):
        gblk += [pm_ref[rows[c + i], C_GATE:C_GATE + LANES] for i in range(2)]
        cum2 = _dot(t_cum, jnp.concatenate([_split3(gblk[c]), _split3(gblk[c + 1])], axis=1))
        bcum += [cum2[:, :LANES], cum2[:, LANES:]]
        e2 = _dot(t_ref[:, :2 * LC], jnp.concatenate(
            [_split2(la_ref[rows[c], :]), _split2(la_ref[rows[c + 1], :])], axis=1))
        e_ref[c] = e2[:, :LANES]
        e_ref[c + 1] = e2[:, LANES:]
    for c in chunks:
        m_q = [pm_ref[rows[c], C_MQK + tp * LANES:C_MQK + (tp + 1) * LANES] for tp in range(2)]
        m_k.append([pm_ref[rows[c], C_MQK + M_QK_W + tp * LANES:C_MQK + M_QK_W + (tp + 1) * LANES]
                    for tp in range(2)])
        qm.append([m_q[hd // 2] * pair_mask[hd % 2] for hd in heads])
        s_m.append([])
        for tp in range(2):
            kb = _bf(m_k[c][tp])
            s2 = _dot_nt(_bf(m_q[tp]), jnp.concatenate([kb * pair_mask_b[0], kb * pair_mask_b[1]], axis=0))
            s_m[c] += [s2[:, :LC], s2[:, LC:]]

    rg_ref[...] = _dot_nt(u, win_ref[W_G0:W_R0, :])
    rr_ref[...] = _dot_nt(u, win_ref[W_R0:P_COLS, :])

    g_row, b_col, rmax = [], [], []
    for c in chunks:
        xg = jnp.where(col_j < GATE_F, gblk[c], bcum[c])
        xgt = xg.T
        g_row.append([xgt[GATE_I + hd:GATE_I + hd + 1, :] - xgt[GATE_F + hd:GATE_F + hd + 1, :]
                      for hd in heads])
        b_col.append([xg[:, GATE_F + hd:GATE_F + hd + 1] for hd in heads])
        rmax.append([jnp.max(jnp.where(causal, g_row[c][hd], -jnp.inf), axis=1, keepdims=True)
                     for hd in heads])

    m_run = [mm_ref[hd] for hd in heads]
    m_prev, s_old, w_row = [], [], []
    for c in chunks:
        m_prev.append(list(m_run))
        s_old.append([])
        w_row.append([])
        for hd in heads:
            g_max = rmax[c][hd][LC - 1:LC, :]
            b_last = b_col[c][hd][LC - 1:LC, :]
            a_max = b_last + g_max
            m_new = jnp.maximum(b_last + m_run[hd], a_max)
            s_old[c].append(jnp.exp2(b_last + m_run[hd] - m_new))
            w_row[c].append(jnp.exp2(g_row[c][hd] - g_max) * jnp.exp2(a_max - m_new))
            m_run[hd] = m_new
    for hd in heads:
        mm_ref[hd] = m_run[hd]

    mu = [[jnp.maximum(rmax[c][hd], m_prev[c][hd]) for hd in heads] for c in chunks]
    p_m = []
    for c in chunks:
        p_m.append([_bf(jnp.exp2(jnp.where(causal, g_row[c][hd], -jnp.inf) - mu[c][hd]) * s_m[c][hd])
                    for hd in heads])

    vexts, u_m = [], []
    for c in chunks:
        vext = [vext_ref[rows[c], 2 * hd * M_V:(2 * hd + 2) * M_V] for hd in heads]
        vexts.append(vext)
        u_m.append([])
        for tp in range(2):
            kw_t = _bf(m_k[c][tp].T * jnp.where(top_rows, w_row[c][2 * tp], w_row[c][2 * tp + 1]))
            for e in range(2):
                u_m[c].append(_dot(kw_t[e * M_QK:(e + 1) * M_QK], vext[2 * tp + e]))
    c_st = [mc_ref[hd] for hd in heads]
    zero_half = jnp.zeros((M_QK, 2 * M_V), bf16)
    ne_m = []
    for c in chunks:
        ne_m.append([])
        for hd in heads:
            cb = _bf(c_st[hd])
            padded = jnp.concatenate([cb, zero_half] if hd % 2 == 0 else [zero_half, cb], axis=0)
            q_in = _bf(qm[c][hd] * jnp.exp2(m_prev[c][hd] - mu[c][hd]))
            ne_m[c].append(_dot(jnp.concatenate([p_m[c][hd], q_in], axis=1),
                                jnp.concatenate([vexts[c][hd], padded], axis=0)))
            c_st[hd] = s_old[c][hd] * c_st[hd] + u_m[c][hd]
    for hd in heads:
        mc_ref[hd] = c_st[hd]
    for c in chunks:
        for tp in range(2):
            hm = []
            for hd in (2 * tp, 2 * tp + 1):
                ne = ne_m[c][hd]
                num = ne[:, :M_V]
                den = ne[:, M_V:]
                h1 = num / jnp.maximum(jnp.abs(den), jnp.exp2(-(b_col[c][hd] + mu[c][hd])))
                hm.append(h1 * lax.rsqrt(jnp.mean(h1 * h1, axis=-1, keepdims=True) + EPS))
            hm = jnp.concatenate(hm, axis=1)
            cols = slice(2 * tp * M_V, (2 * tp + 2) * M_V)
            o_gate = _sigmoid(pm_ref[rows[c], C_MO + 2 * tp * M_V:C_MO + (2 * tp + 2) * M_V])
            y_ref[rows[c], cols] = _bf(hm * mnorm_ref[0, :, cols] * o_gate)

    preprocess_gla_ret(cos_ref[0], sin_ref[0])
    for c in chunks:
        rq.append(pr_ref[rows[c], C_RQ:C_RQ + R_QK_W])
        rk.append(pr_ref[rows[c], C_RK:C_RK + R_QK_W])
        s_r.append(_dot_nt(_bf(rq[c]), masked_rhs(_bf(rk[c]))))
        gq.append(pg_ref[rows[c], C_GQ:C_GQ + G_QK_W])
        gk.append(pg_ref[rows[c], C_GK:C_GK + G_QK_W])

    RB = 16
    n_rb = LC // RB

    def add_level(acc_c, x, li, blocks):
        for hd in range(G_HEADS):
            for n, b in enumerate(blocks):
                term = (lmb_ref[li, b * RB:(b + 1) * RB, :]
                        * _bf(x[n * RB:(n + 1) * RB, hd * LC:(hd + 1) * LC]))
                acc_c[hd][b] = term if acc_c[hd][b] is None else acc_c[hd][b] + term

    acc = []
    for c in chunks:
        acc.append([[None] * n_rb for _ in range(G_HEADS)])
        add_level(acc[c], _dot_nt(_bf(gq[c]), masked_rhs(_bf(gk[c]))), N_LEVELS, list(range(n_rb)))
    for li in range(N_LEVELS):
        seg = LC >> (li + 1)
        if seg >= RB:
            blocks = [b for b in range(n_rb) if ((b * RB) // seg) % 2 == 1]
        else:
            blocks = list(range(n_rb))
        for c in chunks:
            pw = jnp.exp2(e_ref[c, li * LC:(li + 1) * LC, :])
            qs = _bf(gq[c] * pw)
            if len(blocks) < n_rb:
                qs = jnp.concatenate([qs[b * RB:(b + 1) * RB] for b in blocks], axis=0)
            add_level(acc[c], _dot_nt(qs, masked_rhs(_bf(gk[c] * pw))), li, blocks)
    acc = [[jnp.concatenate(acc[c][hd], axis=0) for hd in range(G_HEADS)] for c in chunks]

    intra_g, u_g, qin_g, dec_g, intra_r, u_r, qin_r = ([] for _ in range(7))
    for c in chunks:
        p_ret = [_bf(d_ref[hd] * s_r[c][:, hd * LC:(hd + 1) * LC]) for hd in range(R_HEADS)]
        rv = pr_ref[rows[c], C_RV:C_RV + R_V_W]
        intra_r.append(pair_tiles(p_ret, _bf(rv)))
        u_r.append(blk_ref[...] * _dot(_bf(rv.T), _bf(rk[c] * kdec_ref[...])))
        qin_r.append(_bf(rq[c] * qdec_ref[...]))
        pb = jnp.exp2(e_ref[c, N_LEVELS * LC:(N_LEVELS + 1) * LC, :])
        pk = jnp.exp2(e_ref[c, (N_LEVELS + 1) * LC:(N_LEVELS + 2) * LC, :])
        gv = pg_ref[rows[c], C_GV:C_GV + G_V_W]
        intra_g.append(pair_tiles(acc[c], _bf(gv)))
        u_g.append(blk_ref[...] * _dot(_bf(gv.T), _bf(gk[c] * pk)))
        qin_g.append(_bf(gq[c] * pb))
        dec_g.append(pb[LC - 1:LC, :])

    g_st = gs_ref[...]
    r_st = rs_ref[...]
    inter_g, inter_r = [], []
    for c in chunks:
        inter_g.append(_dot_nt(qin_g[c], _bf(g_st)))
        g_st = g_st * dec_g[c] + u_g[c]
        inter_r.append(_dot_nt(qin_r[c], _bf(r_st)))
        r_st = r_st * sdec_ref[...] + u_r[c]
    gs_ref[...] = g_st
    rs_ref[...] = r_st

    for c in chunks:
        y_ref[rows[c], M_V_W:M_V_W + G_V_W] = head_norm_gate(
            intra_g[c] + inter_g[c], gnorm_ref[0], pg_ref[rows[c], C_GG:C_GG + G_V_W])
        y_ref[rows[c], M_V_W + G_V_W:M_V_W + G_V_W + R_V_W] = head_norm_gate(
            intra_r[c] + inter_r[c], rnorm_ref[0], pr_ref[rows[c], C_RG:C_RG + R_V_W])

    mixed = _dot(y_ref[...], wout_ref[...])
    out_ref[0] = h_in + _rms(mixed, gpost_ref[0])


def _mixer_layer(layer, h, cosf, sins, params, consts):
    B, S, D = h.shape
    n_t = S // TS
    n_steps = B * n_t

    def tile_of(step):
        return step // n_t, step % n_t, 0

    tile = pl.BlockSpec((1, TS, D), tile_of)
    rope = pl.BlockSpec((1, TS, LANES), tile_of)

    hbm = pl.BlockSpec(memory_space=pl.ANY)

    def small_spec(a):
        return pl.BlockSpec((1,) + a.shape[1:], lambda s: (layer, 0, 0))

    def const_spec(a):
        nd = a.ndim
        return pl.BlockSpec(a.shape, lambda s: (0,) * nd)

    gpre, gpost, win, wout, convw, ifb, wup, gab, mnorm, gnorm, rnorm = params
    in_specs = ([tile, rope, rope, small_spec(gpre), small_spec(gpost), hbm, hbm]
                + [small_spec(a) for a in (convw, ifb, wup, gab, mnorm, gnorm, rnorm)]
                + [const_spec(a) for a in consts])
    return pl.pallas_call(
        functools.partial(_mixer_kernel, n_t, layer),
        grid=(n_steps,),
        in_specs=in_specs,
        out_specs=tile,
        out_shape=jax.ShapeDtypeStruct(h.shape, h.dtype),
        scratch_shapes=[
            pltpu.VMEM((P_COLS, D_MODEL), bf16),
            pltpu.VMEM((MIX, D_MODEL), bf16),
            pltpu.VMEM((2, W_CHUNK, D_MODEL), f32),
            pltpu.VMEM((2 * M_HEADS + G_RANK, D_MODEL), f32),
            pltpu.SemaphoreType.DMA((2,)),
            pltpu.SemaphoreType.DMA((2,)),
            pltpu.VMEM((TS, W_G0 - W_M0), f32),
            pltpu.VMEM((TS, W_R0 - W_G0), f32),
            pltpu.VMEM((TS, P_COLS - W_R0), f32),
            pltpu.VMEM((TS + 8, 2 * M_QK_W), f32),
            pltpu.VMEM((TS, LANES), f32),
            pltpu.VMEM((TS // LC, (N_LEVELS + 2) * LC, LANES), f32),
            pltpu.VMEM((TS, 2 * M_V_W), bf16),
            pltpu.VMEM((TS, D_MODEL), bf16),
            pltpu.VMEM((M_HEADS, M_QK, 2 * M_V), f32),
            pltpu.VMEM((M_HEADS, 1, 1), f32),
            pltpu.VMEM((G_V_W, G_QK_W), f32),
            pltpu.VMEM((R_V_W, R_QK_W), f32),
        ],
        compiler_params=pltpu.CompilerParams(
            dimension_semantics=("arbitrary",), vmem_limit_bytes=VMEM_LIMIT),
        name="mixer_layer",
    )(h, cosf, sins, gpre, gpost, win, wout, convw, ifb, wup, gab, mnorm, gnorm, rnorm, *consts)


N_FF_BLK = D_FF // FF_BLK


def _ffn_kernel(layer, h_ref, gpre_ref, gpost_ref, w1_hbm, w2_hbm, out_ref,
                w1b_ref, w2b_ref, st1_ref, st2_ref, sem1, sem2):
    def w1_copy(j, slot):
        return pltpu.make_async_copy(w1_hbm.at[layer, :, pl.ds(j * FF_BLK, FF_BLK)],
                                     st1_ref.at[slot], sem1.at[slot])

    def w2_copy(j, slot):
        return pltpu.make_async_copy(w2_hbm.at[layer, pl.ds(j * FF_BLK, FF_BLK), :],
                                     st2_ref.at[slot], sem2.at[slot])

    def body(fetch_weights):
        h_in = h_ref[...]
        u = _bf(_rms(h_in, gpre_ref[0]))
        acc = jnp.zeros((TM, D_MODEL), f32)
        if fetch_weights:
            w1_copy(0, 0).start()
            w2_copy(0, 0).start()
        for j in range(N_FF_BLK):
            cols = slice(j * FF_BLK, (j + 1) * FF_BLK)
            if fetch_weights:
                slot = j % 2
                if j + 1 < N_FF_BLK:
                    w1_copy(j + 1, 1 - slot).start()
                    w2_copy(j + 1, 1 - slot).start()
                w1_copy(j, slot).wait()
                w1b_ref[:, cols] = _bf(st1_ref[slot])
                w2_copy(j, slot).wait()
                w2b_ref[cols, :] = _bf(st2_ref[slot])
            hid = _dot(u, w1b_ref[:, cols])
            hid = jnp.square(jnp.maximum(hid, 0.0))
            acc = acc + _dot(_bf(hid), w2b_ref[cols, :])
        out_ref[...] = h_in + _rms(acc, gpost_ref[0])

    @pl.when(pl.program_id(0) == 0)
    def _():
        body(True)

    @pl.when(pl.program_id(0) > 0)
    def _():
        body(False)


def _ffn_layer(layer, h2, gpre, gpost, w1, w2):
    n_tok, D = h2.shape
    tile = pl.BlockSpec((TM, D), lambda i: (i, 0))
    return pl.pallas_call(
        functools.partial(_ffn_kernel, layer),
        grid=(n_tok // TM,),
        in_specs=[tile,
                  pl.BlockSpec((1, 1, D), lambda i: (layer, 0, 0)),
                  pl.BlockSpec((1, 1, D), lambda i: (layer, 0, 0)),
                  pl.BlockSpec(memory_space=pl.ANY),
                  pl.BlockSpec(memory_space=pl.ANY)],
        out_specs=tile,
        out_shape=jax.ShapeDtypeStruct(h2.shape, h2.dtype),
        scratch_shapes=[
            pltpu.VMEM((D, D_FF), bf16),
            pltpu.VMEM((D_FF, D), bf16),
            pltpu.VMEM((2, D, FF_BLK), f32),
            pltpu.VMEM((2, FF_BLK, D), f32),
            pltpu.SemaphoreType.DMA((2,)),
            pltpu.SemaphoreType.DMA((2,)),
        ],
        compiler_params=pltpu.CompilerParams(
            dimension_semantics=("arbitrary",), vmem_limit_bytes=VMEM_LIMIT),
        name="ffn_layer",
    )(h2, gpre, gpost, w1, w2)


def kernel(x, positions, norm_pre_mix, norm_post_mix, norm_pre_ffn, norm_post_ffn, w_in, mlstm_conv_w, mlstm_i_bias, mlstm_f_bias, mlstm_norm, gla_w_up, gla_gate_bias, gla_norm, ret_norm, w_out, w_ff1, w_ff2):
    B, S, D = x.shape
    t_np, m_np = _chunk_constants()
    blk_np, seg_np = _head_constants()
    consts = ((jnp.asarray(t_np, bf16), jnp.asarray(m_np, bf16))
              + tuple(jnp.asarray(a) for a in _ret_constants())
              + (jnp.asarray(blk_np), jnp.asarray(seg_np, bf16)))

    cosf, sins = _rope_tables(positions)

    ifb = jnp.concatenate([mlstm_i_bias, mlstm_f_bias,
                           jnp.zeros((DEPTH, LANES - 2 * M_HEADS), f32)], axis=-1)
    wup = jnp.zeros((DEPTH, LANES, G_QK_W), f32).at[:, GATE_A:GATE_A + G_RANK, :].set(gla_w_up)
    row = lambda a: a[:, None, :]
    params = (row(norm_pre_mix), row(norm_post_mix), jnp.transpose(w_in, (0, 2, 1)), w_out,
              mlstm_conv_w, row(ifb), wup.astype(bf16), row(gla_gate_bias),
              row(mlstm_norm), row(gla_norm), row(ret_norm))
    gpre_f, gpost_f = row(norm_pre_ffn), row(norm_post_ffn)

    h = x
    for l in range(DEPTH):
        h = _mixer_layer(l, h, cosf, sins, params, consts)
        h = _ffn_layer(l, h.reshape(B * S, D), gpre_f, gpost_f, w_ff1, w_ff2).reshape(B, S, D)
    return h
```

```python
import functools

import numpy as np
import jax
import jax.numpy as jnp
from jax import lax
from jax.experimental import pallas as pl
from jax.experimental.pallas import tpu as pltpu

D_MODEL = 1024
DEPTH = 4
EPS = 1e-6
M_HEADS, M_V, M_QK, M_CONV = 4, 128, 64, 4
G_HEADS, G_V, G_QK, G_RANK = 4, 64, 32, 16
G_NORMALIZER = 16.0
R_HEADS, R_V, R_QK = 4, 64, 32
ROPE_BASE = 10000.0
D_FF = 4 * D_MODEL

M_QK_W, M_V_W = M_HEADS * M_QK, M_HEADS * M_V
G_QK_W, G_V_W = G_HEADS * G_QK, G_HEADS * G_V
R_QK_W, R_V_W = R_HEADS * R_QK, R_HEADS * R_V
IN_SIZES = (M_QK_W, M_QK_W, M_V_W, M_HEADS, M_HEADS, M_V_W,
            G_QK_W, G_QK_W, G_V_W, G_RANK, G_V_W,
            R_QK_W, R_QK_W, R_V_W, R_V_W)
IN_OFFS = tuple(int(v) for v in np.cumsum((0,) + IN_SIZES))

LANES = 128
C_MQK = 0
C_MV = C_MQK + 2 * M_QK_W
C_GATE = C_MV + M_V_W
C_MO = C_GATE + LANES
C_GQ = 0
C_GK = C_GQ + G_QK_W
C_GV = C_GK + G_QK_W
C_GG = C_GV + G_V_W
C_RQ = 0
C_RK = C_RQ + R_QK_W
C_RV = C_RK + R_QK_W
C_RG = C_RV + R_V_W
W_M0 = 0
W_G0 = C_MO + M_V_W
W_R0 = W_G0 + C_GG + G_V_W
P_COLS = W_R0 + C_RG + R_V_W
MIX = M_V_W + G_V_W + R_V_W
GATE_I, GATE_F, GATE_A = 0, M_HEADS, 2 * M_HEADS

LC = 128
N_LEVELS = 7
TS = 512
TM = 512
FF_BLK = 1024
VMEM_LIMIT = 56 * 1024 * 1024

LOG2E = 1.4426950408889634
_NT = (((1,), (1,)), ((), ()))
f32 = jnp.float32
bf16 = jnp.bfloat16


def _bf(x):
    return x.astype(bf16)


def _dot(a, b):
    return jnp.dot(a, b, preferred_element_type=f32)


def _dot_nt(a, b):
    return lax.dot_general(a, b, _NT, preferred_element_type=f32)


def _rms(x, g):
    return x * lax.rsqrt(jnp.mean(x * x, axis=-1, keepdims=True) + EPS) * g


def _log_sigmoid(x):
    return jnp.minimum(x, 0.0) - jnp.log(1.0 + jnp.exp(-jnp.abs(x)))


def _sigmoid(x):
    return 0.5 * jnp.tanh(0.5 * x) + 0.5


def _silu(x):
    half = 0.5 * x
    return half + half * jnp.tanh(half)


def _split3(x):
    hi = _bf(x)
    r1 = x - hi.astype(f32)
    mid = _bf(r1)
    lo = _bf(r1 - mid.astype(f32))
    return jnp.concatenate([hi, mid, lo], axis=0)


def _split2(x):
    hi = _bf(x)
    return jnp.concatenate([hi, _bf(x - hi.astype(f32))], axis=0)


def _chunk_constants():
    L = LC
    T = np.zeros((N_LEVELS + 2, L, L), np.float32)
    M = np.zeros((N_LEVELS + 1, L, L), np.float32)
    idx = np.arange(L)
    for li in range(N_LEVELS):
        s = L >> (li + 1)
        for i in range(L):
            seg = i // s
            if seg % 2 == 1:
                T[li, i, seg * s:i + 1] = 1.0
            else:
                T[li, i, i + 1:seg * s + s] = 1.0
        same_blk = (idx[:, None] // (2 * s)) == (idx[None, :] // (2 * s))
        M[li] = same_blk & ((idx[:, None] // s) % 2 == 1) & ((idx[None, :] // s) % 2 == 0)
    M[N_LEVELS] = np.eye(L)
    for i in range(L):
        T[N_LEVELS, i, :i + 1] = 1.0
        T[N_LEVELS + 1, i, i + 1:] = 1.0
    T = T.reshape((N_LEVELS + 2) * L, L)
    return np.concatenate([T, T, T], axis=1), M


def _ret_constants():
    L = LC
    log_gamma = np.log1p(-np.exp2(-5.0 - np.arange(R_HEADS, dtype=np.float64)))
    i = np.arange(L)
    rel = (i[:, None] - i[None, :]).astype(np.float64)
    D = np.where(rel[None] >= 0, np.exp(log_gamma[:, None, None] * rel[None]), 0.0)
    lg_lane = log_gamma[np.repeat(np.arange(R_HEADS), R_QK)][None, :]
    qdec = np.exp(lg_lane * (i[:, None] + 1.0))
    kdec = np.exp(lg_lane * (L - 1.0 - i[:, None]))
    sdec = np.exp(lg_lane * L)
    return tuple(a.astype(np.float32) for a in (D, qdec, kdec, sdec))


def _head_constants():
    blk = (np.arange(G_V_W)[:, None] // G_V) == (np.arange(G_QK_W)[None, :] // G_QK)
    seg = ((np.arange(G_V_W)[:, None] // G_V) == (np.arange(G_V_W)[None, :] // G_V)) / float(G_V)
    return blk.astype(np.float32), seg.astype(np.float32)


N_FREQ = R_QK // 2
POS_PER_ROW = LANES // N_FREQ


def _rope_kernel(pos_ref, invf_ref, cos_ref, sin_ref):
    ang = pos_ref[...].astype(f32) * invf_ref[...]
    cos_ref[...] = jnp.cos(ang)
    sin_ref[...] = jnp.sin(ang)


def _rope_tables(positions):
    B, S = positions.shape
    n_rows = B * S // POS_PER_ROW
    inv_freq = ROPE_BASE ** (-jnp.arange(0, R_QK, 2, dtype=f32) / R_QK)
    invf = jnp.tile(inv_freq, POS_PER_ROW)[None, :]
    pos_x = jnp.repeat(positions.reshape(n_rows, POS_PER_ROW), N_FREQ, axis=1)
    rows = 512
    cos_c, sin_c = pl.pallas_call(
        _rope_kernel,
        grid=(n_rows // rows,),
        in_specs=[pl.BlockSpec((rows, LANES), lambda i: (i, 0)),
                  pl.BlockSpec((1, LANES), lambda i: (0, 0))],
        out_specs=[pl.BlockSpec((rows, LANES), lambda i: (i, 0)),
                   pl.BlockSpec((rows, LANES), lambda i: (i, 0))],
        out_shape=[jax.ShapeDtypeStruct((n_rows, LANES), f32)] * 2,
        name="rope_tables",
    )(pos_x, invf)
    sgn = np.where(np.arange(LANES) % R_QK < N_FREQ, -1.0, 1.0).astype(np.float32)

    def expand(t):
        return jnp.tile(t.reshape(B, S, N_FREQ), (1, 1, LANES // N_FREQ))

    return expand(cos_c), expand(sin_c) * sgn


W_CHUNK = 512


def _mixer_kernel(n_t, layer, hc_ref, cos_ref, sin_ref, gpre_ref, gpost_ref, win_hbm, wout_hbm,
                  convw_ref, ifb_ref, wup_ref, gab_ref, mnorm_ref, gnorm_ref, rnorm_ref,
                  t_ref, lmb_ref, d_ref, qdec_ref, kdec_ref, sdec_ref, blk_ref, seg_ref,
                  out_ref,
                  win_ref, wout_ref, stg_ref, gstg_ref, wsem, gsem,
                  pm_ref, pg_ref, pr_ref, qkpad_ref, la_ref, e_ref, vext_ref, y_ref,
                  mc_ref, mm_ref, gs_ref, rs_ref):
    step = pl.program_id(0)
    rm_ref, rg_ref, rr_ref = pm_ref, pg_ref, pr_ref

    def stream_weights():
        moves = []
        for first, last, dst0 in ((0, 2, C_MQK), (5, 8, C_MO), (10, 14, W_G0 + C_GG)):
            for k in range(0, IN_OFFS[last + 1] - IN_OFFS[first], W_CHUNK):
                moves.append((win_hbm.at[layer, pl.ds(IN_OFFS[first] + k, W_CHUNK), :], win_ref, dst0 + k))
        for k in range(0, D_MODEL, W_CHUNK):
            moves.append((wout_hbm.at[layer, pl.ds(k, W_CHUNK), :], wout_ref, k))

        def chunk_copy(i):
            return pltpu.make_async_copy(moves[i][0], stg_ref.at[i % 2], wsem.at[i % 2])

        n_if = IN_OFFS[5] - IN_OFFS[3]
        gate_copies = [
            pltpu.make_async_copy(win_hbm.at[layer, pl.ds(IN_OFFS[3], n_if), :],
                                  gstg_ref.at[pl.ds(0, n_if), :], gsem.at[0]),
            pltpu.make_async_copy(win_hbm.at[layer, pl.ds(IN_OFFS[9], G_RANK), :],
                                  gstg_ref.at[pl.ds(n_if, G_RANK), :], gsem.at[1])]
        for cp in gate_copies:
            cp.start()
        chunk_copy(0).start()
        for i in range(len(moves)):
            if i + 1 < len(moves):
                chunk_copy(i + 1).start()
            chunk_copy(i).wait()
            _, dst_ref, r0 = moves[i]
            dst_ref[r0:r0 + W_CHUNK, :] = _bf(stg_ref[i % 2])
        for cp in gate_copies:
            cp.wait()
        used = n_if + G_RANK
        gate = jnp.concatenate([gstg_ref[0:used, :], jnp.zeros((LANES - used, D_MODEL), f32)], axis=0)
        win_ref[C_GATE:C_GATE + LANES, :] = _bf(gate)

    def preprocess_mlstm():
        qkpad_ref[8:8 + TS, :] = rm_ref[:, C_MQK:C_MQK + 2 * M_QK_W]
        conv = jnp.zeros((TS, 2 * M_QK_W), f32)
        for j in range(M_CONV):
            conv = conv + convw_ref[0, j:j + 1, :] * qkpad_ref[5 + j:5 + j + TS, :]
        qkpad_ref[0:8, :] = qkpad_ref[TS:TS + 8, :]
        qk = _silu(conv)
        lane_qk = lax.broadcasted_iota(jnp.int32, (1, 2 * M_QK_W), 1)
        pm_ref[:, C_MQK:C_MQK + 2 * M_QK_W] = qk * jnp.where(lane_qk < M_QK_W, 1.0, M_QK ** -0.5)
        for hd in range(M_HEADS):
            vext_ref[:, 2 * hd * M_V:(2 * hd + 1) * M_V] = _bf(
                rm_ref[:, C_MV + hd * M_V:C_MV + (hd + 1) * M_V])
        graw = rm_ref[:, C_GATE:C_GATE + LANES]
        gb = graw + ifb_ref[0]
        lane_g = lax.broadcasted_iota(jnp.int32, (TS, LANES), 1)
        pm_ref[:, C_GATE:C_GATE + LANES] = jnp.where(lane_g < GATE_F, gb, _log_sigmoid(gb)) * LOG2E
        a_pre = _dot(_bf(graw), wup_ref[0]) + gab_ref[0]
        la_ref[...] = _log_sigmoid(a_pre) * (LOG2E / G_NORMALIZER)

    def preprocess_gla_ret(cosf, sins):
        lane_g = lax.broadcasted_iota(jnp.int32, (TS, LANES), 1)
        pg_ref[:, C_GQ:C_GQ + G_QK_W] = rg_ref[:, C_GQ:C_GQ + G_QK_W] * (G_QK ** -0.5)
        first_half = (lane_g % R_QK) < (R_QK // 2)
        for col, scale in ((C_RQ, 1.0), (C_RK, R_QK ** -0.5)):
            xr = rr_ref[:, col:col + R_QK_W]
            swapped = jnp.where(first_half, pltpu.roll(xr, LANES - R_QK // 2, axis=1),
                                pltpu.roll(xr, R_QK // 2, axis=1))
            pr_ref[:, col:col + R_QK_W] = (xr * cosf + swapped * sins) * scale

    @pl.when(step == 0)
    def _():
        stream_weights()
        for hd in range(M_HEADS):
            vext_ref[:, (2 * hd + 1) * M_V:(2 * hd + 2) * M_V] = jnp.ones((TS, M_V), bf16)

    @pl.when(step % n_t == 0)
    def _():
        qkpad_ref[0:8, :] = jnp.zeros((8, 2 * M_QK_W), f32)
        mc_ref[...] = jnp.zeros_like(mc_ref)
        mm_ref[...] = jnp.zeros_like(mm_ref)
        gs_ref[...] = jnp.zeros_like(gs_ref)
        rs_ref[...] = jnp.zeros_like(rs_ref)

    h_in = hc_ref[0]
    u = _bf(_rms(h_in, gpre_ref[0]))
    rm_ref[...] = _dot_nt(u, win_ref[W_M0:W_G0, :])
    preprocess_mlstm()

    row_i = lax.broadcasted_iota(jnp.int32, (LC, LC), 0)
    col_j = lax.broadcasted_iota(jnp.int32, (LC, LC), 1)
    causal = row_i >= col_j
    top_rows = row_i < M_QK
    lane1 = lax.broadcasted_iota(jnp.int32, (1, LANES), 1)
    pair_mask = [(lane1 < M_QK).astype(f32), (lane1 >= M_QK).astype(f32)]
    pair_mask_b = [m.astype(bf16) for m in pair_mask]
    head_mask_b = [((lane1 // G_QK) == hd).astype(bf16) for hd in range(G_HEADS)]

    def masked_rhs(kb):
        return jnp.concatenate([kb * m for m in head_mask_b], axis=0)

    def pair_tiles(p_heads, vb):
        tiles = []
        for tp in range(2):
            vt = vb[:, tp * LANES:(tp + 1) * LANES]
            tiles.append(_dot(jnp.concatenate([p_heads[2 * tp], p_heads[2 * tp + 1]], axis=1),
                              jnp.concatenate([vt * pair_mask_b[0], vt * pair_mask_b[1]], axis=0)))
        return jnp.concatenate(tiles, axis=1)

    def head_norm_gate(hcat, gain, gate):
        ms = _dot(_bf(hcat * hcat), seg_ref[...])
        return _bf(hcat * lax.rsqrt(ms + EPS) * gain * _silu(gate))

    n_ch = TS // LC
    chunks = range(n_ch)
    heads = range(M_HEADS)
    rows = [slice(c * LC, (c + 1) * LC) for c in chunks]
    t_cum = t_ref[N_LEVELS * LC:(N_LEVELS + 1) * LC, :]

    gblk, bcum, qm, s_m, s_r, m_k, gq, gk, rq, rk = ([] for _ in range(10))
    for c in range(0, n_ch, 2):
        gblk += [pm_ref[rows[c + i], C_GATE:C_GATE + LANES] for i in range(2)]
        cum2 = _dot(t_cum, jnp.concatenate([_split3(gblk[c]), _split3(gblk[c + 1])], axis=1))
        bcum += [cum2[:, :LANES], cum2[:, LANES:]]
        e2 = _dot(t_ref[:, :2 * LC], jnp.concatenate(
            [_split2(la_ref[rows[c], :]), _split2(la_ref[rows[c + 1], :])], axis=1))
        e_ref[c] = e2[:, :LANES]
        e_ref[c + 1] = e2[:, LANES:]
    for c in chunks:
        m_q = [pm_ref[rows[c], C_MQK + tp * LANES:C_MQK + (tp + 1) * LANES] for tp in range(2)]
        m_k.append([pm_ref[rows[c], C_MQK + M_QK_W + tp * LANES:C_MQK + M_QK_W + (tp + 1) * LANES]
                    for tp in range(2)])
        qm.append([m_q[hd // 2] * pair_mask[hd % 2] for hd in heads])
        s_m.append([])
        for tp in range(2):
            kb = _bf(m_k[c][tp])
            s2 = _dot_nt(_bf(m_q[tp]), jnp.concatenate([kb * pair_mask_b[0], kb * pair_mask_b[1]], axis=0))
            s_m[c] += [s2[:, :LC], s2[:, LC:]]

    rg_ref[...] = _dot_nt(u, win_ref[W_G0:W_R0, :])
    rr_ref[...] = _dot_nt(u, win_ref[W_R0:P_COLS, :])

    g_row, b_col, rmax = [], [], []
    for c in chunks:
        xg = jnp.where(col_j < GATE_F, gblk[c], bcum[c])
        xgt = xg.T
        g_row.append([xgt[GATE_I + hd:GATE_I + hd + 1, :] - xgt[GATE_F + hd:GATE_F + hd + 1, :]
                      for hd in heads])
        b_col.append([xg[:, GATE_F + hd:GATE_F + hd + 1] for hd in heads])
        rmax.append([jnp.max(jnp.where(causal, g_row[c][hd], -jnp.inf), axis=1, keepdims=True)
                     for hd in heads])

    m_run = [mm_ref[hd] for hd in heads]
    m_prev, s_old, w_row = [], [], []
    for c in chunks:
        m_prev.append(list(m_run))
        s_old.append([])
        w_row.append([])
        for hd in heads:
            g_max = rmax[c][hd][LC - 1:LC, :]
            b_last = b_col[c][hd][LC - 1:LC, :]
            a_max = b_last + g_max
            m_new = jnp.maximum(b_last + m_run[hd], a_max)
            s_old[c].append(jnp.exp2(b_last + m_run[hd] - m_new))
            w_row[c].append(jnp.exp2(g_row[c][hd] - g_max) * jnp.exp2(a_max - m_new))
            m_run[hd] = m_new
    for hd in heads:
        mm_ref[hd] = m_run[hd]

    mu = [[jnp.maximum(rmax[c][hd], m_prev[c][hd]) for hd in heads] for c in chunks]
    p_m = []
    for c in chunks:
        p_m.append([_bf(jnp.exp2(jnp.where(causal, g_row[c][hd], -jnp.inf) - mu[c][hd]) * s_m[c][hd])
                    for hd in heads])

    vexts, u_m = [], []
    for c in chunks:
        vext = [vext_ref[rows[c], 2 * hd * M_V:(2 * hd + 2) * M_V] for hd in heads]
        vexts.append(vext)
        u_m.append([])
        for tp in range(2):
            kw_t = _bf(m_k[c][tp].T * jnp.where(top_rows, w_row[c][2 * tp], w_row[c][2 * tp + 1]))
            for e in range(2):
                u_m[c].append(_dot(kw_t[e * M_QK:(e + 1) * M_QK], vext[2 * tp + e]))
    c_st = [mc_ref[hd] for hd in heads]
    zero_half = jnp.zeros((M_QK, 2 * M_V), bf16)
    ne_m = []
    for c in chunks:
        ne_m.append([])
        for hd in heads:
            cb = _bf(c_st[hd])
            padded = jnp.concatenate([cb, zero_half] if hd % 2 == 0 else [zero_half, cb], axis=0)
            q_in = _bf(qm[c][hd] * jnp.exp2(m_prev[c][hd] - mu[c][hd]))
            ne_m[c].append(_dot(jnp.concatenate([p_m[c][hd], q_in], axis=1),
                                jnp.concatenate([vexts[c][hd], padded], axis=0)))
            c_st[hd] = s_old[c][hd] * c_st[hd] + u_m[c][hd]
    for hd in heads:
        mc_ref[hd] = c_st[hd]
    for c in chunks:
        for tp in range(2):
            hm = []
            for hd in (2 * tp, 2 * tp + 1):
                ne = ne_m[c][hd]
                num = ne[:, :M_V]
                den = ne[:, M_V:]
                h1 = num / jnp.maximum(jnp.abs(den), jnp.exp2(-(b_col[c][hd] + mu[c][hd])))
                hm.append(h1 * lax.rsqrt(jnp.mean(h1 * h1, axis=-1, keepdims=True) + EPS))
            hm = jnp.concatenate(hm, axis=1)
            cols = slice(2 * tp * M_V, (2 * tp + 2) * M_V)
            o_gate = _sigmoid(pm_ref[rows[c], C_MO + 2 * tp * M_V:C_MO + (2 * tp + 2) * M_V])
            y_ref[rows[c], cols] = _bf(hm * mnorm_ref[0, :, cols] * o_gate)

    preprocess_gla_ret(cos_ref[0], sin_ref[0])
    for c in chunks:
        rq.append(pr_ref[rows[c], C_RQ:C_RQ + R_QK_W])
        rk.append(pr_ref[rows[c], C_RK:C_RK + R_QK_W])
        s_r.append(_dot_nt(_bf(rq[c]), masked_rhs(_bf(rk[c]))))
        gq.append(pg_ref[rows[c], C_GQ:C_GQ + G_QK_W])
        gk.append(pg_ref[rows[c], C_GK:C_GK + G_QK_W])

    RB = 16
    n_rb = LC // RB

    def add_level(acc_c, x, li, blocks):
        for hd in range(G_HEADS):
            for n, b in enumerate(blocks):
                term = (lmb_ref[li, b * RB:(b + 1) * RB, :]
                        * _bf(x[n * RB:(n + 1) * RB, hd * LC:(hd + 1) * LC]))
                acc_c[hd][b] = term if acc_c[hd][b] is None else acc_c[hd][b] + term

    acc = []
    for c in chunks:
        acc.append([[None] * n_rb for _ in range(G_HEADS)])
        add_level(acc[c], _dot_nt(_bf(gq[c]), masked_rhs(_bf(gk[c]))), N_LEVELS, list(range(n_rb)))
    for li in range(N_LEVELS):
        seg = LC >> (li + 1)
        if seg >= RB:
            blocks = [b for b in range(n_rb) if ((b * RB) // seg) % 2 == 1]
        else:
            blocks = list(range(n_rb))
        for c in chunks:
            pw = jnp.exp2(e_ref[c, li * LC:(li + 1) * LC, :])
            qs = _bf(gq[c] * pw)
            if len(blocks) < n_rb:
                qs = jnp.concatenate([qs[b * RB:(b + 1) * RB] for b in blocks], axis=0)
            add_level(acc[c], _dot_nt(qs, masked_rhs(_bf(gk[c] * pw))), li, blocks)
    acc = [[jnp.concatenate(acc[c][hd], axis=0) for hd in range(G_HEADS)] for c in chunks]

    intra_g, u_g, qin_g, dec_g, intra_r, u_r, qin_r = ([] for _ in range(7))
    for c in chunks:
        p_ret = [_bf(d_ref[hd] * s_r[c][:, hd * LC:(hd + 1) * LC]) for hd in range(R_HEADS)]
        rv = pr_ref[rows[c], C_RV:C_RV + R_V_W]
        intra_r.append(pair_tiles(p_ret, _bf(rv)))
        u_r.append(blk_ref[...] * _dot(_bf(rv.T), _bf(rk[c] * kdec_ref[...])))
        qin_r.append(_bf(rq[c] * qdec_ref[...]))
        pb = jnp.exp2(e_ref[c, N_LEVELS * LC:(N_LEVELS + 1) * LC, :])
        pk = jnp.exp2(e_ref[c, (N_LEVELS + 1) * LC:(N_LEVELS + 2) * LC, :])
        gv = pg_ref[rows[c], C_GV:C_GV + G_V_W]
        intra_g.append(pair_tiles(acc[c], _bf(gv)))
        u_g.append(blk_ref[...] * _dot(_bf(gv.T), _bf(gk[c] * pk)))
        qin_g.append(_bf(gq[c] * pb))
        dec_g.append(pb[LC - 1:LC, :])

    g_st = gs_ref[...]
    r_st = rs_ref[...]
    inter_g, inter_r = [], []
    for c in chunks:
        inter_g.append(_dot_nt(qin_g[c], _bf(g_st)))
        g_st = g_st * dec_g[c] + u_g[c]
        inter_r.append(_dot_nt(qin_r[c], _bf(r_st)))
        r_st = r_st * sdec_ref[...] + u_r[c]
    gs_ref[...] = g_st
    rs_ref[...] = r_st

    for c in chunks:
        y_ref[rows[c], M_V_W:M_V_W + G_V_W] = head_norm_gate(
            intra_g[c] + inter_g[c], gnorm_ref[0], pg_ref[rows[c], C_GG:C_GG + G_V_W])
        y_ref[rows[c], M_V_W + G_V_W:M_V_W + G_V_W + R_V_W] = head_norm_gate(
            intra_r[c] + inter_r[c], rnorm_ref[0], pr_ref[rows[c], C_RG:C_RG + R_V_W])

    mixed = _dot(y_ref[...], wout_ref[...])
    out_ref[0] = h_in + _rms(mixed, gpost_ref[0])


def _mixer_layer(layer, h, cosf, sins, params, consts):
    B, S, D = h.shape
    n_t = S // TS
    n_steps = B * n_t

    def tile_of(step):
        return step // n_t, step % n_t, 0

    tile = pl.BlockSpec((1, TS, D), tile_of)
    rope = pl.BlockSpec((1, TS, LANES), tile_of)

    hbm = pl.BlockSpec(memory_space=pl.ANY)

    def small_spec(a):
        return pl.BlockSpec((1,) + a.shape[1:], lambda s: (layer, 0, 0))

    def const_spec(a):
        nd = a.ndim
        return pl.BlockSpec(a.shape, lambda s: (0,) * nd)

    gpre, gpost, win, wout, convw, ifb, wup, gab, mnorm, gnorm, rnorm = params
    in_specs = ([tile, rope, rope, small_spec(gpre), small_spec(gpost), hbm, hbm]
                + [small_spec(a) for a in (convw, ifb, wup, gab, mnorm, gnorm, rnorm)]
                + [const_spec(a) for a in consts])
    return pl.pallas_call(
        functools.partial(_mixer_kernel, n_t, layer),
        grid=(n_steps,),
        in_specs=in_specs,
        out_specs=tile,
        out_shape=jax.ShapeDtypeStruct(h.shape, h.dtype),
        scratch_shapes=[
            pltpu.VMEM((P_COLS, D_MODEL), bf16),
            pltpu.VMEM((MIX, D_MODEL), bf16),
            pltpu.VMEM((2, W_CHUNK, D_MODEL), f32),
            pltpu.VMEM((2 * M_HEADS + G_RANK, D_MODEL), f32),
            pltpu.SemaphoreType.DMA((2,)),
            pltpu.SemaphoreType.DMA((2,)),
            pltpu.VMEM((TS, W_G0 - W_M0), f32),
            pltpu.VMEM((TS, W_R0 - W_G0), f32),
            pltpu.VMEM((TS, P_COLS - W_R0), f32),
            pltpu.VMEM((TS + 8, 2 * M_QK_W), f32),
            pltpu.VMEM((TS, LANES), f32),
            pltpu.VMEM((TS // LC, (N_LEVELS + 2) * LC, LANES), f32),
            pltpu.VMEM((TS, 2 * M_V_W), bf16),
            pltpu.VMEM((TS, D_MODEL), bf16),
            pltpu.VMEM((M_HEADS, M_QK, 2 * M_V), f32),
            pltpu.VMEM((M_HEADS, 1, 1), f32),
            pltpu.VMEM((G_V_W, G_QK_W), f32),
            pltpu.VMEM((R_V_W, R_QK_W), f32),
        ],
        compiler_params=pltpu.CompilerParams(
            dimension_semantics=("arbitrary",), vmem_limit_bytes=VMEM_LIMIT),
        name="mixer_layer",
    )(h, cosf, sins, gpre, gpost, win, wout, convw, ifb, wup, gab, mnorm, gnorm, rnorm, *consts)


N_FF_BLK = D_FF // FF_BLK


def _ffn_kernel(layer, h_ref, gpre_ref, gpost_ref, w1_hbm, w2_hbm, out_ref,
                w1b_ref, w2b_ref, st1_ref, st2_ref, sem1, sem2):
    def w1_copy(j, slot):
        return pltpu.make_async_copy(w1_hbm.at[layer, :, pl.ds(j * FF_BLK, FF_BLK)],
                                     st1_ref.at[slot], sem1.at[slot])

    def w2_copy(j, slot):
        return pltpu.make_async_copy(w2_hbm.at[layer, pl.ds(j * FF_BLK, FF_BLK), :],
                                     st2_ref.at[slot], sem2.at[slot])

    def body(fetch_weights):
        h_in = h_ref[...]
        u = _bf(_rms(h_in, gpre_ref[0]))
        acc = jnp.zeros((TM, D_MODEL), f32)
        if fetch_weights:
            w1_copy(0, 0).start()
            w2_copy(0, 0).start()
        for j in range(N_FF_BLK):
            cols = slice(j * FF_BLK, (j + 1) * FF_BLK)
            if fetch_weights:
                slot = j % 2
                if j + 1 < N_FF_BLK:
                    w1_copy(j + 1, 1 - slot).start()
                    w2_copy(j + 1, 1 - slot).start()
                w1_copy(j, slot).wait()
                w1b_ref[:, cols] = _bf(st1_ref[slot])
                w2_copy(j, slot).wait()
                w2b_ref[cols, :] = _bf(st2_ref[slot])
            hid = _dot(u, w1b_ref[:, cols])
            hid = jnp.square(jnp.maximum(hid, 0.0))
            acc = acc + _dot(_bf(hid), w2b_ref[cols, :])
        out_ref[...] = h_in + _rms(acc, gpost_ref[0])

    @pl.when(pl.program_id(0) == 0)
    def _():
        body(True)

    @pl.when(pl.program_id(0) > 0)
    def _():
        body(False)


def _ffn_layer(layer, h2, gpre, gpost, w1, w2):
    n_tok, D = h2.shape
    tile = pl.BlockSpec((TM, D), lambda i: (i, 0))
    return pl.pallas_call(
        functools.partial(_ffn_kernel, layer),
        grid=(n_tok // TM,),
        in_specs=[tile,
                  pl.BlockSpec((1, 1, D), lambda i: (layer, 0, 0)),
                  pl.BlockSpec((1, 1, D), lambda i: (layer, 0, 0)),
                  pl.BlockSpec(memory_space=pl.ANY),
                  pl.BlockSpec(memory_space=pl.ANY)],
        out_specs=tile,
        out_shape=jax.ShapeDtypeStruct(h2.shape, h2.dtype),
        scratch_shapes=[
            pltpu.VMEM((D, D_FF), bf16),
            pltpu.VMEM((D_FF, D), bf16),
            pltpu.VMEM((2, D, FF_BLK), f32),
            pltpu.VMEM((2, FF_BLK, D), f32),
            pltpu.SemaphoreType.DMA((2,)),
            pltpu.SemaphoreType.DMA((2,)),
        ],
        compiler_params=pltpu.CompilerParams(
            dimension_semantics=("arbitrary",), vmem_limit_bytes=VMEM_LIMIT),
        name="ffn_layer",
    )(h2, gpre, gpost, w1, w2)


def kernel(x, positions, norm_pre_mix, norm_post_mix, norm_pre_ffn, norm_post_ffn, w_in, mlstm_conv_w, mlstm_i_bias, mlstm_f_bias, mlstm_norm, gla_w_up, gla_gate_bias, gla_norm, ret_norm, w_out, w_ff1, w_ff2):
    B, S, D = x.shape
    t_np, m_np = _chunk_constants()
    blk_np, seg_np = _head_constants()
    consts = ((jnp.asarray(t_np, bf16), jnp.asarray(m_np, bf16))
              + tuple(jnp.asarray(a) for a in _ret_constants())
              + (jnp.asarray(blk_np), jnp.asarray(seg_np, bf16)))

    cosf, sins = _rope_tables(positions)

    ifb = jnp.concatenate([mlstm_i_bias, mlstm_f_bias,
                           jnp.zeros((DEPTH, LANES - 2 * M_HEADS), f32)], axis=-1)
    wup = jnp.zeros((DEPTH, LANES, G_QK_W), f32).at[:, GATE_A:GATE_A + G_RANK, :].set(gla_w_up)
    row = lambda a: a[:, None, :]
    params = (row(norm_pre_mix), row(norm_post_mix), jnp.transpose(w_in, (0, 2, 1)), w_out,
              mlstm_conv_w, row(ifb), wup.astype(bf16), row(gla_gate_bias),
              row(mlstm_norm), row(gla_norm), row(ret_norm))
    gpre_f, gpost_f = row(norm_pre_ffn), row(norm_post_ffn)

    h = x
    for l in range(DEPTH):
        h = _mixer_layer(l, h, cosf, sins, params, consts)
        h = _ffn_layer(l, h.reshape(B * S, D), gpre_f, gpost_f, w_ff1, w_ff2).reshape(B, S, D)
    return h
```

```python
import functools

import numpy as np
import jax
import jax.numpy as jnp
from jax import lax
from jax.experimental import pallas as pl
from jax.experimental.pallas import tpu as pltpu

D_MODEL = 1024
DEPTH = 4
EPS = 1e-6
M_HEADS, M_V, M_QK, M_CONV = 4, 128, 64, 4
G_HEADS, G_V, G_QK, G_RANK = 4, 64, 32, 16
G_NORMALIZER = 16.0
R_HEADS, R_V, R_QK = 4, 64, 32
ROPE_BASE = 10000.0
D_FF = 4 * D_MODEL

M_QK_W, M_V_W = M_HEADS * M_QK, M_HEADS * M_V
G_QK_W, G_V_W = G_HEADS * G_QK, G_HEADS * G_V
R_QK_W, R_V_W = R_HEADS * R_QK, R_HEADS * R_V
IN_SIZES = (M_QK_W, M_QK_W, M_V_W, M_HEADS, M_HEADS, M_V_W,
            G_QK_W, G_QK_W, G_V_W, G_RANK, G_V_W,
            R_QK_W, R_QK_W, R_V_W, R_V_W)
IN_OFFS = tuple(int(v) for v in np.cumsum((0,) + IN_SIZES))

LANES = 128
SUBLANES = 8
C_MQK = 0
C_MV = C_MQK + 2 * M_QK_W
C_GATE = C_MV + M_V_W
C_MO = C_GATE + LANES
C_GQ = 0
C_GK = C_GQ + G_QK_W
C_GV = C_GK + G_QK_W
C_GG = C_GV + G_V_W
C_RQ = 0
C_RK = C_RQ + R_QK_W
C_RV = C_RK + R_QK_W
C_RG = C_RV + R_V_W
W_M0 = 0
W_G0 = C_MO + M_V_W
W_R0 = W_G0 + C_GG + G_V_W
P_COLS = W_R0 + C_RG + R_V_W
MIX = M_V_W + G_V_W + R_V_W
GATE_I, GATE_F, GATE_A = 0, M_HEADS, 2 * M_HEADS

LC = 128
N_LEVELS = 7
TS = 512
TM = 512
FF_BLK = 1024
VMEM_LIMIT = 56 * 1024 * 1024

LOG2E = 1.4426950408889634
_NT = (((1,), (1,)), ((), ()))
f32 = jnp.float32
bf16 = jnp.bfloat16


def _bf(x):
    return x.astype(bf16)


def _dot(a, b):
    return jnp.dot(a, b, preferred_element_type=f32)


def _dot_nt(a, b):
    return lax.dot_general(a, b, _NT, preferred_element_type=f32)


def _rms(x, g):
    return x * lax.rsqrt(jnp.mean(x * x, axis=-1, keepdims=True) + EPS) * g


def _log_sigmoid(x):
    return jnp.minimum(x, 0.0) - jnp.log(1.0 + jnp.exp(-jnp.abs(x)))


def _sigmoid(x):
    return 0.5 * jnp.tanh(0.5 * x) + 0.5


def _silu(x):
    half = 0.5 * x
    return half + half * jnp.tanh(half)


def _split3(x):
    hi = _bf(x)
    r1 = x - hi.astype(f32)
    mid = _bf(r1)
    lo = _bf(r1 - mid.astype(f32))
    return jnp.concatenate([hi, mid, lo], axis=0)


def _split2(x):
    hi = _bf(x)
    return jnp.concatenate([hi, _bf(x - hi.astype(f32))], axis=0)


def _chunk_constants():
    L = LC
    T = np.zeros((N_LEVELS + 2, L, L), np.float32)
    M = np.zeros((N_LEVELS + 1, L, L), np.float32)
    idx = np.arange(L)
    for li in range(N_LEVELS):
        s = L >> (li + 1)
        for i in range(L):
            seg = i // s
            if seg % 2 == 1:
                T[li, i, seg * s:i + 1] = 1.0
            else:
                T[li, i, i + 1:seg * s + s] = 1.0
        same_blk = (idx[:, None] // (2 * s)) == (idx[None, :] // (2 * s))
        M[li] = same_blk & ((idx[:, None] // s) % 2 == 1) & ((idx[None, :] // s) % 2 == 0)
    M[N_LEVELS] = np.eye(L)
    for i in range(L):
        T[N_LEVELS, i, :i + 1] = 1.0
        T[N_LEVELS + 1, i, i + 1:] = 1.0
    T = T.reshape((N_LEVELS + 2) * L, L)
    return np.concatenate([T, T, T], axis=1), M


def _ret_constants():
    L = LC
    log_gamma = np.log1p(-np.exp2(-5.0 - np.arange(R_HEADS, dtype=np.float64)))
    i = np.arange(L)
    rel = (i[:, None] - i[None, :]).astype(np.float64)
    D = np.where(rel[None] >= 0, np.exp(log_gamma[:, None, None] * rel[None]), 0.0)
    lg_lane = log_gamma[np.repeat(np.arange(R_HEADS), R_QK)][None, :]
    qdec = np.exp(lg_lane * (i[:, None] + 1.0))
    kdec = np.exp(lg_lane * (L - 1.0 - i[:, None]))
    sdec = np.exp(lg_lane * L)
    return tuple(a.astype(np.float32) for a in (D, qdec, kdec, sdec))


def _head_constants():
    blk = (np.arange(G_V_W)[:, None] // G_V) == (np.arange(G_QK_W)[None, :] // G_QK)
    seg = ((np.arange(G_V_W)[:, None] // G_V) == (np.arange(G_V_W)[None, :] // G_V)) / float(G_V)
    return blk.astype(np.float32), seg.astype(np.float32)


def _rope_kernel(pos_ref, invf_ref, sgn_ref, cos_ref, sin_ref):
    ang = pos_ref[0].astype(f32) * invf_ref[...]
    cos_ref[0] = jnp.cos(ang)
    sin_ref[0] = jnp.sin(ang) * sgn_ref[...]


def _rope_tables(positions):
    B, S = positions.shape
    r = np.arange(LANES) % R_QK
    inv_freq = ROPE_BASE ** (-jnp.arange(0, R_QK, 2, dtype=f32) / R_QK)
    invf = jnp.tile(inv_freq, LANES // (R_QK // 2))[None, :]
    sgn = jnp.asarray(np.where(r < R_QK // 2, -1.0, 1.0).astype(np.float32)[None, :])
    return pl.pallas_call(
        _rope_kernel,
        grid=(B,),
        in_specs=[pl.BlockSpec((1, S, 1), lambda b: (b, 0, 0)),
                  pl.BlockSpec((1, LANES), lambda b: (0, 0)),
                  pl.BlockSpec((1, LANES), lambda b: (0, 0))],
        out_specs=[pl.BlockSpec((1, S, LANES), lambda b: (b, 0, 0)),
                   pl.BlockSpec((1, S, LANES), lambda b: (b, 0, 0))],
        out_shape=[jax.ShapeDtypeStruct((B, S, LANES), f32)] * 2,
        name="rope_tables",
    )(positions[:, :, None], invf, sgn)


W_CHUNK = 512


def _mixer_kernel(n_t, layer, hc_ref, cos_ref, sin_ref, gpre_ref, gpost_ref, win_hbm, wout_hbm,
                  convw_ref, ifb_ref, wup_ref, gab_ref, mnorm_ref, gnorm_ref, rnorm_ref,
                  t_ref, lmb_ref, d_ref, qdec_ref, kdec_ref, sdec_ref, blk_ref, seg_ref,
                  out_ref,
                  win_ref, wout_ref, stg_ref, gstg_ref, wsem, gsem,
                  pm_ref, pg_ref, pr_ref, qkpad_ref, la_ref, e_ref, vext_ref, y_ref,
                  mc_ref, mm_ref, gs_ref, rs_ref):
    step = pl.program_id(0)
    rm_ref, rg_ref, rr_ref = pm_ref, pg_ref, pr_ref

    def stream_weights():
        moves = []
        for first, last, dst0 in ((0, 2, C_MQK), (5, 8, C_MO), (10, 14, W_G0 + C_GG)):
            for k in range(0, IN_OFFS[last + 1] - IN_OFFS[first], W_CHUNK):
                moves.append((win_hbm.at[layer, pl.ds(IN_OFFS[first] + k, W_CHUNK), :], win_ref, dst0 + k))
        for k in range(0, D_MODEL, W_CHUNK):
            moves.append((wout_hbm.at[layer, pl.ds(k, W_CHUNK), :], wout_ref, k))

        def chunk_copy(i):
            return pltpu.make_async_copy(moves[i][0], stg_ref.at[i % 2], wsem.at[i % 2])

        n_if = IN_OFFS[5] - IN_OFFS[3]
        gate_copies = [
            pltpu.make_async_copy(win_hbm.at[layer, pl.ds(IN_OFFS[3], n_if), :],
                                  gstg_ref.at[pl.ds(0, n_if), :], gsem.at[0]),
            pltpu.make_async_copy(win_hbm.at[layer, pl.ds(IN_OFFS[9], G_RANK), :],
                                  gstg_ref.at[pl.ds(n_if, G_RANK), :], gsem.at[1])]
        for cp in gate_copies:
            cp.start()
        chunk_copy(0).start()
        for i in range(len(moves)):
            if i + 1 < len(moves):
                chunk_copy(i + 1).start()
            chunk_copy(i).wait()
            _, dst_ref, r0 = moves[i]
            dst_ref[r0:r0 + W_CHUNK, :] = _bf(stg_ref[i % 2])
        for cp in gate_copies:
            cp.wait()
        used = n_if + G_RANK
        gate = jnp.concatenate([gstg_ref[0:used, :], jnp.zeros((LANES - used, D_MODEL), f32)], axis=0)
        win_ref[C_GATE:C_GATE + LANES, :] = _bf(gate)

    def preprocess_mlstm():
        qkpad_ref[SUBLANES:SUBLANES + TS, :] = rm_ref[:, C_MQK:C_MQK + 2 * M_QK_W]
        first = SUBLANES - (M_CONV - 1)
        conv = jnp.zeros((TS, 2 * M_QK_W), f32)
        for j in range(M_CONV):
            conv = conv + convw_ref[0, j:j + 1, :] * qkpad_ref[first + j:first + j + TS, :]
        qkpad_ref[0:SUBLANES, :] = qkpad_ref[TS:TS + SUBLANES, :]
        qk = _silu(conv)
        lane_qk = lax.broadcasted_iota(jnp.int32, (1, 2 * M_QK_W), 1)
        pm_ref[:, C_MQK:C_MQK + 2 * M_QK_W] = qk * jnp.where(lane_qk < M_QK_W, 1.0, M_QK ** -0.5)
        for hd in range(M_HEADS):
            vext_ref[:, 2 * hd * M_V:(2 * hd + 1) * M_V] = _bf(
                rm_ref[:, C_MV + hd * M_V:C_MV + (hd + 1) * M_V])
        graw = rm_ref[:, C_GATE:C_GATE + LANES]
        gb = graw + ifb_ref[0]
        lane_g = lax.broadcasted_iota(jnp.int32, (TS, LANES), 1)
        pm_ref[:, C_GATE:C_GATE + LANES] = jnp.where(lane_g < GATE_F, gb, _log_sigmoid(gb)) * LOG2E
        a_pre = _dot(_bf(graw), wup_ref[0]) + gab_ref[0]
        la_ref[...] = _log_sigmoid(a_pre) * (LOG2E / G_NORMALIZER)

    def preprocess_gla_ret(cosf, sins):
        lane_g = lax.broadcasted_iota(jnp.int32, (TS, LANES), 1)
        pg_ref[:, C_GQ:C_GQ + G_QK_W] = rg_ref[:, C_GQ:C_GQ + G_QK_W] * (G_QK ** -0.5)
        first_half = (lane_g % R_QK) < (R_QK // 2)
        for col, scale in ((C_RQ, 1.0), (C_RK, R_QK ** -0.5)):
            xr = rr_ref[:, col:col + R_QK_W]
            swapped = jnp.where(first_half, pltpu.roll(xr, LANES - R_QK // 2, axis=1),
                                pltpu.roll(xr, R_QK // 2, axis=1))
            pr_ref[:, col:col + R_QK_W] = (xr * cosf + swapped * sins) * scale

    @pl.when(step == 0)
    def _():
        stream_weights()
        for hd in range(M_HEADS):
            vext_ref[:, (2 * hd + 1) * M_V:(2 * hd + 2) * M_V] = jnp.ones((TS, M_V), bf16)

    @pl.when(step % n_t == 0)
    def _():
        qkpad_ref[0:SUBLANES, :] = jnp.zeros((SUBLANES, 2 * M_QK_W), f32)
        mc_ref[...] = jnp.zeros_like(mc_ref)
        mm_ref[...] = jnp.zeros_like(mm_ref)
        gs_ref[...] = jnp.zeros_like(gs_ref)
        rs_ref[...] = jnp.zeros_like(rs_ref)

    h_in = hc_ref[0]
    u = _bf(_rms(h_in, gpre_ref[0]))
    rm_ref[...] = _dot_nt(u, win_ref[W_M0:W_G0, :])
    preprocess_mlstm()

    row_i = lax.broadcasted_iota(jnp.int32, (LC, LC), 0)
    col_j = lax.broadcasted_iota(jnp.int32, (LC, LC), 1)
    causal = row_i >= col_j
    top_rows = row_i < M_QK
    lane1 = lax.broadcasted_iota(jnp.int32, (1, LANES), 1)
    pair_mask = [(lane1 < M_QK).astype(f32), (lane1 >= M_QK).astype(f32)]
    pair_mask_b = [m.astype(bf16) for m in pair_mask]
    head_mask_b = [((lane1 // G_QK) == hd).astype(bf16) for hd in range(G_HEADS)]

    def loader(ref, row_slice, col0, width):
        return lambda: ref[row_slice, col0:col0 + width]

    def masked_rhs(kb):
        return jnp.concatenate([kb * m for m in head_mask_b], axis=0)

    def pair_tiles(p_heads, vb):
        tiles = []
        for tp in range(2):
            vt = vb[:, tp * LANES:(tp + 1) * LANES]
            tiles.append(_dot(jnp.concatenate([p_heads[2 * tp], p_heads[2 * tp + 1]], axis=1),
                              jnp.concatenate([vt * pair_mask_b[0], vt * pair_mask_b[1]], axis=0)))
        return jnp.concatenate(tiles, axis=1)

    def head_norm_gate(hcat, gain, gate):
        ms = _dot(_bf(hcat * hcat), seg_ref[...])
        return _bf(hcat * lax.rsqrt(ms + EPS) * gain * _silu(gate))

    n_ch = TS // LC
    chunks = range(n_ch)
    heads = range(M_HEADS)
    rows = [slice(c * LC, (c + 1) * LC) for c in chunks]
    t_cum = t_ref[N_LEVELS * LC:(N_LEVELS + 1) * LC, :]

    gblk, bcum, qm, s_m, s_r, m_k, gq, gk, rq, rk = ([] for _ in range(10))
    for c in range(0, n_ch, 2):
        gblk += [pm_ref[rows[c + i], C_GATE:C_GATE + LANES] for i in range(2)]
        cum2 = _dot(t_cum, jnp.concatenate([_split3(gblk[c]), _split3(gblk[c + 1])], axis=1))
        bcum += [cum2[:, :LANES], cum2[:, LANES:]]
        e2 = _dot(t_ref[:, :2 * LC], jnp.concatenate(
            [_split2(la_ref[rows[c], :]), _split2(la_ref[rows[c + 1], :])], axis=1))
        e_ref[c] = e2[:, :LANES]
        e_ref[c + 1] = e2[:, LANES:]
    for c in chunks:
        m_q = [loader(pm_ref, rows[c], C_MQK + tp * LANES, LANES) for tp in range(2)]
        m_k.append([loader(pm_ref, rows[c], C_MQK + M_QK_W + tp * LANES, LANES) for tp in range(2)])
        qm.append([m_q[hd // 2] for hd in heads])
        s_m.append([])
        for tp in range(2):
            kb = _bf(m_k[c][tp]())
            s2 = _dot_nt(_bf(m_q[tp]()), jnp.concatenate([kb * pair_mask_b[0], kb * pair_mask_b[1]], axis=0))
            s_m[c] += [s2[:, :LC], s2[:, LC:]]

    rg_ref[...] = _dot_nt(u, win_ref[W_G0:W_R0, :])
    rr_ref[...] = _dot_nt(u, win_ref[W_R0:P_COLS, :])

    g_row, b_col, rmax = [], [], []
    for c in chunks:
        xg = jnp.where(col_j < GATE_F, gblk[c], bcum[c])
        xgt = xg.T
        g_row.append([xgt[GATE_I + hd:GATE_I + hd + 1, :] - xgt[GATE_F + hd:GATE_F + hd + 1, :]
                      for hd in heads])
        b_col.append([xg[:, GATE_F + hd:GATE_F + hd + 1] for hd in heads])
        rmax.append([jnp.max(jnp.where(causal, g_row[c][hd], -jnp.inf), axis=1, keepdims=True)
                     for hd in heads])

    m_run = [mm_ref[hd] for hd in heads]
    m_prev, s_old, w_row = [], [], []
    for c in chunks:
        m_prev.append(list(m_run))
        s_old.append([])
        w_row.append([])
        for hd in heads:
            g_max = rmax[c][hd][LC - 1:LC, :]
            b_last = b_col[c][hd][LC - 1:LC, :]
            a_max = b_last + g_max
            m_new = jnp.maximum(b_last + m_run[hd], a_max)
            s_old[c].append(jnp.exp2(b_last + m_run[hd] - m_new))
            w_row[c].append(jnp.exp2(g_row[c][hd] - g_max) * jnp.exp2(a_max - m_new))
            m_run[hd] = m_new
    for hd in heads:
        mm_ref[hd] = m_run[hd]

    mu = [[jnp.maximum(rmax[c][hd], m_prev[c][hd]) for hd in heads] for c in chunks]
    p_m = []
    for c in chunks:
        p_m.append([_bf(jnp.exp2(jnp.where(causal, g_row[c][hd], -jnp.inf) - mu[c][hd]) * s_m[c][hd])
                    for hd in heads])

    vexts, u_m = [], []
    for c in chunks:
        vext = [vext_ref[rows[c], 2 * hd * M_V:(2 * hd + 2) * M_V] for hd in heads]
        vexts.append(vext)
        u_m.append([])
        for tp in range(2):
            kw_t = _bf(m_k[c][tp]().T * jnp.where(top_rows, w_row[c][2 * tp], w_row[c][2 * tp + 1]))
            for e in range(2):
                u_m[c].append(_dot(kw_t[e * M_QK:(e + 1) * M_QK], vext[2 * tp + e]))
    c_st = [mc_ref[hd] for hd in heads]
    zero_half = jnp.zeros((M_QK, 2 * M_V), bf16)
    ne_m = []
    for c in chunks:
        ne_m.append([])
        for hd in heads:
            cb = _bf(c_st[hd])
            padded = jnp.concatenate([cb, zero_half] if hd % 2 == 0 else [zero_half, cb], axis=0)
            q_in = _bf(qm[c][hd]() * pair_mask[hd % 2] * jnp.exp2(m_prev[c][hd] - mu[c][hd]))
            ne_m[c].append(_dot(jnp.concatenate([p_m[c][hd], q_in], axis=1),
                                jnp.concatenate([vexts[c][hd], padded], axis=0)))
            c_st[hd] = s_old[c][hd] * c_st[hd] + u_m[c][hd]
    for hd in heads:
        mc_ref[hd] = c_st[hd]
    for c in chunks:
        for tp in range(2):
            hm = []
            for hd in (2 * tp, 2 * tp + 1):
                ne = ne_m[c][hd]
                num = ne[:, :M_V]
                den = ne[:, M_V:]
                h1 = num / jnp.maximum(jnp.abs(den), jnp.exp2(-(b_col[c][hd] + mu[c][hd])))
                hm.append(h1 * lax.rsqrt(jnp.mean(h1 * h1, axis=-1, keepdims=True) + EPS))
            hm = jnp.concatenate(hm, axis=1)
            cols = slice(2 * tp * M_V, (2 * tp + 2) * M_V)
            o_gate = _sigmoid(pm_ref[rows[c], C_MO + 2 * tp * M_V:C_MO + (2 * tp + 2) * M_V])
            y_ref[rows[c], cols] = _bf(hm * mnorm_ref[0, :, cols] * o_gate)

    preprocess_gla_ret(cos_ref[0], sin_ref[0])
    for c in chunks:
        rq.append(loader(pr_ref, rows[c], C_RQ, R_QK_W))
        rk.append(loader(pr_ref, rows[c], C_RK, R_QK_W))
        s_r.append(_dot_nt(_bf(rq[c]()), masked_rhs(_bf(rk[c]()))))
        gq.append(loader(pg_ref, rows[c], C_GQ, G_QK_W))
        gk.append(loader(pg_ref, rows[c], C_GK, G_QK_W))

    RB = 16
    n_rb = LC // RB

    def add_level(acc_c, x, li, blocks):
        for hd in range(G_HEADS):
            for n, b in enumerate(blocks):
                term = (lmb_ref[li, b * RB:(b + 1) * RB, :]
                        * _bf(x[n * RB:(n + 1) * RB, hd * LC:(hd + 1) * LC]))
                acc_c[hd][b] = term if acc_c[hd][b] is None else acc_c[hd][b] + term

    acc = []
    for c in chunks:
        acc.append([[None] * n_rb for _ in range(G_HEADS)])
        add_level(acc[c], _dot_nt(_bf(gq[c]()), masked_rhs(_bf(gk[c]()))), N_LEVELS, list(range(n_rb)))
    for li in range(N_LEVELS):
        seg = LC >> (li + 1)
        if seg >= RB:
            blocks = [b for b in range(n_rb) if ((b * RB) // seg) % 2 == 1]
        else:
            blocks = list(range(n_rb))
        for c in chunks:
            pw = jnp.exp2(e_ref[c, li * LC:(li + 1) * LC, :])
            qs = _bf(gq[c]() * pw)
            if len(blocks) < n_rb:
                qs = jnp.concatenate([qs[b * RB:(b + 1) * RB] for b in blocks], axis=0)
            add_level(acc[c], _dot_nt(qs, masked_rhs(_bf(gk[c]() * pw))), li, blocks)
    acc = [[jnp.concatenate(acc[c][hd], axis=0) for hd in range(G_HEADS)] for c in chunks]

    intra_g, u_g, qin_g, dec_g, intra_r, u_r, qin_r = ([] for _ in range(7))
    for c in chunks:
        p_ret = [_bf(d_ref[hd] * s_r[c][:, hd * LC:(hd + 1) * LC]) for hd in range(R_HEADS)]
        rv = pr_ref[rows[c], C_RV:C_RV + R_V_W]
        intra_r.append(pair_tiles(p_ret, _bf(rv)))
        u_r.append(blk_ref[...] * _dot(_bf(rv.T), _bf(rk[c]() * kdec_ref[...])))
        qin_r.append(_bf(rq[c]() * qdec_ref[...]))
        pb = jnp.exp2(e_ref[c, N_LEVELS * LC:(N_LEVELS + 1) * LC, :])
        pk = jnp.exp2(e_ref[c, (N_LEVELS + 1) * LC:(N_LEVELS + 2) * LC, :])
        gv = pg_ref[rows[c], C_GV:C_GV + G_V_W]
        intra_g.append(pair_tiles(acc[c], _bf(gv)))
        u_g.append(blk_ref[...] * _dot(_bf(gv.T), _bf(gk[c]() * pk)))
        qin_g.append(_bf(gq[c]() * pb))
        dec_g.append(pb[LC - 1:LC, :])

    g_st = gs_ref[...]
    r_st = rs_ref[...]
    inter_g, inter_r = [], []
    for c in chunks:
        inter_g.append(_dot_nt(qin_g[c], _bf(g_st)))
        g_st = g_st * dec_g[c] + u_g[c]
        inter_r.append(_dot_nt(qin_r[c], _bf(r_st)))
        r_st = r_st * sdec_ref[...] + u_r[c]
    gs_ref[...] = g_st
    rs_ref[...] = r_st

    for c in chunks:
        y_ref[rows[c], M_V_W:M_V_W + G_V_W] = head_norm_gate(
            intra_g[c] + inter_g[c], gnorm_ref[0], pg_ref[rows[c], C_GG:C_GG + G_V_W])
        y_ref[rows[c], M_V_W + G_V_W:M_V_W + G_V_W + R_V_W] = head_norm_gate(
            intra_r[c] + inter_r[c], rnorm_ref[0], pr_ref[rows[c], C_RG:C_RG + R_V_W])

    mixed = _dot(y_ref[...], wout_ref[...])
    out_ref[0] = h_in + _rms(mixed, gpost_ref[0])


def _mixer_layer(layer, h, cosf, sins, params, consts):
    B, S, D = h.shape
    n_t = S // TS
    n_steps = B * n_t

    def tile_of(step):
        return step // n_t, step % n_t, 0

    tile = pl.BlockSpec((1, TS, D), tile_of)
    rope = pl.BlockSpec((1, TS, LANES), tile_of)

    hbm = pl.BlockSpec(memory_space=pl.ANY)

    def small_spec(a):
        return pl.BlockSpec((1,) + a.shape[1:], lambda s: (layer, 0, 0))

    def const_spec(a):
        nd = a.ndim
        return pl.BlockSpec(a.shape, lambda s: (0,) * nd)

    gpre, gpost, win, wout, convw, ifb, wup, gab, mnorm, gnorm, rnorm = params
    in_specs = ([tile, rope, rope, small_spec(gpre), small_spec(gpost), hbm, hbm]
                + [small_spec(a) for a in (convw, ifb, wup, gab, mnorm, gnorm, rnorm)]
                + [const_spec(a) for a in consts])
    return pl.pallas_call(
        functools.partial(_mixer_kernel, n_t, layer),
        grid=(n_steps,),
        in_specs=in_specs,
        out_specs=tile,
        out_shape=jax.ShapeDtypeStruct(h.shape, h.dtype),
        scratch_shapes=[
            pltpu.VMEM((P_COLS, D_MODEL), bf16),
            pltpu.VMEM((MIX, D_MODEL), bf16),
            pltpu.VMEM((2, W_CHUNK, D_MODEL), f32),
            pltpu.VMEM((2 * M_HEADS + G_RANK, D_MODEL), f32),
            pltpu.SemaphoreType.DMA((2,)),
            pltpu.SemaphoreType.DMA((2,)),
            pltpu.VMEM((TS, W_G0 - W_M0), f32),
            pltpu.VMEM((TS, W_R0 - W_G0), f32),
            pltpu.VMEM((TS, P_COLS - W_R0), f32),
            pltpu.VMEM((TS + SUBLANES, 2 * M_QK_W), f32),
            pltpu.VMEM((TS, LANES), f32),
            pltpu.VMEM((TS // LC, (N_LEVELS + 2) * LC, LANES), f32),
            pltpu.VMEM((TS, 2 * M_V_W), bf16),
            pltpu.VMEM((TS, D_MODEL), bf16),
            pltpu.VMEM((M_HEADS, M_QK, 2 * M_V), f32),
            pltpu.VMEM((M_HEADS, 1, 1), f32),
            pltpu.VMEM((G_V_W, G_QK_W), f32),
            pltpu.VMEM((R_V_W, R_QK_W), f32),
        ],
        compiler_params=pltpu.CompilerParams(
            dimension_semantics=("arbitrary",), vmem_limit_bytes=VMEM_LIMIT),
        name="mixer_layer",
    )(h, cosf, sins, gpre, gpost, win, wout, convw, ifb, wup, gab, mnorm, gnorm, rnorm, *consts)


N_FF_BLK = D_FF // FF_BLK


def _ffn_kernel(layer, h_ref, gpre_ref, gpost_ref, w1_hbm, w2_hbm, out_ref,
                w1b_ref, w2b_ref, st1_ref, st2_ref, sem1, sem2):
    def w1_copy(j, slot):
        return pltpu.make_async_copy(w1_hbm.at[layer, :, pl.ds(j * FF_BLK, FF_BLK)],
                                     st1_ref.at[slot], sem1.at[slot])

    def w2_copy(j, slot):
        return pltpu.make_async_copy(w2_hbm.at[layer, pl.ds(j * FF_BLK, FF_BLK), :],
                                     st2_ref.at[slot], sem2.at[slot])

    def body(fetch_weights):
        h_in = h_ref[...]
        u = _bf(_rms(h_in, gpre_ref[0]))
        acc = jnp.zeros((TM, D_MODEL), f32)
        if fetch_weights:
            w1_copy(0, 0).start()
            w2_copy(0, 0).start()
        for j in range(N_FF_BLK):
            cols = slice(j * FF_BLK, (j + 1) * FF_BLK)
            if fetch_weights:
                slot = j % 2
                if j + 1 < N_FF_BLK:
                    w1_copy(j + 1, 1 - slot).start()
                    w2_copy(j + 1, 1 - slot).start()
                w1_copy(j, slot).wait()
                w1b_ref[:, cols] = _bf(st1_ref[slot])
                w2_copy(j, slot).wait()
                w2b_ref[cols, :] = _bf(st2_ref[slot])
            hid = _dot(u, w1b_ref[:, cols])
            hid = jnp.square(jnp.maximum(hid, 0.0))
            acc = acc + _dot(_bf(hid), w2b_ref[cols, :])
        out_ref[...] = h_in + _rms(acc, gpost_ref[0])

    @pl.when(pl.program_id(0) == 0)
    def _():
        body(True)

    @pl.when(pl.program_id(0) > 0)
    def _():
        body(False)


def _ffn_layer(layer, h2, gpre, gpost, w1, w2):
    n_tok, D = h2.shape
    tile = pl.BlockSpec((TM, D), lambda i: (i, 0))
    return pl.pallas_call(
        functools.partial(_ffn_kernel, layer),
        grid=(n_tok // TM,),
        in_specs=[tile,
                  pl.BlockSpec((1, 1, D), lambda i: (layer, 0, 0)),
                  pl.BlockSpec((1, 1, D), lambda i: (layer, 0, 0)),
                  pl.BlockSpec(memory_space=pl.ANY),
                  pl.BlockSpec(memory_space=pl.ANY)],
        out_specs=tile,
        out_shape=jax.ShapeDtypeStruct(h2.shape, h2.dtype),
        scratch_shapes=[
            pltpu.VMEM((D, D_FF), bf16),
            pltpu.VMEM((D_FF, D), bf16),
            pltpu.VMEM((2, D, FF_BLK), f32),
            pltpu.VMEM((2, FF_BLK, D), f32),
            pltpu.SemaphoreType.DMA((2,)),
            pltpu.SemaphoreType.DMA((2,)),
        ],
        compiler_params=pltpu.CompilerParams(
            dimension_semantics=("arbitrary",), vmem_limit_bytes=VMEM_LIMIT),
        name="ffn_layer",
    )(h2, gpre, gpost, w1, w2)


def kernel(x, positions, norm_pre_mix, norm_post_mix, norm_pre_ffn, norm_post_ffn, w_in, mlstm_conv_w, mlstm_i_bias, mlstm_f_bias, mlstm_norm, gla_w_up, gla_gate_bias, gla_norm, ret_norm, w_out, w_ff1, w_ff2):
    B, S, D = x.shape
    t_np, m_np = _chunk_constants()
    blk_np, seg_np = _head_constants()
    consts = ((jnp.asarray(t_np, bf16), jnp.asarray(m_np, bf16))
              + tuple(jnp.asarray(a) for a in _ret_constants())
              + (jnp.asarray(blk_np), jnp.asarray(seg_np, bf16)))

    cosf, sins = _rope_tables(positions)

    ifb = jnp.concatenate([mlstm_i_bias, mlstm_f_bias,
                           jnp.zeros((DEPTH, LANES - 2 * M_HEADS), f32)], axis=-1)
    wup = jnp.zeros((DEPTH, LANES, G_QK_W), f32).at[:, GATE_A:GATE_A + G_RANK, :].set(gla_w_up)
    row = lambda a: a[:, None, :]
    params = (row(norm_pre_mix), row(norm_post_mix), jnp.transpose(w_in, (0, 2, 1)), w_out,
              mlstm_conv_w, row(ifb), wup.astype(bf16), row(gla_gate_bias),
              row(mlstm_norm), row(gla_norm), row(ret_norm))
    gpre_f, gpost_f = row(norm_pre_ffn), row(norm_post_ffn)

    h = x
    for l in range(DEPTH):
        h = _mixer_layer(l, h, cosf, sins, params, consts)
        h = _ffn_layer(l, h.reshape(B * S, D), gpre_f, gpost_f, w_ff1, w_ff2).reshape(B, S, D)
    return h
```

```python
import functools

import numpy as np
import jax
import jax.numpy as jnp
from jax import lax
from jax.experimental import pallas as pl
from jax.experimental.pallas import tpu as pltpu

D_MODEL = 1024
DEPTH = 4
EPS = 1e-6
M_HEADS, M_V, M_QK, M_CONV = 4, 128, 64, 4
G_HEADS, G_V, G_QK, G_RANK = 4, 64, 32, 16
G_NORMALIZER = 16.0
R_HEADS, R_V, R_QK = 4, 64, 32
ROPE_BASE = 10000.0
D_FF = 4 * D_MODEL

M_QK_W, M_V_W = M_HEADS * M_QK, M_HEADS * M_V
G_QK_W, G_V_W = G_HEADS * G_QK, G_HEADS * G_V
R_QK_W, R_V_W = R_HEADS * R_QK, R_HEADS * R_V
IN_SIZES = (M_QK_W, M_QK_W, M_V_W, M_HEADS, M_HEADS, M_V_W,
            G_QK_W, G_QK_W, G_V_W, G_RANK, G_V_W,
            R_QK_W, R_QK_W, R_V_W, R_V_W)
IN_OFFS = tuple(int(v) for v in np.cumsum((0,) + IN_SIZES))

LANES = 128
SUBLANES = 8
C_MQK = 0
C_MV = C_MQK + 2 * M_QK_W
C_GATE = C_MV + M_V_W
C_MO = C_GATE + LANES
C_GQ = 0
C_GK = C_GQ + G_QK_W
C_GV = C_GK + G_QK_W
C_GG = C_GV + G_V_W
C_RQ = 0
C_RK = C_RQ + R_QK_W
C_RV = C_RK + R_QK_W
C_RG = C_RV + R_V_W
W_M0 = 0
W_G0 = C_MO + M_V_W
W_R0 = W_G0 + C_GG + G_V_W
P_COLS = W_R0 + C_RG + R_V_W
MIX = M_V_W + G_V_W + R_V_W
GATE_I, GATE_F, GATE_A = 0, M_HEADS, 2 * M_HEADS

LC = 128
N_LEVELS = 7
TS = 512
TM = 512
FF_BLK = 1024
VMEM_LIMIT = 56 * 1024 * 1024

LOG2E = 1.4426950408889634
_NT = (((1,), (1,)), ((), ()))
f32 = jnp.float32
bf16 = jnp.bfloat16


def _bf(x):
    return x.astype(bf16)


def _dot(a, b):
    return jnp.dot(a, b, preferred_element_type=f32)


def _dot_nt(a, b):
    return lax.dot_general(a, b, _NT, preferred_element_type=f32)


def _rms(x, g):
    return x * lax.rsqrt(jnp.mean(x * x, axis=-1, keepdims=True) + EPS) * g


def _log_sigmoid(x):
    return jnp.minimum(x, 0.0) - jnp.log(1.0 + jnp.exp(-jnp.abs(x)))


def _sigmoid(x):
    return 0.5 * jnp.tanh(0.5 * x) + 0.5


def _silu(x):
    half = 0.5 * x
    return half + half * jnp.tanh(half)


def _split3(x):
    hi = _bf(x)
    r1 = x - hi.astype(f32)
    mid = _bf(r1)
    lo = _bf(r1 - mid.astype(f32))
    return jnp.concatenate([hi, mid, lo], axis=0)


def _split2(x):
    hi = _bf(x)
    return jnp.concatenate([hi, _bf(x - hi.astype(f32))], axis=0)


def _chunk_constants():
    L = LC
    T = np.zeros((N_LEVELS + 2, L, L), np.float32)
    M = np.zeros((N_LEVELS + 1, L, L), np.float32)
    idx = np.arange(L)
    for li in range(N_LEVELS):
        s = L >> (li + 1)
        for i in range(L):
            seg = i // s
            if seg % 2 == 1:
                T[li, i, seg * s:i + 1] = 1.0
            else:
                T[li, i, i + 1:seg * s + s] = 1.0
        same_blk = (idx[:, None] // (2 * s)) == (idx[None, :] // (2 * s))
        M[li] = same_blk & ((idx[:, None] // s) % 2 == 1) & ((idx[None, :] // s) % 2 == 0)
    M[N_LEVELS] = np.eye(L)
    for i in range(L):
        T[N_LEVELS, i, :i + 1] = 1.0
        T[N_LEVELS + 1, i, i + 1:] = 1.0
    T = T.reshape((N_LEVELS + 2) * L, L)
    return np.concatenate([T, T, T], axis=1), M


def _ret_constants():
    L = LC
    log_gamma = np.log1p(-np.exp2(-5.0 - np.arange(R_HEADS, dtype=np.float64)))
    i = np.arange(L)
    rel = (i[:, None] - i[None, :]).astype(np.float64)
    D = np.where(rel[None] >= 0, np.exp(log_gamma[:, None, None] * rel[None]), 0.0)
    lg_lane = log_gamma[np.repeat(np.arange(R_HEADS), R_QK)][None, :]
    qdec = np.exp(lg_lane * (i[:, None] + 1.0))
    kdec = np.exp(lg_lane * (L - 1.0 - i[:, None]))
    sdec = np.exp(lg_lane * L)
    return tuple(a.astype(np.float32) for a in (D, qdec, kdec, sdec))


def _head_constants():
    blk = (np.arange(G_V_W)[:, None] // G_V) == (np.arange(G_QK_W)[None, :] // G_QK)
    seg = ((np.arange(G_V_W)[:, None] // G_V) == (np.arange(G_V_W)[None, :] // G_V)) / float(G_V)
    return blk.astype(np.float32), seg.astype(np.float32)


def _rope_kernel(pos_ref, invf_ref, sgn_ref, cos_ref, sin_ref):
    ang = pos_ref[0].astype(f32) * invf_ref[...]
    cos_ref[0] = jnp.cos(ang)
    sin_ref[0] = jnp.sin(ang) * sgn_ref[...]


def _rope_tables(positions):
    B, S = positions.shape
    r = np.arange(LANES) % R_QK
    inv_freq = ROPE_BASE ** (-jnp.arange(0, R_QK, 2, dtype=f32) / R_QK)
    invf = jnp.tile(inv_freq, LANES // (R_QK // 2))[None, :]
    sgn = jnp.asarray(np.where(r < R_QK // 2, -1.0, 1.0).astype(np.float32)[None, :])
    return pl.pallas_call(
        _rope_kernel,
        grid=(B,),
        in_specs=[pl.BlockSpec((1, S, 1), lambda b: (b, 0, 0)),
                  pl.BlockSpec((1, LANES), lambda b: (0, 0)),
                  pl.BlockSpec((1, LANES), lambda b: (0, 0))],
        out_specs=[pl.BlockSpec((1, S, LANES), lambda b: (b, 0, 0)),
                   pl.BlockSpec((1, S, LANES), lambda b: (b, 0, 0))],
        out_shape=[jax.ShapeDtypeStruct((B, S, LANES), f32)] * 2,
        name="rope_tables",
    )(positions[:, :, None], invf, sgn)


W_CHUNK = 512


def _mixer_kernel(n_t, layer, hc_ref, cos_ref, sin_ref, gpre_ref, gpost_ref, win_hbm, wout_hbm,
                  convw_ref, ifb_ref, wup_ref, gab_ref, mnorm_ref, gnorm_ref, rnorm_ref,
                  t_ref, lmb_ref, d_ref, qdec_ref, kdec_ref, sdec_ref, blk_ref, seg_ref,
                  out_ref,
                  win_ref, wout_ref, stg_ref, gstg_ref, wsem, gsem,
                  pm_ref, pg_ref, pr_ref, qkpad_ref, la_ref, e_ref, vext_ref, y_ref,
                  mc_ref, mm_ref, gs_ref, rs_ref):
    step = pl.program_id(0)
    rm_ref, rg_ref, rr_ref = pm_ref, pg_ref, pr_ref

    def stream_weights():
        moves = []
        for first, last, dst0 in ((0, 2, C_MQK), (5, 8, C_MO), (10, 14, W_G0 + C_GG)):
            for k in range(0, IN_OFFS[last + 1] - IN_OFFS[first], W_CHUNK):
                moves.append((win_hbm.at[layer, pl.ds(IN_OFFS[first] + k, W_CHUNK), :], win_ref, dst0 + k))
        for k in range(0, D_MODEL, W_CHUNK):
            moves.append((wout_hbm.at[layer, pl.ds(k, W_CHUNK), :], wout_ref, k))

        def chunk_copy(i):
            return pltpu.make_async_copy(moves[i][0], stg_ref.at[i % 2], wsem.at[i % 2])

        n_if = IN_OFFS[5] - IN_OFFS[3]
        gate_copies = [
            pltpu.make_async_copy(win_hbm.at[layer, pl.ds(IN_OFFS[3], n_if), :],
                                  gstg_ref.at[pl.ds(0, n_if), :], gsem.at[0]),
            pltpu.make_async_copy(win_hbm.at[layer, pl.ds(IN_OFFS[9], G_RANK), :],
                                  gstg_ref.at[pl.ds(n_if, G_RANK), :], gsem.at[1])]
        for cp in gate_copies:
            cp.start()
        chunk_copy(0).start()
        for i in range(len(moves)):
            if i + 1 < len(moves):
                chunk_copy(i + 1).start()
            chunk_copy(i).wait()
            _, dst_ref, r0 = moves[i]
            dst_ref[r0:r0 + W_CHUNK, :] = _bf(stg_ref[i % 2])
        for cp in gate_copies:
            cp.wait()
        used = n_if + G_RANK
        gate = jnp.concatenate([gstg_ref[0:used, :], jnp.zeros((LANES - used, D_MODEL), f32)], axis=0)
        win_ref[C_GATE:C_GATE + LANES, :] = _bf(gate)

    def preprocess_mlstm():
        qkpad_ref[SUBLANES:SUBLANES + TS, :] = rm_ref[:, C_MQK:C_MQK + 2 * M_QK_W]
        first = SUBLANES - (M_CONV - 1)
        conv = jnp.zeros((TS, 2 * M_QK_W), f32)
        for j in range(M_CONV):
            conv = conv + convw_ref[0, j:j + 1, :] * qkpad_ref[first + j:first + j + TS, :]
        qkpad_ref[0:SUBLANES, :] = qkpad_ref[TS:TS + SUBLANES, :]
        qk = _silu(conv)
        lane_qk = lax.broadcasted_iota(jnp.int32, (1, 2 * M_QK_W), 1)
        pm_ref[:, C_MQK:C_MQK + 2 * M_QK_W] = qk * jnp.where(lane_qk < M_QK_W, 1.0, M_QK ** -0.5)
        for hd in range(M_HEADS):
            vext_ref[:, 2 * hd * M_V:(2 * hd + 1) * M_V] = _bf(
                rm_ref[:, C_MV + hd * M_V:C_MV + (hd + 1) * M_V])
        graw = rm_ref[:, C_GATE:C_GATE + LANES]
        gb = graw + ifb_ref[0]
        lane_g = lax.broadcasted_iota(jnp.int32, (TS, LANES), 1)
        pm_ref[:, C_GATE:C_GATE + LANES] = jnp.where(lane_g < GATE_F, gb, _log_sigmoid(gb)) * LOG2E
        a_pre = _dot(_bf(graw), wup_ref[0]) + gab_ref[0]
        la_ref[...] = _log_sigmoid(a_pre) * (LOG2E / G_NORMALIZER)

    def preprocess_gla_ret(cosf, sins):
        lane_g = lax.broadcasted_iota(jnp.int32, (TS, LANES), 1)
        pg_ref[:, C_GQ:C_GQ + G_QK_W] = rg_ref[:, C_GQ:C_GQ + G_QK_W] * (G_QK ** -0.5)
        first_half = (lane_g % R_QK) < (R_QK // 2)
        for col, scale in ((C_RQ, 1.0), (C_RK, R_QK ** -0.5)):
            xr = rr_ref[:, col:col + R_QK_W]
            swapped = jnp.where(first_half, pltpu.roll(xr, LANES - R_QK // 2, axis=1),
                                pltpu.roll(xr, R_QK // 2, axis=1))
            pr_ref[:, col:col + R_QK_W] = (xr * cosf + swapped * sins) * scale

    @pl.when(step == 0)
    def _():
        stream_weights()
        for hd in range(M_HEADS):
            vext_ref[:, (2 * hd + 1) * M_V:(2 * hd + 2) * M_V] = jnp.ones((TS, M_V), bf16)

    @pl.when(step % n_t == 0)
    def _():
        qkpad_ref[0:SUBLANES, :] = jnp.zeros((SUBLANES, 2 * M_QK_W), f32)
        mc_ref[...] = jnp.zeros_like(mc_ref)
        mm_ref[...] = jnp.zeros_like(mm_ref)
        gs_ref[...] = jnp.zeros_like(gs_ref)
        rs_ref[...] = jnp.zeros_like(rs_ref)

    h_in = hc_ref[0]
    u = _bf(_rms(h_in, gpre_ref[0]))
    rm_ref[...] = _dot_nt(u, win_ref[W_M0:W_G0, :])
    preprocess_mlstm()

    row_i = lax.broadcasted_iota(jnp.int32, (LC, LC), 0)
    col_j = lax.broadcasted_iota(jnp.int32, (LC, LC), 1)
    causal = row_i >= col_j
    top_rows = row_i < M_QK
    lane1 = lax.broadcasted_iota(jnp.int32, (1, LANES), 1)
    pair_mask = [(lane1 < M_QK).astype(f32), (lane1 >= M_QK).astype(f32)]
    pair_mask_b = [m.astype(bf16) for m in pair_mask]
    head_mask_b = [((lane1 // G_QK) == hd).astype(bf16) for hd in range(G_HEADS)]

    def loader(ref, row_slice, col0, width):
        return lambda: ref[row_slice, col0:col0 + width]

    def masked_rhs(kb):
        return jnp.concatenate([kb * m for m in head_mask_b], axis=0)

    def pair_tiles(p_heads, vb):
        tiles = []
        for tp in range(2):
            vt = vb[:, tp * LANES:(tp + 1) * LANES]
            tiles.append(_dot(jnp.concatenate([p_heads[2 * tp], p_heads[2 * tp + 1]], axis=1),
                              jnp.concatenate([vt * pair_mask_b[0], vt * pair_mask_b[1]], axis=0)))
        return jnp.concatenate(tiles, axis=1)

    def head_norm_gate(hcat, gain, gate):
        ms = _dot(_bf(hcat * hcat), seg_ref[...])
        return _bf(hcat * lax.rsqrt(ms + EPS) * gain * _silu(gate))

    n_ch = TS // LC
    chunks = range(n_ch)
    heads = range(M_HEADS)
    rows = [slice(c * LC, (c + 1) * LC) for c in chunks]
    t_cum = t_ref[N_LEVELS * LC:(N_LEVELS + 1) * LC, :]

    gblk, bcum, qm, s_m, s_r, m_k, gq, gk, rq, rk = ([] for _ in range(10))
    for c in range(0, n_ch, 2):
        gblk += [pm_ref[rows[c + i], C_GATE:C_GATE + LANES] for i in range(2)]
        cum2 = _dot(t_cum, jnp.concatenate([_split3(gblk[c]), _split3(gblk[c + 1])], axis=1))
        bcum += [cum2[:, :LANES], cum2[:, LANES:]]
        e2 = _dot(t_ref[:, :2 * LC], jnp.concatenate(
            [_split2(la_ref[rows[c], :]), _split2(la_ref[rows[c + 1], :])], axis=1))
        e_ref[c] = e2[:, :LANES]
        e_ref[c + 1] = e2[:, LANES:]
    for c in chunks:
        m_q = [loader(pm_ref, rows[c], C_MQK + tp * LANES, LANES) for tp in range(2)]
        m_k.append([loader(pm_ref, rows[c], C_MQK + M_QK_W + tp * LANES, LANES) for tp in range(2)])
        qm.append([m_q[hd // 2] for hd in heads])
        s_m.append([])
        for tp in range(2):
            kb = _bf(m_k[c][tp]())
            s2 = _dot_nt(_bf(m_q[tp]()), jnp.concatenate([kb * pair_mask_b[0], kb * pair_mask_b[1]], axis=0))
            s_m[c] += [s2[:, :LC], s2[:, LC:]]

    rg_ref[...] = _dot_nt(u, win_ref[W_G0:W_R0, :])
    rr_ref[...] = _dot_nt(u, win_ref[W_R0:P_COLS, :])

    g_row, b_col, rmax = [], [], []
    for c in chunks:
        xg = jnp.where(col_j < GATE_F, gblk[c], bcum[c])
        xgt = xg.T
        g_row.append([xgt[GATE_I + hd:GATE_I + hd + 1, :] - xgt[GATE_F + hd:GATE_F + hd + 1, :]
                      for hd in heads])
        b_col.append([xg[:, GATE_F + hd:GATE_F + hd + 1] for hd in heads])
        rmax.append([jnp.max(jnp.where(causal, g_row[c][hd], -jnp.inf), axis=1, keepdims=True)
                     for hd in heads])

    m_run = [mm_ref[hd] for hd in heads]
    m_prev, s_old, w_row = [], [], []
    for c in chunks:
        m_prev.append(list(m_run))
        s_old.append([])
        w_row.append([])
        for hd in heads:
            g_max = rmax[c][hd][LC - 1:LC, :]
            b_last = b_col[c][hd][LC - 1:LC, :]
            a_max = b_last + g_max
            m_new = jnp.maximum(b_last + m_run[hd], a_max)
            s_old[c].append(jnp.exp2(b_last + m_run[hd] - m_new))
            w_row[c].append(jnp.exp2(g_row[c][hd] - g_max) * jnp.exp2(a_max - m_new))
            m_run[hd] = m_new
    for hd in heads:
        mm_ref[hd] = m_run[hd]

    mu = [[jnp.maximum(rmax[c][hd], m_prev[c][hd]) for hd in heads] for c in chunks]
    p_m = []
    for c in chunks:
        p_m.append([_bf(jnp.exp2(jnp.where(causal, g_row[c][hd], -jnp.inf) - mu[c][hd]) * s_m[c][hd])
                    for hd in heads])

    vexts, u_m = [], []
    for c in chunks:
        vext = [vext_ref[rows[c], 2 * hd * M_V:(2 * hd + 2) * M_V] for hd in heads]
        vexts.append(vext)
        u_m.append([])
        for tp in range(2):
            kw_t = _bf(m_k[c][tp]().T * jnp.where(top_rows, w_row[c][2 * tp], w_row[c][2 * tp + 1]))
            for e in range(2):
                u_m[c].append(_dot(kw_t[e * M_QK:(e + 1) * M_QK], vext[2 * tp + e]))
    c_st = [mc_ref[hd] for hd in heads]
    zero_half = jnp.zeros((M_QK, 2 * M_V), bf16)
    ne_m = []
    for c in chunks:
        ne_m.append([])
        for hd in heads:
            cb = _bf(c_st[hd])
            padded = jnp.concatenate([cb, zero_half] if hd % 2 == 0 else [zero_half, cb], axis=0)
            q_in = _bf(qm[c][hd]() * pair_mask[hd % 2] * jnp.exp2(m_prev[c][hd] - mu[c][hd]))
            ne_m[c].append(_dot(jnp.concatenate([p_m[c][hd], q_in], axis=1),
                                jnp.concatenate([vexts[c][hd], padded], axis=0)))
            c_st[hd] = s_old[c][hd] * c_st[hd] + u_m[c][hd]
    for hd in heads:
        mc_ref[hd] = c_st[hd]
    for c in chunks:
        for tp in range(2):
            hm = []
            for hd in (2 * tp, 2 * tp + 1):
                ne = ne_m[c][hd]
                num = ne[:, :M_V]
                den = ne[:, M_V:]
                h1 = num / jnp.maximum(jnp.abs(den), jnp.exp2(-(b_col[c][hd] + mu[c][hd])))
                hm.append(h1 * lax.rsqrt(jnp.mean(h1 * h1, axis=-1, keepdims=True) + EPS))
            hm = jnp.concatenate(hm, axis=1)
            cols = slice(2 * tp * M_V, (2 * tp + 2) * M_V)
            o_gate = _sigmoid(pm_ref[rows[c], C_MO + 2 * tp * M_V:C_MO + (2 * tp + 2) * M_V])
            y_ref[rows[c], cols] = _bf(hm * mnorm_ref[0, :, cols] * o_gate)

    preprocess_gla_ret(cos_ref[0], sin_ref[0])
    for c in chunks:
        rq.append(loader(pr_ref, rows[c], C_RQ, R_QK_W))
        rk.append(loader(pr_ref, rows[c], C_RK, R_QK_W))
        gq.append(loader(pg_ref, rows[c], C_GQ, G_QK_W))
        gk.append(loader(pg_ref, rows[c], C_GK, G_QK_W))

    RB = 16
    n_rb = LC // RB

    def add_level(acc_c, x, li, blocks):
        for hd in range(G_HEADS):
            for n, b in enumerate(blocks):
                term = (lmb_ref[li, b * RB:(b + 1) * RB, :]
                        * _bf(x[n * RB:(n + 1) * RB, hd * LC:(hd + 1) * LC]))
                acc_c[hd][b] = term if acc_c[hd][b] is None else acc_c[hd][b] + term

    acc = []
    for c in chunks:
        acc.append([[None] * n_rb for _ in range(G_HEADS)])
        add_level(acc[c], _dot_nt(_bf(gq[c]()), masked_rhs(_bf(gk[c]()))), N_LEVELS, list(range(n_rb)))
    for li in range(N_LEVELS):
        seg = LC >> (li + 1)
        if seg >= RB:
            blocks = [b for b in range(n_rb) if ((b * RB) // seg) % 2 == 1]
        else:
            blocks = list(range(n_rb))
        for c in chunks:
            pw = jnp.exp2(e_ref[c, li * LC:(li + 1) * LC, :])
            qs = _bf(gq[c]() * pw)
            if len(blocks) < n_rb:
                qs = jnp.concatenate([qs[b * RB:(b + 1) * RB] for b in blocks], axis=0)
            add_level(acc[c], _dot_nt(qs, masked_rhs(_bf(gk[c]() * pw))), li, blocks)
    acc = [[jnp.concatenate(acc[c][hd], axis=0) for hd in range(G_HEADS)] for c in chunks]

    intra_g, u_g, qin_g, dec_g, intra_r, u_r, qin_r = ([] for _ in range(7))
    for c in chunks:
        s_r = _dot_nt(_bf(rq[c]()), masked_rhs(_bf(rk[c]())))
        p_ret = [_bf(d_ref[hd] * s_r[:, hd * LC:(hd + 1) * LC]) for hd in range(R_HEADS)]
        rv = pr_ref[rows[c], C_RV:C_RV + R_V_W]
        intra_r.append(pair_tiles(p_ret, _bf(rv)))
        u_r.append(blk_ref[...] * _dot(_bf(rv.T), _bf(rk[c]() * kdec_ref[...])))
        qin_r.append(_bf(rq[c]() * qdec_ref[...]))
        pb = jnp.exp2(e_ref[c, N_LEVELS * LC:(N_LEVELS + 1) * LC, :])
        pk = jnp.exp2(e_ref[c, (N_LEVELS + 1) * LC:(N_LEVELS + 2) * LC, :])
        gv = pg_ref[rows[c], C_GV:C_GV + G_V_W]
        intra_g.append(pair_tiles(acc[c], _bf(gv)))
        u_g.append(blk_ref[...] * _dot(_bf(gv.T), _bf(gk[c]() * pk)))
        qin_g.append(_bf(gq[c]() * pb))
        dec_g.append(pb[LC - 1:LC, :])

    g_st = gs_ref[...]
    r_st = rs_ref[...]
    inter_g, inter_r = [], []
    for c in chunks:
        inter_g.append(_dot_nt(qin_g[c], _bf(g_st)))
        g_st = g_st * dec_g[c] + u_g[c]
        inter_r.append(_dot_nt(qin_r[c], _bf(r_st)))
        r_st = r_st * sdec_ref[...] + u_r[c]
    gs_ref[...] = g_st
    rs_ref[...] = r_st

    for c in chunks:
        y_ref[rows[c], M_V_W:M_V_W + G_V_W] = head_norm_gate(
            intra_g[c] + inter_g[c], gnorm_ref[0], pg_ref[rows[c], C_GG:C_GG + G_V_W])
        y_ref[rows[c], M_V_W + G_V_W:M_V_W + G_V_W + R_V_W] = head_norm_gate(
            intra_r[c] + inter_r[c], rnorm_ref[0], pr_ref[rows[c], C_RG:C_RG + R_V_W])

    mixed = _dot(y_ref[...], wout_ref[...])
    out_ref[0] = h_in + _rms(mixed, gpost_ref[0])


def _mixer_layer(layer, h, cosf, sins, params, consts):
    B, S, D = h.shape
    n_t = S // TS
    n_steps = B * n_t

    def tile_of(step):
        return step // n_t, step % n_t, 0

    tile = pl.BlockSpec((1, TS, D), tile_of)
    rope = pl.BlockSpec((1, TS, LANES), tile_of)

    hbm = pl.BlockSpec(memory_space=pl.ANY)

    def small_spec(a):
        return pl.BlockSpec((1,) + a.shape[1:], lambda s: (layer, 0, 0))

    def const_spec(a):
        nd = a.ndim
        return pl.BlockSpec(a.shape, lambda s: (0,) * nd)

    gpre, gpost, win, wout, convw, ifb, wup, gab, mnorm, gnorm, rnorm = params
    in_specs = ([tile, rope, rope, small_spec(gpre), small_spec(gpost), hbm, hbm]
                + [small_spec(a) for a in (convw, ifb, wup, gab, mnorm, gnorm, rnorm)]
                + [const_spec(a) for a in consts])
    return pl.pallas_call(
        functools.partial(_mixer_kernel, n_t, layer),
        grid=(n_steps,),
        in_specs=in_specs,
        out_specs=tile,
        out_shape=jax.ShapeDtypeStruct(h.shape, h.dtype),
        scratch_shapes=[
            pltpu.VMEM((P_COLS, D_MODEL), bf16),
            pltpu.VMEM((MIX, D_MODEL), bf16),
            pltpu.VMEM((2, W_CHUNK, D_MODEL), f32),
            pltpu.VMEM((2 * M_HEADS + G_RANK, D_MODEL), f32),
            pltpu.SemaphoreType.DMA((2,)),
            pltpu.SemaphoreType.DMA((2,)),
            pltpu.VMEM((TS, W_G0 - W_M0), f32),
            pltpu.VMEM((TS, W_R0 - W_G0), f32),
            pltpu.VMEM((TS, P_COLS - W_R0), f32),
            pltpu.VMEM((TS + SUBLANES, 2 * M_QK_W), f32),
            pltpu.VMEM((TS, LANES), f32),
            pltpu.VMEM((TS // LC, (N_LEVELS + 2) * LC, LANES), f32),
            pltpu.VMEM((TS, 2 * M_V_W), bf16),
            pltpu.VMEM((TS, D_MODEL), bf16),
            pltpu.VMEM((M_HEADS, M_QK, 2 * M_V), f32),
            pltpu.VMEM((M_HEADS, 1, 1), f32),
            pltpu.VMEM((G_V_W, G_QK_W), f32),
            pltpu.VMEM((R_V_W, R_QK_W), f32),
        ],
        compiler_params=pltpu.CompilerParams(
            dimension_semantics=("arbitrary",), vmem_limit_bytes=VMEM_LIMIT),
        name="mixer_layer",
    )(h, cosf, sins, gpre, gpost, win, wout, convw, ifb, wup, gab, mnorm, gnorm, rnorm, *consts)


N_FF_BLK = D_FF // FF_BLK


def _ffn_kernel(layer, h_ref, gpre_ref, gpost_ref, w1_hbm, w2_hbm, out_ref,
                w1b_ref, w2b_ref, st1_ref, st2_ref, sem1, sem2):
    def w1_copy(j, slot):
        return pltpu.make_async_copy(w1_hbm.at[layer, :, pl.ds(j * FF_BLK, FF_BLK)],
                                     st1_ref.at[slot], sem1.at[slot])

    def w2_copy(j, slot):
        return pltpu.make_async_copy(w2_hbm.at[layer, pl.ds(j * FF_BLK, FF_BLK), :],
                                     st2_ref.at[slot], sem2.at[slot])

    def body(fetch_weights):
        h_in = h_ref[...]
        u = _bf(_rms(h_in, gpre_ref[0]))
        acc = jnp.zeros((TM, D_MODEL), f32)
        if fetch_weights:
            w1_copy(0, 0).start()
            w2_copy(0, 0).start()
        for j in range(N_FF_BLK):
            cols = slice(j * FF_BLK, (j + 1) * FF_BLK)
            if fetch_weights:
                slot = j % 2
                if j + 1 < N_FF_BLK:
                    w1_copy(j + 1, 1 - slot).start()
                    w2_copy(j + 1, 1 - slot).start()
                w1_copy(j, slot).wait()
                w1b_ref[:, cols] = _bf(st1_ref[slot])
                w2_copy(j, slot).wait()
                w2b_ref[cols, :] = _bf(st2_ref[slot])
            hid = _dot(u, w1b_ref[:, cols])
            hid = jnp.square(jnp.maximum(hid, 0.0))
            acc = acc + _dot(_bf(hid), w2b_ref[cols, :])
        out_ref[...] = h_in + _rms(acc, gpost_ref[0])

    @pl.when(pl.program_id(0) == 0)
    def _():
        body(True)

    @pl.when(pl.program_id(0) > 0)
    def _():
        body(False)


def _ffn_layer(layer, h2, gpre, gpost, w1, w2):
    n_tok, D = h2.shape
    tile = pl.BlockSpec((TM, D), lambda i: (i, 0))
    return pl.pallas_call(
        functools.partial(_ffn_kernel, layer),
        grid=(n_tok // TM,),
        in_specs=[tile,
                  pl.BlockSpec((1, 1, D), lambda i: (layer, 0, 0)),
                  pl.BlockSpec((1, 1, D), lambda i: (layer, 0, 0)),
                  pl.BlockSpec(memory_space=pl.ANY),
                  pl.BlockSpec(memory_space=pl.ANY)],
        out_specs=tile,
        out_shape=jax.ShapeDtypeStruct(h2.shape, h2.dtype),
        scratch_shapes=[
            pltpu.VMEM((D, D_FF), bf16),
            pltpu.VMEM((D_FF, D), bf16),
            pltpu.VMEM((2, D, FF_BLK), f32),
            pltpu.VMEM((2, FF_BLK, D), f32),
            pltpu.SemaphoreType.DMA((2,)),
            pltpu.SemaphoreType.DMA((2,)),
        ],
        compiler_params=pltpu.CompilerParams(
            dimension_semantics=("arbitrary",), vmem_limit_bytes=VMEM_LIMIT),
        name="ffn_layer",
    )(h2, gpre, gpost, w1, w2)


def kernel(x, positions, norm_pre_mix, norm_post_mix, norm_pre_ffn, norm_post_ffn, w_in, mlstm_conv_w, mlstm_i_bias, mlstm_f_bias, mlstm_norm, gla_w_up, gla_gate_bias, gla_norm, ret_norm, w_out, w_ff1, w_ff2):
    B, S, D = x.shape
    t_np, m_np = _chunk_constants()
    blk_np, seg_np = _head_constants()
    consts = ((jnp.asarray(t_np, bf16), jnp.asarray(m_np, bf16))
              + tuple(jnp.asarray(a) for a in _ret_constants())
              + (jnp.asarray(blk_np), jnp.asarray(seg_np, bf16)))

    cosf, sins = _rope_tables(positions)

    ifb = jnp.concatenate([mlstm_i_bias, mlstm_f_bias,
                           jnp.zeros((DEPTH, LANES - 2 * M_HEADS), f32)], axis=-1)
    wup = jnp.zeros((DEPTH, LANES, G_QK_W), f32).at[:, GATE_A:GATE_A + G_RANK, :].set(gla_w_up)
    row = lambda a: a[:, None, :]
    params = (row(norm_pre_mix), row(norm_post_mix), jnp.transpose(w_in, (0, 2, 1)), w_out,
              mlstm_conv_w, row(ifb), wup.astype(bf16), row(gla_gate_bias),
              row(mlstm_norm), row(gla_norm), row(ret_norm))
    gpre_f, gpost_f = row(norm_pre_ffn), row(norm_post_ffn)

    h = x
    for l in range(DEPTH):
        h = _mixer_layer(l, h, cosf, sins, params, consts)
        h = _ffn_layer(l, h.reshape(B * S, D), gpre_f, gpost_f, w_ff1, w_ff2).reshape(B, S, D)
    return h
```
